```python
import jax, jax.numpy as jnp
from jax import lax
import numpy as np

D_MODEL = 1024
BATCH = 16
SEQ = 2048
DEPTH = 2
DEC_BATCH = 8
DEC_SEQ = 64
PAST_LEN = 4096

CHUNK = 64
N_A = DEPTH // 2
N_B = DEPTH - N_A
K_A = 128
H_A = D_MODEL // K_A
V_A = D_MODEL // H_A
D_A = H_A * K_A
H_B = 16
D_HB = D_MODEL // H_B
D_FF = 4 * D_MODEL
Q_BLOCK = 128
EPS = 1e-6

kernel_name = "yoco_hgrn2_fox_stream_step"


def _rmsnorm(x, g):
    xf = x.astype(jnp.float32)
    y = xf * lax.rsqrt(jnp.mean(xf * xf, axis=-1, keepdims=True) + EPS)
    return (y * g.astype(jnp.float32)).astype(x.dtype)


def _mlp(xn, w_up, w_down):
    h = jax.nn.relu(xn @ w_up)
    return (h * h) @ w_down


def _gla_chunked(q, k, v, logf, s0, chunk):
    B, T, H, K = q.shape
    V = v.shape[-1]
    N = T // chunk
    q = q.reshape(B, N, chunk, H, K)
    k = k.reshape(B, N, chunk, H, K)
    v = v.reshape(B, N, chunk, H, V)
    bc = jnp.cumsum(logf.reshape(B, N, chunk, H, K), axis=2)
    btot = bc[:, :, -1]
    q_dec = q * jnp.exp(bc)
    k_inv = k * jnp.exp(-bc)
    k_end = k * jnp.exp(btot[:, :, None] - bc)
    mask = jnp.tril(jnp.ones((chunk, chunk), dtype=bool))
    scores = jnp.einsum('bnthk,bnshk->bnhts', q_dec, k_inv)
    scores = jnp.where(mask, scores, 0.0)
    o_intra = jnp.einsum('bnhts,bnshv->bnthv', scores, v)
    u = jnp.einsum('bnshk,bnshv->bnhkv', k_end, v)
    decay = jnp.exp(btot)

    def step(S, inp):
        d, un = inp
        return d[..., None] * S + un, S

    s_fin, s_before = lax.scan(step, s0, (jnp.moveaxis(decay, 1, 0), jnp.moveaxis(u, 1, 0)))
    s_before = jnp.moveaxis(s_before, 0, 1)
    o_inter = jnp.einsum('bnthk,bnhkv->bnthv', q_dec, s_before)
    return (o_intra + o_inter).reshape(B, T, H, V), s_fin


def _hgrn2(xn, w_in, lb, g_norm, w_o, s0, chunk):
    B, T, _ = xn.shape
    p = (xn @ w_in).astype(jnp.float32)
    q, zf, i, g = jnp.split(p, [D_A, 2 * D_A, 3 * D_A], axis=-1)
    lb = lb.reshape(H_A, K_A)
    q = jax.nn.silu(q).reshape(B, T, H_A, K_A)
    zf = zf.reshape(B, T, H_A, K_A)
    f = lb + (1.0 - lb) * jax.nn.sigmoid(zf)
    logf = jnp.log(f)
    k = (1.0 - lb) * jax.nn.sigmoid(-zf)
    i = i.reshape(B, T, H_A, V_A)
    o, s_fin = _gla_chunked(q, k, i, logf, s0, chunk)
    o = o * lax.rsqrt(jnp.mean(o * o, axis=-1, keepdims=True) + EPS) * g_norm.astype(jnp.float32)
    o = o * jax.nn.sigmoid(g.reshape(B, T, H_A, V_A))
    return o.reshape(B, T, H_A * V_A).astype(xn.dtype) @ w_o, s_fin


def _shared_kv(h, norm_kv, w_kv, b_f):
    B, T, _ = h.shape
    p = _rmsnorm(h, norm_kv) @ w_kv
    hd = H_B * D_HB
    k = p[..., :hd].reshape(B, T, H_B, D_HB)
    v = p[..., hd:2 * hd].reshape(B, T, H_B, D_HB)
    logf = jax.nn.log_sigmoid(p[..., 2 * hd:].astype(jnp.float32) + b_f.astype(jnp.float32))
    return k, v, logf


def _fox_attend(q, k, v, F_q, F_k, q_pos, k_pos):
    B, T, H, Dh = q.shape
    qb_len = min(Q_BLOCK, T)
    nb = T // qb_len
    scale = Dh ** -0.5
    qb = jnp.moveaxis(q.reshape(B, nb, qb_len, H, Dh), 1, 0)
    Fb = jnp.moveaxis(F_q.reshape(B, nb, qb_len, H), 1, 0)
    pb = q_pos.reshape(nb, qb_len)
    kf = k.astype(jnp.float32)
    vf = v.astype(jnp.float32)
    Fk_t = jnp.swapaxes(F_k, 1, 2)

    def block(args):
        qi, Fi, pi = args
        s = jnp.einsum('bqhd,bkhd->bhqk', qi.astype(jnp.float32), kf) * scale
        s = s + jnp.swapaxes(Fi, 1, 2)[..., :, None] - Fk_t[:, :, None, :]
        mask = k_pos[None, :] <= pi[:, None]
        s = jnp.where(mask[None, None], s, -jnp.inf)
        pr = jax.nn.softmax(s, axis=-1)
        return jnp.einsum('bhqk,bkhd->bqhd', pr, vf)

    o = lax.map(block, (qb, Fb, pb))
    return jnp.moveaxis(o, 0, 1).reshape(B, T, H, Dh)


def _fox(xn, w_q, w_o, k_all, v_all, F_q, F_k, q_pos, k_pos):
    B, T, _ = xn.shape
    hd = H_B * D_HB
    p = xn @ w_q
    q = p[..., :hd].reshape(B, T, H_B, D_HB)
    g = p[..., hd:].astype(jnp.float32)
    o = _fox_attend(q, k_all, v_all, F_q, F_k, q_pos, k_pos).reshape(B, T, hd)
    o = o * jax.nn.sigmoid(g)
    return o.astype(xn.dtype) @ w_o


def _trunk(x, state0, k_past, v_past, logf_past, prm):
    B, T, _ = x.shape
    chunk = min(CHUNK, T)
    lb_all = jnp.cumsum(jax.nn.softmax(prm['lb_logits'].astype(jnp.float32), axis=0), axis=0)
    states = []
    for l in range(N_A):
        if state0 is None:
            s0 = jnp.zeros((B, H_A, K_A, V_A), jnp.float32)
        else:
            s0 = state0[:, l].astype(jnp.float32)
        o, s_fin = _hgrn2(_rmsnorm(x, prm['norm_a'][l]), prm['w_in_a'][l], lb_all[l],
                          prm['g_norm_a'][l], prm['w_o_a'][l], s0, chunk)
        x = x + o
        x = x + _mlp(_rmsnorm(x, prm['norm_mlp'][l]), prm['w_up'][l], prm['w_down'][l])
        states.append(s_fin)
    k_new, v_new, logf_new = _shared_kv(x, prm['norm_kv'], prm['w_kv'], prm['b_f'])
    if k_past is None:
        P = 0
        k_all, v_all, logf_all = k_new, v_new, logf_new
    else:
        P = k_past.shape[1]
        k_all = jnp.concatenate([k_past.astype(k_new.dtype), k_new], axis=1)
        v_all = jnp.concatenate([v_past.astype(v_new.dtype), v_new], axis=1)
        logf_all = jnp.concatenate([logf_past.astype(jnp.float32), logf_new], axis=1)
    F = jnp.cumsum(logf_all, axis=1)
    F_q = F[:, P:]
    k_pos = jnp.arange(P + T, dtype=jnp.int32)
    q_pos = P + jnp.arange(T, dtype=jnp.int32)
    for j in range(N_B):
        l = N_A + j
        x = x + _fox(_rmsnorm(x, prm['norm_b'][j]), prm['w_q_b'][j], prm['w_o_b'][j],
                     k_all, v_all, F_q, F, q_pos, k_pos)
        x = x + _mlp(_rmsnorm(x, prm['norm_mlp'][l]), prm['w_up'][l], prm['w_down'][l])
    y = _rmsnorm(x, prm['norm_f'])
    return y, jnp.stack(states, axis=1), k_new, v_new, logf_new


def setup_inputs(seed: int = 0) -> dict:
    key = jax.random.key(seed)
    ks = jax.random.split(key, 24)

    def nrm(k, shape, scale):
        return jax.random.normal(k, shape, jnp.float32) * scale

    b_f = jnp.linspace(1.0, 5.0, H_B, dtype=jnp.float32) + nrm(ks[0], (H_B,), 0.1)
    hd = H_B * D_HB
    return {
        'x_prompt': nrm(ks[1], (BATCH, SEQ, D_MODEL), 1.0),
        'x_sample': nrm(ks[2], (DEC_BATCH, DEC_SEQ, D_MODEL), 1.0),
        'state_hgrn': nrm(ks[3], (DEC_BATCH, N_A, H_A, K_A, V_A), 0.1),
        'cache_k': nrm(ks[4], (DEC_BATCH, PAST_LEN, H_B, D_HB), 1.0),
        'cache_v': nrm(ks[5], (DEC_BATCH, PAST_LEN, H_B, D_HB), 1.0),
        'cache_logf': jax.nn.log_sigmoid(b_f + nrm(ks[6], (DEC_BATCH, PAST_LEN, H_B), 1.0)),
        'norm_a': 1.0 + nrm(ks[7], (N_A, D_MODEL), 0.02),
        'w_in_a': nrm(ks[8], (N_A, D_MODEL, 4 * D_A), D_MODEL ** -0.5),
        'lb_logits': nrm(ks[9], (N_A + 1, D_A), 0.1),
        'g_norm_a': 1.0 + nrm(ks[10], (N_A, V_A), 0.02),
        'w_o_a': nrm(ks[11], (N_A, H_A * V_A, D_MODEL), (H_A * V_A) ** -0.5),
        'norm_kv': 1.0 + nrm(ks[12], (D_MODEL,), 0.02),
        'w_kv': nrm(ks[13], (D_MODEL, 2 * hd + H_B), D_MODEL ** -0.5),
        'b_f': b_f,
        'norm_b': 1.0 + nrm(ks[14], (N_B, D_MODEL), 0.02),
        'w_q_b': nrm(ks[15], (N_B, D_MODEL, 2 * hd), D_MODEL ** -0.5),
        'w_o_b': nrm(ks[16], (N_B, hd, D_MODEL), hd ** -0.5),
        'norm_mlp': 1.0 + nrm(ks[17], (DEPTH, D_MODEL), 0.02),
        'w_up': nrm(ks[18], (DEPTH, D_MODEL, D_FF), D_MODEL ** -0.5),
        'w_down': nrm(ks[19], (DEPTH, D_FF, D_MODEL), D_FF ** -0.5),
        'norm_f': 1.0 + nrm(ks[20], (D_MODEL,), 0.02),
    }


def reference(x_prompt, x_sample, state_hgrn, cache_k, cache_v, cache_logf,
              norm_a, w_in_a, lb_logits, g_norm_a, w_o_a,
              norm_kv, w_kv, b_f, norm_b, w_q_b, w_o_b,
              norm_mlp, w_up, w_down, norm_f):
    prm = {
        'norm_a': norm_a, 'w_in_a': w_in_a, 'lb_logits': lb_logits, 'g_norm_a': g_norm_a,
        'w_o_a': w_o_a, 'norm_kv': norm_kv, 'w_kv': w_kv, 'b_f': b_f, 'norm_b': norm_b,
        'w_q_b': w_q_b, 'w_o_b': w_o_b, 'norm_mlp': norm_mlp, 'w_up': w_up,
        'w_down': w_down, 'norm_f': norm_f,
    }
    y_prompt, st_p, k_p, v_p, lf_p = _trunk(x_prompt, None, None, None, None, prm)
    y_sample, st_s, k_s, v_s, lf_s = _trunk(x_sample, state_hgrn, cache_k, cache_v, cache_logf, prm)
    return (y_prompt, y_sample, st_p, k_p, v_p, lf_p, st_s, k_s, v_s, lf_s)
```

```python
import functools

import jax
import jax.numpy as jnp
from jax import lax
from jax.experimental import pallas as pl
from jax.experimental.pallas import tpu as pltpu

EPS = 1e-6
CHUNK = 64
K_A = 128
V_A = 128
H_B = 16
D_HB = 64
LANES = 128
VMEM_LIMIT = 56 * 1024 * 1024

F32 = jnp.float32
BF16 = jnp.bfloat16


def _dot(a, b):
    return jnp.dot(a, b, preferred_element_type=F32)


def _dot_nt(a, b):
    return lax.dot_general(a, b, (((1,), (1,)), ((), ())), preferred_element_type=F32)


def _dot_tn(a, b):
    return lax.dot_general(a, b, (((0,), (0,)), ((), ())), preferred_element_type=F32)


def _sigmoid(x):
    return 1.0 / (1.0 + jnp.exp(-x))


def _const_spec(shape):
    nd = len(shape)
    return pl.BlockSpec(shape, lambda *_: (0,) * nd, pipeline_mode=pl.Buffered(1))


def _hgrn_kernel(*refs, n_heads, n_chunks, carry):
    if carry:
        (x_ref, nrm_ref, win_ref, lbl_ref, gn_ref, o_ref, sout_ref, xn_scr, o_scr, st_scr) = refs
        s0_ref = None
    else:
        (x_ref, nrm_ref, win_ref, lbl_ref, gn_ref, s0_ref, o_ref, sout_ref, xn_scr, o_scr) = refs
        st_scr = None
    tm = n_chunks * CHUNK

    x = x_ref[...]
    inv = lax.rsqrt(jnp.mean(x * x, axis=-1, keepdims=True) + EPS)
    xn_scr[...] = (x * inv * nrm_ref[...]).astype(BF16)

    if carry:
        @pl.when(pl.program_id(1) == 0)
        def _():
            st_scr[...] = jnp.zeros_like(st_scr)

    row_in_chunk = lax.broadcasted_iota(jnp.int32, (tm, K_A), 0) % CHUNK
    tri = (lax.broadcasted_iota(jnp.int32, (CHUNK, CHUNK), 1)
           <= lax.broadcasted_iota(jnp.int32, (CHUNK, CHUNK), 0))

    def head_body(h, c_):
        p = _dot(xn_scr[...], win_ref[h])
        pq, pz = p[:, 0:K_A], p[:, K_A:2 * K_A]
        pi, pg = p[:, 2 * K_A:3 * K_A], p[:, 3 * K_A:4 * K_A]
        n_rows = lbl_ref.shape[0]
        lrows = [lbl_ref[r, h] for r in range(n_rows)]
        lmax = functools.reduce(jnp.maximum, lrows)
        lexp = [jnp.exp(l - lmax) for l in lrows]
        lb = lexp[0] / functools.reduce(lambda a, b: a + b, lexp)

        q = pq * _sigmoid(pq)
        f = lb + (1.0 - lb) * _sigmoid(pz)
        logf = jnp.log(f)
        k = (1.0 - lb) * _sigmoid(-pz)
        bc = logf
        s = 1
        while s < CHUNK:
            bc = bc + jnp.where(row_in_chunk >= s, pltpu.roll(bc, s, axis=0), 0.0)
            s *= 2
        bc3 = bc.reshape(n_chunks, CHUNK, K_A)
        btot = bc3[:, CHUNK - 1:CHUNK, :]
        q_dec = (q * jnp.exp(bc)).astype(BF16)
        k_inv = (k * jnp.exp(-bc)).astype(BF16)
        k_end = (k.reshape(n_chunks, CHUNK, K_A) * jnp.exp(btot - bc3)).astype(BF16)
        decay = jnp.exp(btot)
        v = pi.astype(BF16)

        if carry:
            st = st_scr[h]
        outs = []
        for c in range(n_chunks):
            r0 = c * CHUNK
            if not carry:
                st = s0_ref[c, h].T
            qd = q_dec[r0:r0 + CHUNK]
            sc = _dot_nt(qd, k_inv[r0:r0 + CHUNK])
            sc = jnp.where(tri, sc, 0.0).astype(BF16)
            vc = v[r0:r0 + CHUNK]
            outs.append(_dot(sc, vc) + _dot_nt(qd, st.astype(BF16)))
            st = st * decay[c] + _dot_tn(vc, k_end[c])
            if not carry:
                sout_ref[c, h] = st.T
        if carry:
            st_scr[h] = st
            sout_ref[0, h] = st.T
        o = jnp.concatenate(outs, axis=0)
        o = o * lax.rsqrt(jnp.mean(o * o, axis=-1, keepdims=True) + EPS) * gn_ref[...]
        o = o * _sigmoid(pg)
        o_scr[h] = o.astype(BF16)
        return c_

    lax.fori_loop(0, n_heads, head_body, 0)
    for h in range(n_heads):
        o_ref[:, h * V_A:(h + 1) * V_A] = o_scr[h]


def _hgrn(x2d, batch, seq, norm_w, w_in_h, lb_logits_h, g_norm, s0):
    ntok, d = x2d.shape
    n_heads = w_in_h.shape[0]
    carry = s0 is None
    if carry:
        tm = min(seq, 512)
        n_t = seq // tm
        grid = (batch, n_t)
        tok_map = lambda b, t: (b * n_t + t, 0)
        st_spec = pl.BlockSpec((1, n_heads, K_A, V_A), lambda b, t: (b, 0, 0, 0))
        scratch = [pltpu.VMEM((tm, d), BF16), pltpu.VMEM((n_heads, tm, V_A), BF16),
                   pltpu.VMEM((n_heads, V_A, K_A), F32)]
        extra_in, extra_specs = [], []
    else:
        assert seq == CHUNK
        bt = min(batch, 8)
        tm = bt * CHUNK
        grid = (batch // bt, 1)
        tok_map = lambda b, t: (b, 0)
        st_spec = pl.BlockSpec((bt, n_heads, K_A, V_A), lambda b, t: (b, 0, 0, 0))
        scratch = [pltpu.VMEM((tm, d), BF16), pltpu.VMEM((n_heads, tm, V_A), BF16)]
        extra_in, extra_specs = [s0], [st_spec]
    n_chunks = tm // CHUNK
    kern = functools.partial(_hgrn_kernel, n_heads=n_heads, n_chunks=n_chunks, carry=carry)
    return pl.pallas_call(
        kern,
        grid=grid,
        in_specs=[pl.BlockSpec((tm, d), tok_map),
                  _const_spec(norm_w.shape), _const_spec(w_in_h.shape),
                  _const_spec(lb_logits_h.shape), _const_spec(g_norm.shape)] + extra_specs,
        out_specs=[pl.BlockSpec((tm, n_heads * V_A), tok_map), st_spec],
        out_shape=[jax.ShapeDtypeStruct((ntok, n_heads * V_A), BF16),
                   jax.ShapeDtypeStruct((batch, n_heads, K_A, V_A), F32)],
        scratch_shapes=scratch,
        compiler_params=pltpu.CompilerParams(
            dimension_semantics=("arbitrary", "arbitrary"), vmem_limit_bytes=VMEM_LIMIT),
        name="hgrn_mixer",
    )(x2d, norm_w, w_in_h, lb_logits_h, g_norm, *extra_in)


def _post_kernel(*refs, ff_block, final_norm):
    if final_norm:
        x_ref, o_ref, wo_ref, nm_ref, wup_ref, wdn_ref, nf_ref, y_ref = refs
    else:
        x_ref, o_ref, wo_ref, nm_ref, wup_ref, wdn_ref, y_ref = refs
    x1 = x_ref[...] + _dot(o_ref[...], wo_ref[...])
    inv = lax.rsqrt(jnp.mean(x1 * x1, axis=-1, keepdims=True) + EPS)
    xn = (x1 * inv * nm_ref[...]).astype(BF16)
    acc = x1
    d_ff = wup_ref.shape[1]
    for j in range(d_ff // ff_block):
        hcol = jnp.maximum(_dot(xn, wup_ref[:, j * ff_block:(j + 1) * ff_block]), 0.0)
        acc = acc + _dot((hcol * hcol).astype(BF16), wdn_ref[j * ff_block:(j + 1) * ff_block, :])
    if final_norm:
        inv = lax.rsqrt(jnp.mean(acc * acc, axis=-1, keepdims=True) + EPS)
        acc = acc * inv * nf_ref[...]
    y_ref[...] = acc


def _post(x2d, o2d, w_o, norm_mlp, w_up, w_down, norm_f=None):
    ntok, d = x2d.shape
    tm = min(ntok, 512)
    final_norm = norm_f is not None
    tok = lambda i: (i, 0)
    ins = [x2d, o2d, w_o, norm_mlp, w_up, w_down] + ([norm_f] if final_norm else [])
    specs = [pl.BlockSpec((tm, d), tok), pl.BlockSpec((tm, o2d.shape[1]), tok)]
    specs += [_const_spec(a.shape) for a in ins[2:]]
    kern = functools.partial(_post_kernel, ff_block=1024, final_norm=final_norm)
    return pl.pallas_call(
        kern,
        grid=(ntok // tm,),
        in_specs=specs,
        out_specs=pl.BlockSpec((tm, d), tok),
        out_shape=jax.ShapeDtypeStruct((ntok, d), F32),
        compiler_params=pltpu.CompilerParams(
            dimension_semantics=("arbitrary",), vmem_limit_bytes=VMEM_LIMIT),
        name="post_mlp",
    )(*ins)


def _proj_kernel(x_ref, nkv_ref, nb_ref, wk_ref, wv_ref, wf_ref, bf_ref, wq_ref, wg_ref,
                 k_ref, v_ref, lf_ref, q_ref, g_ref):
    x = x_ref[...]
    xs = x * lax.rsqrt(jnp.mean(x * x, axis=-1, keepdims=True) + EPS)
    xk = (xs * nkv_ref[...]).astype(BF16)
    xq = (xs * nb_ref[...]).astype(BF16)
    k_ref[...] = _dot(xk, wk_ref[...])
    v_ref[...] = _dot(xk, wv_ref[...])
    z = _dot(xk, wf_ref[...])[:, :H_B] + bf_ref[...]
    lf_ref[...] = jnp.minimum(z, 0.0) - jnp.log(1.0 + jnp.exp(-jnp.abs(z)))
    q_ref[...] = _dot(xq, wq_ref[...])
    g_ref[...] = _dot(xq, wg_ref[...])


def _proj(x2d, norm_kv, norm_b, w_k, w_v, w_f, b_f, w_qq, w_qg):
    ntok, d = x2d.shape
    tm = min(ntok, 512)
    hd = w_k.shape[1]
    tok = lambda i: (i, 0)
    ins = [x2d, norm_kv, norm_b, w_k, w_v, w_f, b_f, w_qq, w_qg]
    return pl.pallas_call(
        _proj_kernel,
        grid=(ntok // tm,),
        in_specs=[pl.BlockSpec((tm, d), tok)] + [_const_spec(a.shape) for a in ins[1:]],
        out_specs=[pl.BlockSpec((tm, hd), tok), pl.BlockSpec((tm, hd), tok),
                   pl.BlockSpec((tm, H_B), tok), pl.BlockSpec((tm, hd), tok),
                   pl.BlockSpec((tm, hd), tok)],
        out_shape=[jax.ShapeDtypeStruct((ntok, hd), F32), jax.ShapeDtypeStruct((ntok, hd), F32),
                   jax.ShapeDtypeStruct((ntok, H_B), F32), jax.ShapeDtypeStruct((ntok, hd), F32),
                   jax.ShapeDtypeStruct((ntok, hd), F32)],
        compiler_params=pltpu.CompilerParams(
            dimension_semantics=("arbitrary",), vmem_limit_bytes=VMEM_LIMIT),
        name="kvq_proj",
    )(*ins)


def _cumsum_kernel(lf_ref, f_ref):
    n_blk = lf_ref.shape[2] // LANES
    upper = (lax.broadcasted_iota(jnp.int32, (LANES, LANES), 0)
             <= lax.broadcasted_iota(jnp.int32, (LANES, LANES), 1)).astype(BF16)
    run = jnp.zeros((lf_ref.shape[1], 1), F32)
    for c in range(n_blk):
        x = lf_ref[0, :, c * LANES:(c + 1) * LANES]
        h1 = x.astype(BF16)
        r1 = x - h1.astype(F32)
        h2 = r1.astype(BF16)
        h3 = (r1 - h2.astype(F32)).astype(BF16)
        cs = (_dot(h1, upper) + _dot(h2, upper)) + _dot(h3, upper) + run
        f_ref[0, :, c * LANES:(c + 1) * LANES] = cs
        run = cs[:, LANES - 1:LANES]


def _cumsum_time(lf_t):
    b, h, l = lf_t.shape
    return pl.pallas_call(
        _cumsum_kernel,
        grid=(b,),
        in_specs=[pl.BlockSpec((1, h, l), lambda i: (i, 0, 0))],
        out_specs=pl.BlockSpec((1, h, l), lambda i: (i, 0, 0)),
        out_shape=jax.ShapeDtypeStruct((b, h, l), F32),
        compiler_params=pltpu.CompilerParams(dimension_semantics=("arbitrary",)),
        name="logf_cumsum",
    )(lf_t)


def _pair_block(qs, k2, v2, fq, fk, mask, m_ref, l_ref, acc_ref, p_scr):
    tq = qs.shape[0] // 2
    s = _dot_nt(qs, k2)
    alphas = []
    for hf in range(2):
        sh = s[hf * tq:(hf + 1) * tq] + (fq[hf] - fk[hf])
        if mask is not None:
            sh = jnp.where(mask, sh, -jnp.inf)
        m_old = m_ref[hf]
        m_new = jnp.maximum(m_old, jnp.max(sh, axis=-1, keepdims=True))
        p = jnp.exp(sh - m_new)
        alpha = jnp.exp(m_old - m_new)
        l_ref[hf] = alpha * l_ref[hf] + jnp.sum(p, axis=-1, keepdims=True)
        m_ref[hf] = m_new
        p_scr[hf * tq:(hf + 1) * tq, :] = p.astype(BF16)
        alphas.append(alpha)
    pv = _dot(p_scr[...], v2)
    acc_ref[...] = jnp.concatenate(alphas, axis=0) * acc_ref[...] + pv


def _pair_finish(m_ref, l_ref, acc_ref, lane_a):
    tq = acc_ref.shape[0] // 2
    acc = acc_ref[...]
    return jnp.where(lane_a, acc[:tq] / l_ref[0], acc[tq:] / l_ref[1])


def _pair_init(m_ref, l_ref, acc_ref):
    m_ref[...] = jnp.full(m_ref.shape, -jnp.inf, F32)
    l_ref[...] = jnp.zeros(l_ref.shape, F32)
    acc_ref[...] = jnp.zeros(acc_ref.shape, F32)


def _split_heads(q2, lane_a):
    zero = jnp.zeros_like(q2)
    return jnp.concatenate([jnp.where(lane_a, q2, zero), jnp.where(lane_a, zero, q2)], axis=0)


def _head_column(f_tile, head):
    lane = lax.broadcasted_iota(jnp.int32, f_tile.shape, 1)
    return jnp.sum(jnp.where(lane == head, f_tile, 0.0), axis=-1, keepdims=True)


def _attn_prompt_kernel(q_ref, k_ref, v_ref, g_ref, f_ref, ft_ref, o_ref,
                        kb, vb, m_scr, l_scr, acc_scr, p_scr, *, tq):
    hp = pl.program_id(1)
    seq = q_ref.shape[1]
    n_q = seq // tq
    scale = D_HB ** -0.5
    lane_a = lax.broadcasted_iota(jnp.int32, (tq, LANES), 1) < D_HB
    causal = (lax.broadcasted_iota(jnp.int32, (tq, tq), 1)
              <= lax.broadcasted_iota(jnp.int32, (tq, tq), 0))
    kb[...] = k_ref[0].astype(BF16)
    vb[...] = v_ref[0].astype(BF16)

    def q_tile(qt, c_):
        r0 = pl.multiple_of(qt * tq, tq)
        qs = _split_heads(q_ref[0, pl.ds(r0, tq), :] * scale, lane_a).astype(BF16)
        f_tile = f_ref[0, pl.ds(r0, tq), :]
        fq = [_head_column(f_tile, 2 * hp + hf) for hf in range(2)]
        _pair_init(m_scr, l_scr, acc_scr)

        def kv_block(kt, mask):
            c0 = pl.multiple_of(kt * tq, tq)
            fk = [ft_ref[0, pl.ds(2 * hp + hf, 1), pl.ds(c0, tq)] for hf in range(2)]
            _pair_block(qs, kb[pl.ds(c0, tq), :], vb[pl.ds(c0, tq), :], fq, fk, mask,
                        m_scr, l_scr, acc_scr, p_scr)

        def full_block(kt, c2_):
            kv_block(kt, None)
            return c2_

        lax.fori_loop(0, qt, full_block, 0)
        kv_block(qt, causal)
        o = _pair_finish(m_scr, l_scr, acc_scr, lane_a)
        o = o * _sigmoid(g_ref[0, pl.ds(r0, tq), :])
        o_ref[0, pl.ds(r0, tq), :] = o.astype(BF16)
        return c_

    lax.fori_loop(0, n_q, q_tile, 0)


def _attn_prompt(q, k, v, g, f, f_t):
    b, t, hd = q.shape
    tq = min(t, 512)
    n_pairs = hd // LANES
    col = pl.BlockSpec((1, t, LANES), lambda i, j: (i, 0, j))
    kern = functools.partial(_attn_prompt_kernel, tq=tq)
    return pl.pallas_call(
        kern,
        grid=(b, n_pairs),
        in_specs=[col, col, col, col,
                  pl.BlockSpec((1, t, H_B), lambda i, j: (i, 0, 0)),
                  pl.BlockSpec((1, H_B, t), lambda i, j: (i, 0, 0))],
        out_specs=col,
        out_shape=jax.ShapeDtypeStruct((b, t, hd), BF16),
        scratch_shapes=[pltpu.VMEM((t, LANES), BF16), pltpu.VMEM((t, LANES), BF16),
                        pltpu.VMEM((2, tq, 1), F32), pltpu.VMEM((2, tq, 1), F32),
                        pltpu.VMEM((2 * tq, LANES), F32), pltpu.VMEM((2 * tq, tq), BF16)],
        compiler_params=pltpu.CompilerParams(
            dimension_semantics=("arbitrary", "arbitrary"), vmem_limit_bytes=VMEM_LIMIT),
        name="fox_attn_prompt",
    )(q, k, v, g, f, f_t)


def _attn_sample_kernel(q_ref, g_ref, kc_ref, vc_ref, kn_ref, vn_ref, fq_ref, fkc_ref, fkn_ref,
                        o_ref, m_scr, l_scr, acc_scr, p_scr, pn_scr, *, n_pairs):
    kt = pl.program_id(1)
    n_kt = pl.num_programs(1)
    tq = q_ref.shape[1]
    scale = D_HB ** -0.5
    lane_a = lax.broadcasted_iota(jnp.int32, (tq, LANES), 1) < D_HB
    causal = (lax.broadcasted_iota(jnp.int32, (tq, tq), 1)
              <= lax.broadcasted_iota(jnp.int32, (tq, tq), 0))
    f_tile = fq_ref[0]

    @pl.when(kt == 0)
    def _():
        _pair_init(m_scr, l_scr, acc_scr)

    def pair_inputs(hp):
        cols = slice(hp * LANES, (hp + 1) * LANES)
        qs = _split_heads(q_ref[0, :, cols] * scale, lane_a).astype(BF16)
        fq = [_head_column(f_tile, 2 * hp + hf) for hf in range(2)]
        return cols, qs, fq

    for hp in range(n_pairs):
        cols, qs, fq = pair_inputs(hp)
        fk = [fkc_ref[0, 2 * hp + hf:2 * hp + hf + 1, :] for hf in range(2)]
        _pair_block(qs, kc_ref[0, :, cols].astype(BF16), vc_ref[0, :, cols].astype(BF16),
                    fq, fk, None, m_scr.at[hp], l_scr.at[hp], acc_scr.at[hp], p_scr)

    @pl.when(kt == n_kt - 1)
    def _():
        for hp in range(n_pairs):
            cols, qs, fq = pair_inputs(hp)
            fk = [fkn_ref[0, 2 * hp + hf:2 * hp + hf + 1, 0:tq] for hf in range(2)]
            _pair_block(qs, kn_ref[0, :, cols].astype(BF16), vn_ref[0, :, cols].astype(BF16),
                        fq, fk, causal, m_scr.at[hp], l_scr.at[hp], acc_scr.at[hp], pn_scr)
            o = _pair_finish(m_scr.at[hp], l_scr.at[hp], acc_scr.at[hp], lane_a)
            o = o * _sigmoid(g_ref[0, :, cols])
            o_ref[0, :, cols] = o.astype(BF16)


def _attn_sample(q, g, k_cache, v_cache, k_new, v_new, f, f_t, past):
    b, t, hd = q.shape
    n_pairs = hd // LANES
    tk = min(past, 1024)
    n_kt = past // tk
    row = pl.BlockSpec((1, t, hd), lambda i, j: (i, 0, 0))
    cache = pl.BlockSpec((1, tk, hd), lambda i, j: (i, j, 0))
    kern = functools.partial(_attn_sample_kernel, n_pairs=n_pairs)
    return pl.pallas_call(
        kern,
        grid=(b, n_kt),
        in_specs=[row, row, cache, cache, row, row,
                  pl.BlockSpec((1, t, H_B), lambda i, j: (i, past // t, 0)),
                  pl.BlockSpec((1, H_B, tk), lambda i, j: (i, 0, j)),
                  pl.BlockSpec((1, H_B, LANES), lambda i, j: (i, 0, past // LANES))],
        out_specs=row,
        out_shape=jax.ShapeDtypeStruct((b, t, hd), BF16),
        scratch_shapes=[pltpu.VMEM((n_pairs, 2, t, 1), F32), pltpu.VMEM((n_pairs, 2, t, 1), F32),
                        pltpu.VMEM((n_pairs, 2 * t, LANES), F32),
                        pltpu.VMEM((2 * t, tk), BF16), pltpu.VMEM((2 * t, t), BF16)],
        compiler_params=pltpu.CompilerParams(
            dimension_semantics=("arbitrary", "arbitrary"), vmem_limit_bytes=VMEM_LIMIT),
        name="fox_attn_sample",
    )(q, g, k_cache, v_cache, k_new, v_new, f, f_t, f_t)


def _trunk(x, state0, k_past, v_past, logf_past, w):
    batch, seq, d = x.shape
    ntok = batch * seq
    hd = H_B * D_HB
    x2d = x.reshape(ntok, d)
    s0 = None if state0 is None else state0[:, 0]
    o_a, state = _hgrn(x2d, batch, seq, w["norm_a"], w["w_in_h"], w["lb_logits_h"], w["g_norm_a"], s0)
    x2d = _post(x2d, o_a, w["w_o_a"], w["norm_mlp0"], w["w_up0"], w["w_down0"])
    k, v, logf, q, g = _proj(x2d, w["norm_kv"], w["norm_b"], w["w_k"], w["w_v"], w["w_f"],
                             w["b_f"], w["w_qq"], w["w_qg"])
    lf_t = jnp.swapaxes(logf.reshape(batch, seq, H_B), 1, 2)
    q3, k3, v3, g3 = (a.reshape(batch, seq, hd) for a in (q, k, v, g))
    if k_past is None:
        f_t = _cumsum_time(lf_t)
        o_b = _attn_prompt(q3, k3, v3, g3, jnp.swapaxes(f_t, 1, 2), f_t)
    else:
        past = k_past.shape[1]
        total = past + seq
        padded = -(-total // LANES) * LANES
        lf_all = jnp.concatenate(
            [jnp.swapaxes(logf_past.astype(F32), 1, 2), lf_t,
             jnp.zeros((batch, H_B, padded - total), F32)], axis=2)
        f_t = _cumsum_time(lf_all)
        o_b = _attn_sample(q3, g3, k_past.reshape(batch, past, hd), v_past.reshape(batch, past, hd),
                           k3, v3, jnp.swapaxes(f_t, 1, 2), f_t, past)
    y = _post(x2d, o_b.reshape(ntok, hd), w["w_o_b"], w["norm_mlp1"], w["w_up1"], w["w_down1"],
              w["norm_f"])
    return (y.reshape(batch, seq, d), state[:, None],
            k.reshape(batch, seq, H_B, D_HB), v.reshape(batch, seq, H_B, D_HB),
            logf.reshape(batch, seq, H_B))


def kernel(x_prompt, x_sample, state_hgrn, cache_k, cache_v, cache_logf, norm_a, w_in_a, lb_logits, g_norm_a, w_o_a, norm_kv, w_kv, b_f, norm_b, w_q_b, w_o_b, norm_mlp, w_up, w_down, norm_f):
    d = x_prompt.shape[-1]
    assert w_in_a.shape[0] == 1 and w_q_b.shape[0] == 1, "one HGRN2 layer and one FoX layer"
    n_heads = w_in_a.shape[2] // (4 * K_A)
    hd = H_B * D_HB
    row = lambda a: a.reshape(1, -1).astype(F32)
    w_f = jnp.zeros((d, LANES), F32).at[:, :H_B].set(w_kv[:, 2 * hd:])
    w = {
        "norm_a": row(norm_a[0]),
        "w_in_h": w_in_a[0].reshape(d, 4, n_heads, K_A).transpose(2, 0, 1, 3)
                           .reshape(n_heads, d, 4 * K_A).astype(BF16),
        "lb_logits_h": lb_logits.astype(F32).reshape(lb_logits.shape[0], n_heads, 1, K_A),
        "g_norm_a": row(g_norm_a[0]),
        "w_o_a": w_o_a[0].astype(BF16),
        "norm_mlp0": row(norm_mlp[0]), "w_up0": w_up[0].astype(BF16), "w_down0": w_down[0].astype(BF16),
        "norm_mlp1": row(norm_mlp[1]), "w_up1": w_up[1].astype(BF16), "w_down1": w_down[1].astype(BF16),
        "norm_kv": row(norm_kv), "norm_b": row(norm_b[0]),
        "w_k": w_kv[:, :hd].astype(BF16), "w_v": w_kv[:, hd:2 * hd].astype(BF16),
        "w_f": w_f.astype(BF16), "b_f": row(b_f),
        "w_qq": w_q_b[0][:, :hd].astype(BF16), "w_qg": w_q_b[0][:, hd:].astype(BF16),
        "w_o_b": w_o_b[0].astype(BF16),
        "norm_f": row(norm_f),
    }
    y_p, st_p, k_p, v_p, lf_p = _trunk(x_prompt, None, None, None, None, w)
    y_s, st_s, k_s, v_s, lf_s = _trunk(x_sample, state_hgrn, cache_k, cache_v, cache_logf, w)
    return (y_p, y_s, st_p, k_p, v_p, lf_p, st_s, k_s, v_s, lf_s)
```

```python
import functools

import jax
import jax.numpy as jnp
from jax import lax
from jax.experimental import pallas as pl
from jax.experimental.pallas import tpu as pltpu

EPS = 1e-6
CHUNK = 64
K_A = 128
V_A = 128
H_B = 16
D_HB = 64
LANES = 128
VMEM_LIMIT = 56 * 1024 * 1024

F32 = jnp.float32
BF16 = jnp.bfloat16


def _dot(a, b):
    return jnp.dot(a, b, preferred_element_type=F32)


def _dot_nt(a, b):
    return lax.dot_general(a, b, (((1,), (1,)), ((), ())), preferred_element_type=F32)


def _dot_tn(a, b):
    return lax.dot_general(a, b, (((0,), (0,)), ((), ())), preferred_element_type=F32)


def _sigmoid(x):
    return 1.0 / (1.0 + jnp.exp(-x))


def _const_spec(shape):
    nd = len(shape)
    return pl.BlockSpec(shape, lambda *_: (0,) * nd, pipeline_mode=pl.Buffered(1))


def _hgrn_kernel(*refs, n_heads, n_chunks, carry):
    if carry:
        (x_ref, nrm_ref, win_ref, lbl_ref, gn_ref, o_ref, sout_ref, xn_scr, o_scr, st_scr) = refs
        s0_ref = None
    else:
        (x_ref, nrm_ref, win_ref, lbl_ref, gn_ref, s0_ref, o_ref, sout_ref, xn_scr, o_scr) = refs
        st_scr = None
    tm = n_chunks * CHUNK

    x = x_ref[...]
    inv = lax.rsqrt(jnp.mean(x * x, axis=-1, keepdims=True) + EPS)
    xn_scr[...] = (x * inv * nrm_ref[...]).astype(BF16)

    if carry:
        @pl.when(pl.program_id(1) == 0)
        def _():
            st_scr[...] = jnp.zeros_like(st_scr)

    row_in_chunk = lax.broadcasted_iota(jnp.int32, (tm, K_A), 0) % CHUNK
    tri = (lax.broadcasted_iota(jnp.int32, (CHUNK, CHUNK), 1)
           <= lax.broadcasted_iota(jnp.int32, (CHUNK, CHUNK), 0))

    def head_body(h, c_):
        p = _dot(xn_scr[...], win_ref[h])
        pq, pz = p[:, 0:K_A], p[:, K_A:2 * K_A]
        pi, pg = p[:, 2 * K_A:3 * K_A], p[:, 3 * K_A:4 * K_A]
        n_rows = lbl_ref.shape[0]
        lrows = [lbl_ref[r, h] for r in range(n_rows)]
        lmax = functools.reduce(jnp.maximum, lrows)
        lexp = [jnp.exp(l - lmax) for l in lrows]
        lb = lexp[0] / functools.reduce(lambda a, b: a + b, lexp)

        q = pq * _sigmoid(pq)
        f = lb + (1.0 - lb) * _sigmoid(pz)
        logf = jnp.log(f)
        k = (1.0 - lb) * _sigmoid(-pz)
        bc = logf
        s = 1
        while s < CHUNK:
            bc = bc + jnp.where(row_in_chunk >= s, pltpu.roll(bc, s, axis=0), 0.0)
            s *= 2
        bc3 = bc.reshape(n_chunks, CHUNK, K_A)
        btot = bc3[:, CHUNK - 1:CHUNK, :]
        q_dec = (q * jnp.exp(bc)).astype(BF16)
        k_inv = (k * jnp.exp(-bc)).astype(BF16)
        k_end = (k.reshape(n_chunks, CHUNK, K_A) * jnp.exp(btot - bc3)).astype(BF16)
        decay = jnp.exp(btot)
        v = pi.astype(BF16)

        if carry:
            st = st_scr[h]
        outs = []
        for c in range(n_chunks):
            r0 = c * CHUNK
            if not carry:
                st = s0_ref[c, h].T
            qd = q_dec[r0:r0 + CHUNK]
            sc = _dot_nt(qd, k_inv[r0:r0 + CHUNK])
            sc = jnp.where(tri, sc, 0.0).astype(BF16)
            vc = v[r0:r0 + CHUNK]
            outs.append(_dot(sc, vc) + _dot_nt(qd, st.astype(BF16)))
            st = st * decay[c] + _dot_tn(vc, k_end[c])
            if not carry:
                sout_ref[c, h] = st.T
        if carry:
            st_scr[h] = st
            sout_ref[0, h] = st.T
        o = jnp.concatenate(outs, axis=0)
        o = o * lax.rsqrt(jnp.mean(o * o, axis=-1, keepdims=True) + EPS) * gn_ref[...]
        o = o * _sigmoid(pg)
        o_scr[h] = o.astype(BF16)
        return c_

    lax.fori_loop(0, n_heads, head_body, 0)
    for h in range(n_heads):
        o_ref[:, h * V_A:(h + 1) * V_A] = o_scr[h]


def _hgrn(x2d, batch, seq, norm_w, w_in_h, lb_logits_h, g_norm, s0):
    ntok, d = x2d.shape
    n_heads = w_in_h.shape[0]
    carry = s0 is None
    if carry:
        tm = min(seq, 512)
        n_t = seq // tm
        grid = (batch, n_t)
        tok_map = lambda b, t: (b * n_t + t, 0)
        st_spec = pl.BlockSpec((1, n_heads, K_A, V_A), lambda b, t: (b, 0, 0, 0))
        scratch = [pltpu.VMEM((tm, d), BF16), pltpu.VMEM((n_heads, tm, V_A), BF16),
                   pltpu.VMEM((n_heads, V_A, K_A), F32)]
        extra_in, extra_specs = [], []
    else:
        assert seq == CHUNK
        bt = min(batch, 8)
        tm = bt * CHUNK
        grid = (batch // bt, 1)
        tok_map = lambda b, t: (b, 0)
        st_spec = pl.BlockSpec((bt, n_heads, K_A, V_A), lambda b, t: (b, 0, 0, 0))
        scratch = [pltpu.VMEM((tm, d), BF16), pltpu.VMEM((n_heads, tm, V_A), BF16)]
        extra_in, extra_specs = [s0], [st_spec]
    n_chunks = tm // CHUNK
    kern = functools.partial(_hgrn_kernel, n_heads=n_heads, n_chunks=n_chunks, carry=carry)
    return pl.pallas_call(
        kern,
        grid=grid,
        in_specs=[pl.BlockSpec((tm, d), tok_map),
                  _const_spec(norm_w.shape), _const_spec(w_in_h.shape),
                  _const_spec(lb_logits_h.shape), _const_spec(g_norm.shape)] + extra_specs,
        out_specs=[pl.BlockSpec((tm, n_heads * V_A), tok_map), st_spec],
        out_shape=[jax.ShapeDtypeStruct((ntok, n_heads * V_A), BF16),
                   jax.ShapeDtypeStruct((batch, n_heads, K_A, V_A), F32)],
        scratch_shapes=scratch,
        compiler_params=pltpu.CompilerParams(
            dimension_semantics=("arbitrary", "arbitrary"), vmem_limit_bytes=VMEM_LIMIT),
        name="hgrn_mixer",
    )(x2d, norm_w, w_in_h, lb_logits_h, g_norm, *extra_in)


def _post_kernel(*refs, ff_block, final_norm):
    if final_norm:
        x_ref, o_ref, wo_ref, nm_ref, wup_ref, wdn_ref, nf_ref, y_ref = refs
    else:
        x_ref, o_ref, wo_ref, nm_ref, wup_ref, wdn_ref, y_ref = refs
    x1 = x_ref[...] + _dot(o_ref[...], wo_ref[...])
    inv = lax.rsqrt(jnp.mean(x1 * x1, axis=-1, keepdims=True) + EPS)
    xn = (x1 * inv * nm_ref[...]).astype(BF16)
    acc = x1
    d_ff = wup_ref.shape[1]
    for j in range(d_ff // ff_block):
        hcol = jnp.maximum(_dot(xn, wup_ref[:, j * ff_block:(j + 1) * ff_block]), 0.0)
        acc = acc + _dot((hcol * hcol).astype(BF16), wdn_ref[j * ff_block:(j + 1) * ff_block, :])
    if final_norm:
        inv = lax.rsqrt(jnp.mean(acc * acc, axis=-1, keepdims=True) + EPS)
        acc = acc * inv * nf_ref[...]
    y_ref[...] = acc


def _post(x2d, o2d, w_o, norm_mlp, w_up, w_down, norm_f=None):
    ntok, d = x2d.shape
    tm = min(ntok, 512)
    final_norm = norm_f is not None
    tok = lambda i: (i, 0)
    ins = [x2d, o2d, w_o, norm_mlp, w_up, w_down] + ([norm_f] if final_norm else [])
    specs = [pl.BlockSpec((tm, d), tok), pl.BlockSpec((tm, o2d.shape[1]), tok)]
    specs += [_const_spec(a.shape) for a in ins[2:]]
    kern = functools.partial(_post_kernel, ff_block=1024, final_norm=final_norm)
    return pl.pallas_call(
        kern,
        grid=(ntok // tm,),
        in_specs=specs,
        out_specs=pl.BlockSpec((tm, d), tok),
        out_shape=jax.ShapeDtypeStruct((ntok, d), F32),
        compiler_params=pltpu.CompilerParams(
            dimension_semantics=("arbitrary",), vmem_limit_bytes=VMEM_LIMIT),
        name="post_mlp",
    )(*ins)


def _proj_kernel(x_ref, nkv_ref, nb_ref, wk_ref, wv_ref, wf_ref, bf_ref, wq_ref, wg_ref,
                 k_ref, v_ref, lf_ref, q_ref, g_ref):
    x = x_ref[...]
    xs = x * lax.rsqrt(jnp.mean(x * x, axis=-1, keepdims=True) + EPS)
    xk = (xs * nkv_ref[...]).astype(BF16)
    xq = (xs * nb_ref[...]).astype(BF16)
    k_ref[...] = _dot(xk, wk_ref[...])
    v_ref[...] = _dot(xk, wv_ref[...])
    z = _dot(xk, wf_ref[...])[:, :H_B] + bf_ref[...]
    lf_ref[...] = jnp.minimum(z, 0.0) - jnp.log(1.0 + jnp.exp(-jnp.abs(z)))
    q_ref[...] = _dot(xq, wq_ref[...])
    g_ref[...] = _dot(xq, wg_ref[...])


def _proj(x2d, norm_kv, norm_b, w_k, w_v, w_f, b_f, w_qq, w_qg):
    ntok, d = x2d.shape
    tm = min(ntok, 512)
    hd = w_k.shape[1]
    tok = lambda i: (i, 0)
    ins = [x2d, norm_kv, norm_b, w_k, w_v, w_f, b_f, w_qq, w_qg]
    return pl.pallas_call(
        _proj_kernel,
        grid=(ntok // tm,),
        in_specs=[pl.BlockSpec((tm, d), tok)] + [_const_spec(a.shape) for a in ins[1:]],
        out_specs=[pl.BlockSpec((tm, hd), tok), pl.BlockSpec((tm, hd), tok),
                   pl.BlockSpec((tm, H_B), tok), pl.BlockSpec((tm, hd), tok),
                   pl.BlockSpec((tm, hd), tok)],
        out_shape=[jax.ShapeDtypeStruct((ntok, hd), F32), jax.ShapeDtypeStruct((ntok, hd), F32),
                   jax.ShapeDtypeStruct((ntok, H_B), F32), jax.ShapeDtypeStruct((ntok, hd), F32),
                   jax.ShapeDtypeStruct((ntok, hd), F32)],
        compiler_params=pltpu.CompilerParams(
            dimension_semantics=("arbitrary",), vmem_limit_bytes=VMEM_LIMIT),
        name="kvq_proj",
    )(*ins)


def _cumsum_kernel(lf_ref, f_ref):
    n_blk = lf_ref.shape[2] // LANES
    upper = (lax.broadcasted_iota(jnp.int32, (LANES, LANES), 0)
             <= lax.broadcasted_iota(jnp.int32, (LANES, LANES), 1)).astype(BF16)
    run = jnp.zeros((lf_ref.shape[1], 1), F32)
    for c in range(n_blk):
        x = lf_ref[0, :, c * LANES:(c + 1) * LANES]
        h1 = x.astype(BF16)
        r1 = x - h1.astype(F32)
        h2 = r1.astype(BF16)
        h3 = (r1 - h2.astype(F32)).astype(BF16)
        cs = (_dot(h1, upper) + _dot(h2, upper)) + _dot(h3, upper) + run
        f_ref[0, :, c * LANES:(c + 1) * LANES] = cs
        run = cs[:, LANES - 1:LANES]


def _cumsum_time(lf_t):
    b, h, l = lf_t.shape
    return pl.pallas_call(
        _cumsum_kernel,
        grid=(b,),
        in_specs=[pl.BlockSpec((1, h, l), lambda i: (i, 0, 0))],
        out_specs=pl.BlockSpec((1, h, l), lambda i: (i, 0, 0)),
        out_shape=jax.ShapeDtypeStruct((b, h, l), F32),
        compiler_params=pltpu.CompilerParams(dimension_semantics=("arbitrary",)),
        name="logf_cumsum",
    )(lf_t)


def _pair_block(qs, k2, v2, fq, fk, mask, m_ref, l_ref, acc_ref, p_scr):
    tq = qs.shape[0] // 2
    s = _dot_nt(qs, k2)
    alphas = []
    for hf in range(2):
        sh = s[hf * tq:(hf + 1) * tq] + (fq[hf] - fk[hf])
        if mask is not None:
            sh = jnp.where(mask, sh, -jnp.inf)
        m_old = m_ref[hf]
        m_new = jnp.maximum(m_old, jnp.max(sh, axis=-1, keepdims=True))
        p = jnp.exp(sh - m_new)
        alpha = jnp.exp(m_old - m_new)
        l_ref[hf] = alpha * l_ref[hf] + jnp.sum(p, axis=-1, keepdims=True)
        m_ref[hf] = m_new
        p_scr[hf * tq:(hf + 1) * tq, :] = p.astype(BF16)
        alphas.append(alpha)
    pv = _dot(p_scr[...], v2)
    acc_ref[...] = jnp.concatenate(alphas, axis=0) * acc_ref[...] + pv


def _pair_finish(m_ref, l_ref, acc_ref, lane_a):
    tq = acc_ref.shape[0] // 2
    acc = acc_ref[...]
    return jnp.where(lane_a, acc[:tq] / l_ref[0], acc[tq:] / l_ref[1])


def _pair_init(m_ref, l_ref, acc_ref):
    m_ref[...] = jnp.full(m_ref.shape, -jnp.inf, F32)
    l_ref[...] = jnp.zeros(l_ref.shape, F32)
    acc_ref[...] = jnp.zeros(acc_ref.shape, F32)


def _split_heads(q2, lane_a):
    zero = jnp.zeros_like(q2)
    return jnp.concatenate([jnp.where(lane_a, q2, zero), jnp.where(lane_a, zero, q2)], axis=0)


def _head_column(f_tile, head):
    lane = lax.broadcasted_iota(jnp.int32, f_tile.shape, 1)
    return jnp.sum(jnp.where(lane == head, f_tile, 0.0), axis=-1, keepdims=True)


LOG2E = 1.4426950408889634


def _split3(x):
    h1 = x.astype(BF16).astype(F32)
    r1 = x - h1
    h2 = r1.astype(BF16).astype(F32)
    h3 = (r1 - h2).astype(BF16).astype(F32)
    return h1, h2, h3


def _lane_table(lane, base, pieces):
    out = jnp.zeros(lane.shape, F32)
    for i, piece in enumerate(pieces):
        out = jnp.where(lane == base + i, piece, out)
    return out


def _attn_prompt_kernel(q_ref, k_ref, v_ref, g_ref, f_ref, o_ref,
                        kaug, vaug, m_scr, acc_scr, *, tq):
    hp = pl.program_id(1)
    seq = q_ref.shape[1]
    n_q = seq // tq
    n_col = tq // LANES
    qscale = (D_HB ** -0.5) * LOG2E
    lane_t = lax.broadcasted_iota(jnp.int32, (seq, LANES), 1)
    lane_q = lax.broadcasted_iota(jnp.int32, (tq, LANES), 1)
    causal = (lax.broadcasted_iota(jnp.int32, (tq, tq), 1)
              <= lax.broadcasted_iota(jnp.int32, (tq, tq), 0))
    one = jnp.ones((), F32)

    f_all = f_ref[0] * LOG2E
    for hf in range(2):
        k2, v2 = k_ref[0], v_ref[0]
        if hf == 1:
            k2, v2 = pltpu.roll(k2, D_HB, axis=1), pltpu.roll(v2, D_HB, axis=1)
        h1, h2, h3 = _split3(_head_column(f_all, 2 * hp + hf))
        tail = _lane_table(lane_t, D_HB, [one, one, one, -h1, -h2, -h3])
        kaug[hf] = jnp.where(lane_t < D_HB, k2, tail).astype(BF16)
        vaug[hf] = jnp.where(lane_t < D_HB, v2, _lane_table(lane_t, D_HB, [one])).astype(BF16)

    def q_tile(qt, c_):
        r0 = pl.multiple_of(qt * tq, tq)
        q2 = q_ref[0, pl.ds(r0, tq), :] * qscale
        f_tile = f_ref[0, pl.ds(r0, tq), :] * LOG2E
        q_aug = []
        for hf in range(2):
            qh = q2 if hf == 0 else pltpu.roll(q2, D_HB, axis=1)
            h1, h2, h3 = _split3(_head_column(f_tile, 2 * hp + hf))
            tail = _lane_table(lane_q, D_HB, [h1, h2, h3, one, one, one])
            q_aug.append(jnp.where(lane_q < D_HB, qh, tail).astype(BF16))
        m_scr[...] = jnp.full(m_scr.shape, -jnp.inf, F32)
        acc_scr[...] = jnp.zeros(acc_scr.shape, F32)

        def kv_block(kt, mask):
            c0 = pl.multiple_of(kt * tq, tq)
            for hf in range(2):
                s2 = _dot_nt(q_aug[hf], kaug[hf, pl.ds(c0, tq), :])
                if mask is not None:
                    s2 = jnp.where(mask, s2, -jnp.inf)
                cols = [s2[:, c * LANES:(c + 1) * LANES] for c in range(n_col)]
                m_old = m_scr[hf]
                m_new = jnp.maximum(
                    m_old, jnp.max(functools.reduce(jnp.maximum, cols), axis=-1, keepdims=True))
                p = jnp.concatenate([jnp.exp2(c - m_new).astype(BF16) for c in cols], axis=1)
                acc_scr[hf] = (jnp.exp2(m_old - m_new) * acc_scr[hf]
                               + _dot(p, vaug[hf, pl.ds(c0, tq), :]))
                m_scr[hf] = m_new

        def full_block(kt, c2_):
            kv_block(kt, None)
            return c2_

        lax.fori_loop(0, qt, full_block, 0)
        kv_block(qt, causal)
        outs = []
        for hf in range(2):
            acc = acc_scr[hf]
            denom = jnp.sum(jnp.where(lane_q == D_HB, acc, 0.0), axis=-1, keepdims=True)
            outs.append(acc / denom)
        o = jnp.where(lane_q < D_HB, outs[0], pltpu.roll(outs[1], D_HB, axis=1))
        o = o * _sigmoid(g_ref[0, pl.ds(r0, tq), :])
        o_ref[0, pl.ds(r0, tq), :] = o.astype(BF16)
        return c_

    lax.fori_loop(0, n_q, q_tile, 0)


def _attn_prompt(q, k, v, g, f):
    b, t, hd = q.shape
    tq = min(t, 512)
    n_pairs = hd // LANES
    col = pl.BlockSpec((1, t, LANES), lambda i, j: (i, 0, j))
    kern = functools.partial(_attn_prompt_kernel, tq=tq)
    return pl.pallas_call(
        kern,
        grid=(b, n_pairs),
        in_specs=[col, col, col, col, pl.BlockSpec((1, t, H_B), lambda i, j: (i, 0, 0))],
        out_specs=col,
        out_shape=jax.ShapeDtypeStruct((b, t, hd), BF16),
        scratch_shapes=[pltpu.VMEM((2, t, LANES), BF16), pltpu.VMEM((2, t, LANES), BF16),
                        pltpu.VMEM((2, tq, LANES), F32), pltpu.VMEM((2, tq, LANES), F32)],
        compiler_params=pltpu.CompilerParams(
            dimension_semantics=("arbitrary", "arbitrary"), vmem_limit_bytes=VMEM_LIMIT),
        name="fox_attn_prompt",
    )(q, k, v, g, f)


def _attn_sample_kernel(q_ref, g_ref, kc_ref, vc_ref, kn_ref, vn_ref, fq_ref, fkc_ref, fkn_ref,
                        o_ref, m_scr, l_scr, acc_scr, p_scr, pn_scr, *, n_pairs):
    kt = pl.program_id(1)
    n_kt = pl.num_programs(1)
    tq = q_ref.shape[1]
    scale = D_HB ** -0.5
    lane_a = lax.broadcasted_iota(jnp.int32, (tq, LANES), 1) < D_HB
    causal = (lax.broadcasted_iota(jnp.int32, (tq, tq), 1)
              <= lax.broadcasted_iota(jnp.int32, (tq, tq), 0))
    f_tile = fq_ref[0]

    @pl.when(kt == 0)
    def _():
        _pair_init(m_scr, l_scr, acc_scr)

    def pair_inputs(hp):
        cols = slice(hp * LANES, (hp + 1) * LANES)
        qs = _split_heads(q_ref[0, :, cols] * scale, lane_a).astype(BF16)
        fq = [_head_column(f_tile, 2 * hp + hf) for hf in range(2)]
        return cols, qs, fq

    for hp in range(n_pairs):
        cols, qs, fq = pair_inputs(hp)
        fk = [fkc_ref[0, 2 * hp + hf:2 * hp + hf + 1, :] for hf in range(2)]
        _pair_block(qs, kc_ref[0, :, cols].astype(BF16), vc_ref[0, :, cols].astype(BF16),
                    fq, fk, None, m_scr.at[hp], l_scr.at[hp], acc_scr.at[hp], p_scr)

    @pl.when(kt == n_kt - 1)
    def _():
        for hp in range(n_pairs):
            cols, qs, fq = pair_inputs(hp)
            fk = [fkn_ref[0, 2 * hp + hf:2 * hp + hf + 1, 0:tq] for hf in range(2)]
            _pair_block(qs, kn_ref[0, :, cols].astype(BF16), vn_ref[0, :, cols].astype(BF16),
                        fq, fk, causal, m_scr.at[hp], l_scr.at[hp], acc_scr.at[hp], pn_scr)
            o = _pair_finish(m_scr.at[hp], l_scr.at[hp], acc_scr.at[hp], lane_a)
            o = o * _sigmoid(g_ref[0, :, cols])
            o_ref[0, :, cols] = o.astype(BF16)


def _attn_sample(q, g, k_cache, v_cache, k_new, v_new, f, f_t, past):
    b, t, hd = q.shape
    n_pairs = hd // LANES
    tk = min(past, 1024)
    n_kt = past // tk
    row = pl.BlockSpec((1, t, hd), lambda i, j: (i, 0, 0))
    cache = pl.BlockSpec((1, tk, hd), lambda i, j: (i, j, 0))
    kern = functools.partial(_attn_sample_kernel, n_pairs=n_pairs)
    return pl.pallas_call(
        kern,
        grid=(b, n_kt),
        in_specs=[row, row, cache, cache, row, row,
                  pl.BlockSpec((1, t, H_B), lambda i, j: (i, past // t, 0)),
                  pl.BlockSpec((1, H_B, tk), lambda i, j: (i, 0, j)),
                  pl.BlockSpec((1, H_B, LANES), lambda i, j: (i, 0, past // LANES))],
        out_specs=row,
        out_shape=jax.ShapeDtypeStruct((b, t, hd), BF16),
        scratch_shapes=[pltpu.VMEM((n_pairs, 2, t, 1), F32), pltpu.VMEM((n_pairs, 2, t, 1), F32),
                        pltpu.VMEM((n_pairs, 2 * t, LANES), F32),
                        pltpu.VMEM((2 * t, tk), BF16), pltpu.VMEM((2 * t, t), BF16)],
        compiler_params=pltpu.CompilerParams(
            dimension_semantics=("arbitrary", "arbitrary"), vmem_limit_bytes=VMEM_LIMIT),
        name="fox_attn_sample",
    )(q, g, k_cache, v_cache, k_new, v_new, f, f_t, f_t)


def _trunk(x, state0, k_past, v_past, logf_past, w):
    batch, seq, d = x.shape
    ntok = batch * seq
    hd = H_B * D_HB
    x2d = x.reshape(ntok, d)
    s0 = None if state0 is None else state0[:, 0]
    o_a, state = _hgrn(x2d, batch, seq, w["norm_a"], w["w_in_h"], w["lb_logits_h"], w["g_norm_a"], s0)
    x2d = _post(x2d, o_a, w["w_o_a"], w["norm_mlp0"], w["w_up0"], w["w_down0"])
    k, v, logf, q, g = _proj(x2d, w["norm_kv"], w["norm_b"], w["w_k"], w["w_v"], w["w_f"],
                             w["b_f"], w["w_qq"], w["w_qg"])
    lf_t = jnp.swapaxes(logf.reshape(batch, seq, H_B), 1, 2)
    q3, k3, v3, g3 = (a.reshape(batch, seq, hd) for a in (q, k, v, g))
    if k_past is None:
        f_t = _cumsum_time(lf_t)
        o_b = _attn_prompt(q3, k3, v3, g3, jnp.swapaxes(f_t, 1, 2))
    else:
        past = k_past.shape[1]
        total = past + seq
        padded = -(-total // LANES) * LANES
        lf_all = jnp.concatenate(
            [jnp.swapaxes(logf_past.astype(F32), 1, 2), lf_t,
             jnp.zeros((batch, H_B, padded - total), F32)], axis=2)
        f_t = _cumsum_time(lf_all)
        o_b = _attn_sample(q3, g3, k_past.reshape(batch, past, hd), v_past.reshape(batch, past, hd),
                           k3, v3, jnp.swapaxes(f_t, 1, 2), f_t, past)
    y = _post(x2d, o_b.reshape(ntok, hd), w["w_o_b"], w["norm_mlp1"], w["w_up1"], w["w_down1"],
              w["norm_f"])
    return (y.reshape(batch, seq, d), state[:, None],
            k.reshape(batch, seq, H_B, D_HB), v.reshape(batch, seq, H_B, D_HB),
            logf.reshape(batch, seq, H_B))


def kernel(x_prompt, x_sample, state_hgrn, cache_k, cache_v, cache_logf, norm_a, w_in_a, lb_logits, g_norm_a, w_o_a, norm_kv, w_kv, b_f, norm_b, w_q_b, w_o_b, norm_mlp, w_up, w_down, norm_f):
    d = x_prompt.shape[-1]
    assert w_in_a.shape[0] == 1 and w_q_b.shape[0] == 1, "one HGRN2 layer and one FoX layer"
    n_heads = w_in_a.shape[2] // (4 * K_A)
    hd = H_B * D_HB
    row = lambda a: a.reshape(1, -1).astype(F32)
    w_f = jnp.zeros((d, LANES), F32).at[:, :H_B].set(w_kv[:, 2 * hd:])
    w = {
        "norm_a": row(norm_a[0]),
        "w_in_h": w_in_a[0].reshape(d, 4, n_heads, K_A).transpose(2, 0, 1, 3)
                           .reshape(n_heads, d, 4 * K_A).astype(BF16),
        "lb_logits_h": lb_logits.astype(F32).reshape(lb_logits.shape[0], n_heads, 1, K_A),
        "g_norm_a": row(g_norm_a[0]),
        "w_o_a": w_o_a[0].astype(BF16),
        "norm_mlp0": row(norm_mlp[0]), "w_up0": w_up[0].astype(BF16), "w_down0": w_down[0].astype(BF16),
        "norm_mlp1": row(norm_mlp[1]), "w_up1": w_up[1].astype(BF16), "w_down1": w_down[1].astype(BF16),
        "norm_kv": row(norm_kv), "norm_b": row(norm_b[0]),
        "w_k": w_kv[:, :hd].astype(BF16), "w_v": w_kv[:, hd:2 * hd].astype(BF16),
        "w_f": w_f.astype(BF16), "b_f": row(b_f),
        "w_qq": w_q_b[0][:, :hd].astype(BF16), "w_qg": w_q_b[0][:, hd:].astype(BF16),
        "w_o_b": w_o_b[0].astype(BF16),
        "norm_f": row(norm_f),
    }
    y_p, st_p, k_p, v_p, lf_p = _trunk(x_prompt, None, None, None, None, w)
    y_s, st_s, k_s, v_s, lf_s = _trunk(x_sample, state_hgrn, cache_k, cache_v, cache_logf, w)
    return (y_p, y_s, st_p, k_p, v_p, lf_p, st_s, k_s, v_s, lf_s)
```

```python
import functools

import jax
import jax.numpy as jnp
from jax import lax
from jax.experimental import pallas as pl
from jax.experimental.pallas import tpu as pltpu

EPS = 1e-6
CHUNK = 64
K_A = 128
V_A = 128
H_B = 16
D_HB = 64
LANES = 128
VMEM_LIMIT = 56 * 1024 * 1024

F32 = jnp.float32
BF16 = jnp.bfloat16


def _dot(a, b):
    return jnp.dot(a, b, preferred_element_type=F32)


def _dot_nt(a, b):
    return lax.dot_general(a, b, (((1,), (1,)), ((), ())), preferred_element_type=F32)


def _dot_tn(a, b):
    return lax.dot_general(a, b, (((0,), (0,)), ((), ())), preferred_element_type=F32)


def _sigmoid(x):
    return 1.0 / (1.0 + jnp.exp(-x))


def _const_spec(shape):
    nd = len(shape)
    return pl.BlockSpec(shape, lambda *_: (0,) * nd, pipeline_mode=pl.Buffered(1))


def _hgrn_kernel(*refs, n_heads, n_chunks, carry):
    if carry:
        (x_ref, nrm_ref, win_ref, lbl_ref, gn_ref, o_ref, sout_ref, xn_scr, st_scr) = refs
        s0_ref = None
    else:
        (x_ref, nrm_ref, win_ref, lbl_ref, gn_ref, s0_ref, o_ref, sout_ref, xn_scr) = refs
        st_scr = None
    tm = n_chunks * CHUNK

    x = x_ref[...]
    inv = lax.rsqrt(jnp.mean(x * x, axis=-1, keepdims=True) + EPS)
    xn_scr[...] = (x * inv * nrm_ref[...]).astype(BF16)

    if carry:
        @pl.when(pl.program_id(1) == 0)
        def _():
            st_scr[...] = jnp.zeros_like(st_scr)

    row_in_chunk = lax.broadcasted_iota(jnp.int32, (tm, K_A), 0) % CHUNK
    tri = (lax.broadcasted_iota(jnp.int32, (CHUNK, CHUNK), 1)
           <= lax.broadcasted_iota(jnp.int32, (CHUNK, CHUNK), 0))

    def project(h):
        return _dot(xn_scr[...], win_ref[h])

    def mix(h, p):
        pq, pz = p[:, 0:K_A], p[:, K_A:2 * K_A]
        pi, pg = p[:, 2 * K_A:3 * K_A], p[:, 3 * K_A:4 * K_A]
        n_rows = lbl_ref.shape[0]
        lrows = [lbl_ref[r, h] for r in range(n_rows)]
        lmax = functools.reduce(jnp.maximum, lrows)
        lexp = [jnp.exp(l - lmax) for l in lrows]
        lb = lexp[0] / functools.reduce(lambda a, b: a + b, lexp)

        q = pq * _sigmoid(pq)
        f = lb + (1.0 - lb) * _sigmoid(pz)
        logf = jnp.log(f)
        k = (1.0 - lb) * _sigmoid(-pz)
        bc = logf
        s = 1
        while s < CHUNK:
            bc = bc + jnp.where(row_in_chunk >= s, pltpu.roll(bc, s, axis=0), 0.0)
            s *= 2
        chunked = lambda a: a.reshape(n_chunks, CHUNK, K_A)
        bc3 = chunked(bc)
        btot = bc3[:, CHUNK - 1:CHUNK, :]
        q_dec = chunked(q * jnp.exp(bc)).astype(BF16)
        k_inv = chunked(k * jnp.exp(-bc)).astype(BF16)
        k_end = (chunked(k) * jnp.exp(btot - bc3)).astype(BF16)
        decay = jnp.exp(btot)
        v = chunked(pi).astype(BF16)

        sc = jnp.einsum('cqk,csk->cqs', q_dec, k_inv, preferred_element_type=F32)
        sc = jnp.where(tri, sc, 0.0).astype(BF16)
        o_intra = jnp.einsum('cqs,csv->cqv', sc, v, preferred_element_type=F32)
        upd = jnp.einsum('csv,csk->cvk', v, k_end, preferred_element_type=F32)
        states = []
        if carry:
            st = st_scr[h]
            for c in range(n_chunks):
                states.append(st)
                st = st * decay[c] + upd[c]
            st_scr[h] = st
            sout_ref[0, h] = st.T
        else:
            for c in range(n_chunks):
                st = s0_ref[c, h].T
                states.append(st)
                sout_ref[c, h] = (st * decay[c] + upd[c]).T
        s_in = jnp.stack(states, axis=0).astype(BF16)
        o_inter = jnp.einsum('cqk,cvk->cqv', q_dec, s_in, preferred_element_type=F32)
        o = (o_intra + o_inter).reshape(tm, V_A)
        o = o * lax.rsqrt(jnp.mean(o * o, axis=-1, keepdims=True) + EPS) * gn_ref[...]
        o = o * _sigmoid(pg)
        o_ref[:, h * V_A:(h + 1) * V_A] = o.astype(BF16)

    p_next = project(0)
    for h in range(n_heads):
        p_cur = p_next
        if h + 1 < n_heads:
            p_next = project(h + 1)
        mix(h, p_cur)


def _hgrn(x2d, batch, seq, norm_w, w_in_h, lb_logits_h, g_norm, s0):
    ntok, d = x2d.shape
    n_heads = w_in_h.shape[0]
    carry = s0 is None
    if carry:
        tm = min(seq, 512)
        n_t = seq // tm
        grid = (batch, n_t)
        tok_map = lambda b, t: (b * n_t + t, 0)
        st_spec = pl.BlockSpec((1, n_heads, K_A, V_A), lambda b, t: (b, 0, 0, 0))
        scratch = [pltpu.VMEM((tm, d), BF16), pltpu.VMEM((n_heads, V_A, K_A), F32)]
        extra_in, extra_specs = [], []
    else:
        assert seq == CHUNK
        bt = min(batch, 8)
        tm = bt * CHUNK
        grid = (batch // bt, 1)
        tok_map = lambda b, t: (b, 0)
        st_spec = pl.BlockSpec((bt, n_heads, K_A, V_A), lambda b, t: (b, 0, 0, 0))
        scratch = [pltpu.VMEM((tm, d), BF16)]
        extra_in, extra_specs = [s0], [st_spec]
    n_chunks = tm // CHUNK
    kern = functools.partial(_hgrn_kernel, n_heads=n_heads, n_chunks=n_chunks, carry=carry)
    return pl.pallas_call(
        kern,
        grid=grid,
        in_specs=[pl.BlockSpec((tm, d), tok_map),
                  _const_spec(norm_w.shape), _const_spec(w_in_h.shape),
                  _const_spec(lb_logits_h.shape), _const_spec(g_norm.shape)] + extra_specs,
        out_specs=[pl.BlockSpec((tm, n_heads * V_A), tok_map), st_spec],
        out_shape=[jax.ShapeDtypeStruct((ntok, n_heads * V_A), BF16),
                   jax.ShapeDtypeStruct((batch, n_heads, K_A, V_A), F32)],
        scratch_shapes=scratch,
        compiler_params=pltpu.CompilerParams(
            dimension_semantics=("arbitrary", "arbitrary"), vmem_limit_bytes=VMEM_LIMIT),
        name="hgrn_mixer",
    )(x2d, norm_w, w_in_h, lb_logits_h, g_norm, *extra_in)


def _post_kernel(*refs, ff_block, final_norm):
    if final_norm:
        x_ref, o_ref, wo_ref, nm_ref, wup_ref, wdn_ref, nf_ref, y_ref = refs
    else:
        x_ref, o_ref, wo_ref, nm_ref, wup_ref, wdn_ref, y_ref = refs
    x1 = x_ref[...] + _dot(o_ref[...], wo_ref[...])
    inv = lax.rsqrt(jnp.mean(x1 * x1, axis=-1, keepdims=True) + EPS)
    xn = (x1 * inv * nm_ref[...]).astype(BF16)
    acc = x1
    d_ff = wup_ref.shape[1]
    for j in range(d_ff // ff_block):
        hcol = jnp.maximum(_dot(xn, wup_ref[:, j * ff_block:(j + 1) * ff_block]), 0.0)
        acc = acc + _dot((hcol * hcol).astype(BF16), wdn_ref[j * ff_block:(j + 1) * ff_block, :])
    if final_norm:
        inv = lax.rsqrt(jnp.mean(acc * acc, axis=-1, keepdims=True) + EPS)
        acc = acc * inv * nf_ref[...]
    y_ref[...] = acc


def _post(x2d, o2d, w_o, norm_mlp, w_up, w_down, norm_f=None):
    ntok, d = x2d.shape
    tm = min(ntok, 512)
    final_norm = norm_f is not None
    tok = lambda i: (i, 0)
    ins = [x2d, o2d, w_o, norm_mlp, w_up, w_down] + ([norm_f] if final_norm else [])
    specs = [pl.BlockSpec((tm, d), tok), pl.BlockSpec((tm, o2d.shape[1]), tok)]
    specs += [_const_spec(a.shape) for a in ins[2:]]
    kern = functools.partial(_post_kernel, ff_block=1024, final_norm=final_norm)
    return pl.pallas_call(
        kern,
        grid=(ntok // tm,),
        in_specs=specs,
        out_specs=pl.BlockSpec((tm, d), tok),
        out_shape=jax.ShapeDtypeStruct((ntok, d), F32),
        compiler_params=pltpu.CompilerParams(
            dimension_semantics=("arbitrary",), vmem_limit_bytes=VMEM_LIMIT),
        name="post_mlp",
    )(*ins)


LOG2E = 1.4426950408889634
AUG_BASE = (D_HB, 0)


def _split3(x):
    h1 = x.astype(BF16).astype(F32)
    r1 = x - h1
    h2 = r1.astype(BF16).astype(F32)
    h3 = (r1 - h2).astype(BF16).astype(F32)
    return h1, h2, h3


def _aug_selectors():
    rows = jnp.arange(LANES)[:, None]
    cols = jnp.arange(H_B * LANES)[None, :]
    head, lane = cols // LANES, cols % LANES
    off = lane - jnp.where(head % 2 == 0, AUG_BASE[0], AUG_BASE[1])
    piece_row = (rows == head + H_B * off) & (off >= 0) & (off < 3)
    piece_row_k = (rows == head + H_B * (off - 3)) & (off >= 3) & (off < 6)
    ones_q = (rows == 3 * H_B) & (off >= 3) & (off < 6)
    ones_k = (rows == 3 * H_B) & (off >= 0) & (off < 3)
    e_q = piece_row.astype(F32) + ones_q.astype(F32)
    e_k = ones_k.astype(F32) - piece_row_k.astype(F32)
    return e_q.astype(BF16), e_k.astype(BF16)


TOKEN_PITCH = 24


def _store_head_major(val, scr, out_ref):
    tm = val.shape[0]
    for h in range(H_B):
        pair = val[:, (h // 2) * LANES:(h // 2 + 1) * LANES]
        src = pair if h % 2 == 0 else pltpu.roll(pair, D_HB, axis=1)
        scr[pl.ds(h, tm, stride=TOKEN_PITCH), :] = src
    out_ref[...] = scr[...].reshape(tm, TOKEN_PITCH, LANES)[:, :H_B, :D_HB]


def _proj_kernel(*refs, aug):
    if aug:
        (x_ref, nkv_ref, nb_ref, wk_ref, wv_ref, wf_ref, bf_ref, wq_ref, wg_ref, eq_ref, ek_ref,
         k_ref, v_ref, lf_ref, g_ref, qa_ref, ka_ref, va_ref, k_scr, v_scr, carry) = refs
    else:
        (x_ref, nkv_ref, nb_ref, wk_ref, wv_ref, wf_ref, bf_ref, wq_ref, wg_ref,
         k_ref, v_ref, lf_ref, g_ref, q_ref, kf_ref, vf_ref, k_scr, v_scr) = refs
    tm = x_ref.shape[0]

    @pl.when((pl.program_id(0) == 0) & (pl.program_id(1) == 0))
    def _():
        k_scr[...] = jnp.zeros_like(k_scr)
        v_scr[...] = jnp.zeros_like(v_scr)

    x = x_ref[...]
    xs = x * lax.rsqrt(jnp.mean(x * x, axis=-1, keepdims=True) + EPS)
    xk = (xs * nkv_ref[...]).astype(BF16)
    xq = (xs * nb_ref[...]).astype(BF16)
    k = _dot(xk, wk_ref[...])
    v = _dot(xk, wv_ref[...])
    _store_head_major(k, k_scr, k_ref)
    _store_head_major(v, v_scr, v_ref)
    z = _dot(xk, wf_ref[...]) + bf_ref[...]
    lf = jnp.minimum(z, 0.0) - jnp.log(1.0 + jnp.exp(-jnp.abs(z)))
    lf_ref[...] = lf[:, :H_B]
    q = _dot(xq, wq_ref[...])
    g_ref[...] = _dot(xq, wg_ref[...])
    if not aug:
        q_ref[...] = q
        kf_ref[...] = k
        vf_ref[...] = v
        return

    @pl.when(pl.program_id(1) == 0)
    def _():
        carry[...] = jnp.zeros_like(carry)

    row = lax.broadcasted_iota(jnp.int32, (tm, LANES), 0)
    lane = lax.broadcasted_iota(jnp.int32, (tm, LANES), 1)
    f = lf
    s = 1
    while s < tm:
        f = f + jnp.where(row >= s, pltpu.roll(f, s, axis=0), 0.0)
        s *= 2
    f = f + carry[...]
    carry[...] = f[tm - 1:tm, :]
    h1, h2, h3 = _split3(f * LOG2E)
    pieces = jnp.where(lane < H_B, h1,
                       jnp.where(lane < 2 * H_B, pltpu.roll(h2, H_B, axis=1),
                                 jnp.where(lane < 3 * H_B, pltpu.roll(h3, 2 * H_B, axis=1),
                                           jnp.where(lane == 3 * H_B, 1.0, 0.0)))).astype(BF16)
    tail_q = _dot(pieces, eq_ref[...])
    tail_k = _dot(pieces, ek_ref[...])
    low = lane < D_HB
    one_col = [jnp.where(lane == AUG_BASE[par], 1.0, 0.0) for par in range(2)]
    qscale = (D_HB ** -0.5) * LOG2E
    for j in range(H_B // 2):
        sl = slice(j * LANES, (j + 1) * LANES)
        q2, k2, v2 = q[:, sl] * qscale, k[:, sl], v[:, sl]
        ev, od = 2 * j, 2 * j + 1
        ev_sl, od_sl = slice(ev * LANES, (ev + 1) * LANES), slice(od * LANES, (od + 1) * LANES)
        qa_ref[ev] = jnp.where(low, q2, tail_q[:, ev_sl]).astype(BF16)
        qa_ref[od] = jnp.where(low, tail_q[:, od_sl], q2).astype(BF16)
        ka_ref[ev] = jnp.where(low, k2, tail_k[:, ev_sl]).astype(BF16)
        ka_ref[od] = jnp.where(low, tail_k[:, od_sl], k2).astype(BF16)
        va_ref[ev] = jnp.where(low, v2, one_col[0]).astype(BF16)
        va_ref[od] = jnp.where(low, one_col[1], v2).astype(BF16)


def _proj(x2d, batch, seq, aug, norm_kv, norm_b, w_k, w_v, w_f, b_f, w_qq, w_qg):
    ntok, d = x2d.shape
    hd = w_k.shape[1]
    if aug:
        tm = min(seq, 256)
        n_t = seq // tm
        grid = (batch, n_t)
    else:
        tm = min(ntok, 512)
        n_t = ntok // tm
        grid = (1, n_t)
    tok = lambda b, t: (b * n_t + t, 0)
    tok4 = lambda b, t: (b * n_t + t, 0, 0)
    head_major = lambda b, t: (0, b * n_t + t, 0)
    ins = [x2d, norm_kv, norm_b, w_k, w_v, w_f, b_f, w_qq, w_qg]
    out_specs = [pl.BlockSpec((tm, H_B, D_HB), tok4), pl.BlockSpec((tm, H_B, D_HB), tok4),
                 pl.BlockSpec((tm, H_B), tok), pl.BlockSpec((tm, hd), tok)]
    out_shape = [jax.ShapeDtypeStruct((ntok, H_B, D_HB), F32), jax.ShapeDtypeStruct((ntok, H_B, D_HB), F32),
                 jax.ShapeDtypeStruct((ntok, H_B), F32), jax.ShapeDtypeStruct((ntok, hd), F32)]
    scratch = [pltpu.VMEM((tm * TOKEN_PITCH, LANES), F32)] * 2
    if aug:
        ins += list(_aug_selectors())
        out_specs += [pl.BlockSpec((H_B, tm, LANES), head_major)] * 3
        out_shape += [jax.ShapeDtypeStruct((H_B, ntok, LANES), BF16)] * 3
        scratch += [pltpu.VMEM((1, LANES), F32)]
    else:
        out_specs += [pl.BlockSpec((tm, hd), tok)] * 3
        out_shape += [jax.ShapeDtypeStruct((ntok, hd), F32)] * 3
    return pl.pallas_call(
        functools.partial(_proj_kernel, aug=aug),
        grid=grid,
        in_specs=[pl.BlockSpec((tm, d), tok)] + [_const_spec(a.shape) for a in ins[1:]],
        out_specs=out_specs,
        out_shape=out_shape,
        scratch_shapes=scratch,
        compiler_params=pltpu.CompilerParams(
            dimension_semantics=("arbitrary", "arbitrary"), vmem_limit_bytes=VMEM_LIMIT),
        name="kvq_proj",
    )(*ins)


def _cumsum_kernel(lf_ref, f_ref):
    n_blk = lf_ref.shape[2] // LANES
    upper = (lax.broadcasted_iota(jnp.int32, (LANES, LANES), 0)
             <= lax.broadcasted_iota(jnp.int32, (LANES, LANES), 1)).astype(BF16)
    run = jnp.zeros((lf_ref.shape[1], 1), F32)
    for c in range(n_blk):
        x = lf_ref[0, :, c * LANES:(c + 1) * LANES]
        h1 = x.astype(BF16)
        r1 = x - h1.astype(F32)
        h2 = r1.astype(BF16)
        h3 = (r1 - h2.astype(F32)).astype(BF16)
        cs = (_dot(h1, upper) + _dot(h2, upper)) + _dot(h3, upper) + run
        f_ref[0, :, c * LANES:(c + 1) * LANES] = cs
        run = cs[:, LANES - 1:LANES]


def _cumsum_time(lf_t):
    b, h, l = lf_t.shape
    return pl.pallas_call(
        _cumsum_kernel,
        grid=(b,),
        in_specs=[pl.BlockSpec((1, h, l), lambda i: (i, 0, 0))],
        out_specs=pl.BlockSpec((1, h, l), lambda i: (i, 0, 0)),
        out_shape=jax.ShapeDtypeStruct((b, h, l), F32),
        compiler_params=pltpu.CompilerParams(dimension_semantics=("arbitrary",)),
        name="logf_cumsum",
    )(lf_t)


def _head_column(f_tile, head):
    lane = lax.broadcasted_iota(jnp.int32, f_tile.shape, 1)
    return jnp.sum(jnp.where(lane == head, f_tile, 0.0), axis=-1, keepdims=True)


def _attn_prompt_kernel(qa_ref, ka_ref, va_ref, g_ref, o_ref, m_scr, acc_scr, *, tq):
    seq = qa_ref.shape[1]
    n_q = seq // tq
    half = tq // 2
    lane_q = lax.broadcasted_iota(jnp.int32, (tq, LANES), 1)

    def causal(rows, cols):
        return (lax.broadcasted_iota(jnp.int32, (rows, cols), 1)
                <= lax.broadcasted_iota(jnp.int32, (rows, cols), 0))

    items = []
    for qt in range(n_q):
        for kt in range(qt):
            items.append((qt, 0, tq, kt * tq, tq, None))
        items.append((qt, 0, tq, qt * tq, half, causal(tq, half)))
        items.append((qt, half, half, qt * tq + half, half, causal(half, half)))

    def scores(item):
        qt, r_lo, r_len, c0, c_len, _ = item
        return [_dot_nt(qa_ref[hf, qt * tq + r_lo:qt * tq + r_lo + r_len, :],
                        ka_ref[hf, c0:c0 + c_len, :]) for hf in range(2)]

    def update(item, s_pair):
        _, r_lo, r_len, c0, c_len, mask = item
        rows = slice(r_lo, r_lo + r_len)
        for hf in range(2):
            s2 = s_pair[hf]
            if mask is not None:
                s2 = jnp.where(mask, s2, -jnp.inf)
            cols = [s2[:, c * LANES:(c + 1) * LANES] for c in range(c_len // LANES)]
            m_old = m_scr[hf, rows, :]
            m_new = jnp.maximum(
                m_old, jnp.max(functools.reduce(jnp.maximum, cols), axis=-1, keepdims=True))
            p = jnp.concatenate([jnp.exp2(c - m_new).astype(BF16) for c in cols], axis=1)
            acc_scr[hf, rows, :] = (jnp.exp2(m_old - m_new) * acc_scr[hf, rows, :]
                                    + _dot(p, va_ref[hf, c0:c0 + c_len, :]))
            m_scr[hf, rows, :] = m_new

    def finish(qt):
        outs = []
        for hf in range(2):
            acc = acc_scr[hf]
            denom = jnp.sum(jnp.where(lane_q == AUG_BASE[hf], acc, 0.0), axis=-1, keepdims=True)
            outs.append(acc / denom)
        o = jnp.where(lane_q < D_HB, outs[0], outs[1])
        o = o * _sigmoid(g_ref[0, qt * tq:(qt + 1) * tq, :])
        o_ref[0, qt * tq:(qt + 1) * tq, :] = o.astype(BF16)

    s_next = scores(items[0])
    for i, item in enumerate(items):
        s_cur = s_next
        if i + 1 < len(items):
            s_next = scores(items[i + 1])
        if i == 0 or items[i - 1][0] != item[0]:
            m_scr[...] = jnp.full(m_scr.shape, -jnp.inf, F32)
            acc_scr[...] = jnp.zeros(acc_scr.shape, F32)
        update(item, s_cur)
        if i + 1 == len(items) or items[i + 1][0] != item[0]:
            finish(item[0])


def _attn_prompt(qa, ka, va, g):
    b, t, hd = g.shape
    tq = min(t, 512)
    n_pairs = hd // LANES
    heads = pl.BlockSpec((2, t, LANES), lambda i, j: (j, i, 0))
    col = pl.BlockSpec((1, t, LANES), lambda i, j: (i, 0, j))
    kern = functools.partial(_attn_prompt_kernel, tq=tq)
    return pl.pallas_call(
        kern,
        grid=(b, n_pairs),
        in_specs=[heads, heads, heads, col],
        out_specs=col,
        out_shape=jax.ShapeDtypeStruct((b, t, hd), BF16),
        scratch_shapes=[pltpu.VMEM((2, tq, LANES), F32), pltpu.VMEM((2, tq, LANES), F32)],
        compiler_params=pltpu.CompilerParams(
            dimension_semantics=("arbitrary", "arbitrary"), vmem_limit_bytes=VMEM_LIMIT),
        name="fox_attn_prompt",
    )(qa, ka, va, g)


def _pair_block(qs, k2, v2, fq, fk, mask, m_ref, l_ref, acc_ref, p_scr):
    tq = qs.shape[0] // 2
    s = _dot_nt(qs, k2)
    alphas = []
    for hf in range(2):
        sh = s[hf * tq:(hf + 1) * tq] + (fq[hf] - fk[hf])
        if mask is not None:
            sh = jnp.where(mask, sh, -jnp.inf)
        m_old = m_ref[hf]
        m_new = jnp.maximum(m_old, jnp.max(sh, axis=-1, keepdims=True))
        p = jnp.exp(sh - m_new)
        alpha = jnp.exp(m_old - m_new)
        l_ref[hf] = alpha * l_ref[hf] + jnp.sum(p, axis=-1, keepdims=True)
        m_ref[hf] = m_new
        p_scr[hf * tq:(hf + 1) * tq, :] = p.astype(BF16)
        alphas.append(alpha)
    pv = _dot(p_scr[...], v2)
    acc_ref[...] = jnp.concatenate(alphas, axis=0) * acc_ref[...] + pv


def _split_heads(q2, lane_a):
    zero = jnp.zeros_like(q2)
    return jnp.concatenate([jnp.where(lane_a, q2, zero), jnp.where(lane_a, zero, q2)], axis=0)


def _attn_sample_kernel(q_ref, g_ref, kc_ref, vc_ref, kn_ref, vn_ref, fq_ref, fkc_ref, fkn_ref,
                        o_ref, m_scr, l_scr, acc_scr, p_scr, pn_scr, *, n_pairs):
    kt = pl.program_id(1)
    n_kt = pl.num_programs(1)
    tq = q_ref.shape[1]
    scale = D_HB ** -0.5
    lane_a = lax.broadcasted_iota(jnp.int32, (tq, LANES), 1) < D_HB
    causal = (lax.broadcasted_iota(jnp.int32, (tq, tq), 1)
              <= lax.broadcasted_iota(jnp.int32, (tq, tq), 0))
    f_tile = fq_ref[0]

    @pl.when(kt == 0)
    def _():
        m_scr[...] = jnp.full(m_scr.shape, -jnp.inf, F32)
        l_scr[...] = jnp.zeros(l_scr.shape, F32)
        acc_scr[...] = jnp.zeros(acc_scr.shape, F32)

    def pair_inputs(hp):
        cols = slice(hp * LANES, (hp + 1) * LANES)
        qs = _split_heads(q_ref[0, :, cols] * scale, lane_a).astype(BF16)
        fq = [_head_column(f_tile, 2 * hp + hf) for hf in range(2)]
        return cols, qs, fq

    for hp in range(n_pairs):
        cols, qs, fq = pair_inputs(hp)
        fk = [fkc_ref[0, 2 * hp + hf:2 * hp + hf + 1, :] for hf in range(2)]
        _pair_block(qs, kc_ref[0, :, cols].astype(BF16), vc_ref[0, :, cols].astype(BF16),
                    fq, fk, None, m_scr.at[hp], l_scr.at[hp], acc_scr.at[hp], p_scr)

    @pl.when(kt == n_kt - 1)
    def _():
        for hp in range(n_pairs):
            cols, qs, fq = pair_inputs(hp)
            fk = [fkn_ref[0, 2 * hp + hf:2 * hp + hf + 1, 0:tq] for hf in range(2)]
            _pair_block(qs, kn_ref[0, :, cols].astype(BF16), vn_ref[0, :, cols].astype(BF16),
                        fq, fk, causal, m_scr.at[hp], l_scr.at[hp], acc_scr.at[hp], pn_scr)
            acc = acc_scr[hp]
            o = jnp.where(lane_a, acc[:tq] / l_scr[hp, 0], acc[tq:] / l_scr[hp, 1])
            o = o * _sigmoid(g_ref[0, :, cols])
            o_ref[0, :, cols] = o.astype(BF16)


def _attn_sample(q, g, k_cache, v_cache, k_new, v_new, f, f_t, past):
    b, t, hd = q.shape
    n_pairs = hd // LANES
    tk = min(past, 1024)
    n_kt = past // tk
    row = pl.BlockSpec((1, t, hd), lambda i, j: (i, 0, 0))
    cache = pl.BlockSpec((1, tk, hd), lambda i, j: (i, j, 0))
    kern = functools.partial(_attn_sample_kernel, n_pairs=n_pairs)
    return pl.pallas_call(
        kern,
        grid=(b, n_kt),
        in_specs=[row, row, cache, cache, row, row,
                  pl.BlockSpec((1, t, H_B), lambda i, j: (i, past // t, 0)),
                  pl.BlockSpec((1, H_B, tk), lambda i, j: (i, 0, j)),
                  pl.BlockSpec((1, H_B, LANES), lambda i, j: (i, 0, past // LANES))],
        out_specs=row,
        out_shape=jax.ShapeDtypeStruct((b, t, hd), BF16),
        scratch_shapes=[pltpu.VMEM((n_pairs, 2, t, 1), F32), pltpu.VMEM((n_pairs, 2, t, 1), F32),
                        pltpu.VMEM((n_pairs, 2 * t, LANES), F32),
                        pltpu.VMEM((2 * t, tk), BF16), pltpu.VMEM((2 * t, t), BF16)],
        compiler_params=pltpu.CompilerParams(
            dimension_semantics=("arbitrary", "arbitrary"), vmem_limit_bytes=VMEM_LIMIT),
        name="fox_attn_sample",
    )(q, g, k_cache, v_cache, k_new, v_new, f, f_t, f_t)


def _trunk(x, state0, k_past, v_past, logf_past, w):
    batch, seq, d = x.shape
    ntok = batch * seq
    hd = H_B * D_HB
    x2d = x.reshape(ntok, d)
    s0 = None if state0 is None else state0[:, 0]
    o_a, state = _hgrn(x2d, batch, seq, w["norm_a"], w["w_in_h"], w["lb_logits_h"], w["g_norm_a"], s0)
    x2d = _post(x2d, o_a, w["w_o_a"], w["norm_mlp0"], w["w_up0"], w["w_down0"])
    proj_w = (w["norm_kv"], w["norm_b"], w["w_k"], w["w_v"], w["w_f"], w["b_f"], w["w_qq"], w["w_qg"])
    if k_past is None:
        k, v, logf, g, qa, ka, va = _proj(x2d, batch, seq, True, *proj_w)
        o_b = _attn_prompt(qa, ka, va, g.reshape(batch, seq, hd))
    else:
        k, v, logf, g, q, k_flat, v_flat = _proj(x2d, batch, seq, False, *proj_w)
        past = k_past.shape[1]
        total = past + seq
        padded = -(-total // LANES) * LANES
        lf_all = jnp.concatenate(
            [jnp.swapaxes(logf_past.astype(F32), 1, 2),
             jnp.swapaxes(logf.reshape(batch, seq, H_B), 1, 2),
             jnp.zeros((batch, H_B, padded - total), F32)], axis=2)
        f_t = _cumsum_time(lf_all)
        o_b = _attn_sample(q.reshape(batch, seq, hd), g.reshape(batch, seq, hd),
                           k_past.reshape(batch, past, hd), v_past.reshape(batch, past, hd),
                           k_flat.reshape(batch, seq, hd), v_flat.reshape(batch, seq, hd),
                           jnp.swapaxes(f_t, 1, 2), f_t, past)
    y = _post(x2d, o_b.reshape(ntok, hd), w["w_o_b"], w["norm_mlp1"], w["w_up1"], w["w_down1"],
              w["norm_f"])
    return (y.reshape(batch, seq, d), state[:, None],
            k.reshape(batch, seq, H_B, D_HB), v.reshape(batch, seq, H_B, D_HB),
            logf.reshape(batch, seq, H_B))


def kernel(x_prompt, x_sample, state_hgrn, cache_k, cache_v, cache_logf, norm_a, w_in_a, lb_logits, g_norm_a, w_o_a, norm_kv, w_kv, b_f, norm_b, w_q_b, w_o_b, norm_mlp, w_up, w_down, norm_f):
    d = x_prompt.shape[-1]
    assert w_in_a.shape[0] == 1 and w_q_b.shape[0] == 1, "one HGRN2 layer and one FoX layer"
    n_heads = w_in_a.shape[2] // (4 * K_A)
    hd = H_B * D_HB
    row = lambda a: a.reshape(1, -1).astype(F32)
    w_f = jnp.zeros((d, LANES), F32).at[:, :H_B].set(w_kv[:, 2 * hd:])
    w = {
        "norm_a": row(norm_a[0]),
        "w_in_h": w_in_a[0].reshape(d, 4, n_heads, K_A).transpose(2, 0, 1, 3)
                           .reshape(n_heads, d, 4 * K_A).astype(BF16),
        "lb_logits_h": lb_logits.astype(F32).reshape(lb_logits.shape[0], n_heads, 1, K_A),
        "g_norm_a": row(g_norm_a[0]),
        "w_o_a": w_o_a[0].astype(BF16),
        "norm_mlp0": row(norm_mlp[0]), "w_up0": w_up[0].astype(BF16), "w_down0": w_down[0].astype(BF16),
        "norm_mlp1": row(norm_mlp[1]), "w_up1": w_up[1].astype(BF16), "w_down1": w_down[1].astype(BF16),
        "norm_kv": row(norm_kv), "norm_b": row(norm_b[0]),
        "w_k": w_kv[:, :hd].astype(BF16), "w_v": w_kv[:, hd:2 * hd].astype(BF16),
        "w_f": w_f.astype(BF16), "b_f": jnp.zeros((1, LANES), F32).at[0, :H_B].set(b_f.astype(F32)),
        "w_qq": w_q_b[0][:, :hd].astype(BF16), "w_qg": w_q_b[0][:, hd:].astype(BF16),
        "w_o_b": w_o_b[0].astype(BF16),
        "norm_f": row(norm_f),
    }
    y_p, st_p, k_p, v_p, lf_p = _trunk(x_prompt, None, None, None, None, w)
    y_s, st_s, k_s, v_s, lf_s = _trunk(x_sample, state_hgrn, cache_k, cache_v, cache_logf, w)
    return (y_p, y_s, st_p, k_p, v_p, lf_p, st_s, k_s, v_s, lf_s)
```

```python
import functools

import jax
import jax.numpy as jnp
from jax import lax
from jax.experimental import pallas as pl
from jax.experimental.pallas import tpu as pltpu

EPS = 1e-6
CHUNK = 64
K_A = 128
V_A = 128
H_B = 16
D_HB = 64
LANES = 128
VMEM_LIMIT = 56 * 1024 * 1024

F32 = jnp.float32
BF16 = jnp.bfloat16


def _dot(a, b):
    return jnp.dot(a, b, preferred_element_type=F32)


def _dot_nt(a, b):
    return lax.dot_general(a, b, (((1,), (1,)), ((), ())), preferred_element_type=F32)


def _dot_tn(a, b):
    return lax.dot_general(a, b, (((0,), (0,)), ((), ())), preferred_element_type=F32)


def _sigmoid(x):
    return 1.0 / (1.0 + jnp.exp(-x))


def _const_spec(shape):
    nd = len(shape)
    return pl.BlockSpec(shape, lambda *_: (0,) * nd, pipeline_mode=pl.Buffered(1))


def _hgrn_kernel(*refs, n_heads, n_chunks, carry):
    if carry:
        (x_ref, nrm_ref, win_ref, lbl_ref, gn_ref, o_ref, sout_ref, xn_scr, st_scr) = refs
        s0_ref = None
    else:
        (x_ref, nrm_ref, win_ref, lbl_ref, gn_ref, s0_ref, o_ref, sout_ref, xn_scr) = refs
        st_scr = None
    tm = n_chunks * CHUNK

    x = x_ref[...]
    inv = lax.rsqrt(jnp.mean(x * x, axis=-1, keepdims=True) + EPS)
    xn_scr[...] = (x * inv * nrm_ref[...]).astype(BF16)

    if carry:
        @pl.when(pl.program_id(1) == 0)
        def _():
            st_scr[...] = jnp.zeros_like(st_scr)

    row_in_chunk = lax.broadcasted_iota(jnp.int32, (tm, K_A), 0) % CHUNK
    tri = (lax.broadcasted_iota(jnp.int32, (CHUNK, CHUNK), 1)
           <= lax.broadcasted_iota(jnp.int32, (CHUNK, CHUNK), 0))

    def project(h):
        return _dot(xn_scr[...], win_ref[h])

    def mix(h, p):
        pq, pz = p[:, 0:K_A], p[:, K_A:2 * K_A]
        pi, pg = p[:, 2 * K_A:3 * K_A], p[:, 3 * K_A:4 * K_A]
        n_rows = lbl_ref.shape[0]
        lrows = [lbl_ref[r, h] for r in range(n_rows)]
        lmax = functools.reduce(jnp.maximum, lrows)
        lexp = [jnp.exp(l - lmax) for l in lrows]
        lb = lexp[0] / functools.reduce(lambda a, b: a + b, lexp)

        q = pq * _sigmoid(pq)
        f = lb + (1.0 - lb) * _sigmoid(pz)
        logf = jnp.log(f)
        k = 1.0 - f
        bc = logf
        s = 1
        while s < CHUNK:
            bc = bc + jnp.where(row_in_chunk >= s, pltpu.roll(bc, s, axis=0), 0.0)
            s *= 2
        chunked = lambda a: a.reshape(n_chunks, CHUNK, K_A)
        bc3 = chunked(bc)
        btot = bc3[:, CHUNK - 1:CHUNK, :]
        q_dec = chunked(q * jnp.exp(bc)).astype(BF16)
        k_inv = chunked(k * jnp.exp(-bc)).astype(BF16)
        k_end = (chunked(k) * jnp.exp(btot - bc3)).astype(BF16)
        decay = jnp.exp(btot)
        v = chunked(pi).astype(BF16)

        sc = jnp.einsum('cqk,csk->cqs', q_dec, k_inv, preferred_element_type=F32)
        sc = jnp.where(tri, sc, 0.0).astype(BF16)
        o_intra = jnp.einsum('cqs,csv->cqv', sc, v, preferred_element_type=F32)
        upd = jnp.einsum('csv,csk->cvk', v, k_end, preferred_element_type=F32)
        states = []
        if carry:
            st = st_scr[h]
            for c in range(n_chunks):
                states.append(st)
                st = st * decay[c] + upd[c]
            st_scr[h] = st
            sout_ref[0, h] = st.T
        else:
            for c in range(n_chunks):
                st = s0_ref[c, h].T
                states.append(st)
                sout_ref[c, h] = (st * decay[c] + upd[c]).T
        s_in = jnp.stack(states, axis=0).astype(BF16)
        o_inter = jnp.einsum('cqk,cvk->cqv', q_dec, s_in, preferred_element_type=F32)
        o = (o_intra + o_inter).reshape(tm, V_A)
        o = o * lax.rsqrt(jnp.mean(o * o, axis=-1, keepdims=True) + EPS) * gn_ref[...]
        o = o * _sigmoid(pg)
        o_ref[:, h * V_A:(h + 1) * V_A] = o.astype(BF16)

    p_next = project(0)
    for h in range(n_heads):
        p_cur = p_next
        if h + 1 < n_heads:
            p_next = project(h + 1)
        mix(h, p_cur)


def _hgrn(x2d, batch, seq, norm_w, w_in_h, lb_logits_h, g_norm, s0):
    ntok, d = x2d.shape
    n_heads = w_in_h.shape[0]
    carry = s0 is None
    if carry:
        tm = min(seq, 512)
        n_t = seq // tm
        grid = (batch, n_t)
        tok_map = lambda b, t: (b * n_t + t, 0)
        st_spec = pl.BlockSpec((1, n_heads, K_A, V_A), lambda b, t: (b, 0, 0, 0))
        scratch = [pltpu.VMEM((tm, d), BF16), pltpu.VMEM((n_heads, V_A, K_A), F32)]
        extra_in, extra_specs = [], []
    else:
        assert seq == CHUNK
        bt = min(batch, 8)
        tm = bt * CHUNK
        grid = (batch // bt, 1)
        tok_map = lambda b, t: (b, 0)
        st_spec = pl.BlockSpec((bt, n_heads, K_A, V_A), lambda b, t: (b, 0, 0, 0))
        scratch = [pltpu.VMEM((tm, d), BF16)]
        extra_in, extra_specs = [s0], [st_spec]
    n_chunks = tm // CHUNK
    kern = functools.partial(_hgrn_kernel, n_heads=n_heads, n_chunks=n_chunks, carry=carry)
    return pl.pallas_call(
        kern,
        grid=grid,
        in_specs=[pl.BlockSpec((tm, d), tok_map),
                  _const_spec(norm_w.shape), _const_spec(w_in_h.shape),
                  _const_spec(lb_logits_h.shape), _const_spec(g_norm.shape)] + extra_specs,
        out_specs=[pl.BlockSpec((tm, n_heads * V_A), tok_map), st_spec],
        out_shape=[jax.ShapeDtypeStruct((ntok, n_heads * V_A), BF16),
                   jax.ShapeDtypeStruct((batch, n_heads, K_A, V_A), F32)],
        scratch_shapes=scratch,
        compiler_params=pltpu.CompilerParams(
            dimension_semantics=("arbitrary", "arbitrary"), vmem_limit_bytes=VMEM_LIMIT),
        name="hgrn_mixer",
    )(x2d, norm_w, w_in_h, lb_logits_h, g_norm, *extra_in)


def _post_kernel(*refs, ff_block, final_norm):
    if final_norm:
        x_ref, o_ref, wo_ref, nm_ref, wup_ref, wdn_ref, nf_ref, y_ref = refs
    else:
        x_ref, o_ref, wo_ref, nm_ref, wup_ref, wdn_ref, y_ref = refs
    x1 = x_ref[...] + _dot(o_ref[...], wo_ref[...])
    inv = lax.rsqrt(jnp.mean(x1 * x1, axis=-1, keepdims=True) + EPS)
    xn = (x1 * inv * nm_ref[...]).astype(BF16)
    acc = x1
    d_ff = wup_ref.shape[1]
    for j in range(d_ff // ff_block):
        hcol = jnp.maximum(_dot(xn, wup_ref[:, j * ff_block:(j + 1) * ff_block]), 0.0)
        acc = acc + _dot((hcol * hcol).astype(BF16), wdn_ref[j * ff_block:(j + 1) * ff_block, :])
    if final_norm:
        inv = lax.rsqrt(jnp.mean(acc * acc, axis=-1, keepdims=True) + EPS)
        acc = acc * inv * nf_ref[...]
    y_ref[...] = acc


def _post(x2d, o2d, w_o, norm_mlp, w_up, w_down, norm_f=None):
    ntok, d = x2d.shape
    tm = min(ntok, 512)
    final_norm = norm_f is not None
    tok = lambda i: (i, 0)
    ins = [x2d, o2d, w_o, norm_mlp, w_up, w_down] + ([norm_f] if final_norm else [])
    specs = [pl.BlockSpec((tm, d), tok), pl.BlockSpec((tm, o2d.shape[1]), tok)]
    specs += [_const_spec(a.shape) for a in ins[2:]]
    kern = functools.partial(_post_kernel, ff_block=1024, final_norm=final_norm)
    return pl.pallas_call(
        kern,
        grid=(ntok // tm,),
        in_specs=specs,
        out_specs=pl.BlockSpec((tm, d), tok),
        out_shape=jax.ShapeDtypeStruct((ntok, d), F32),
        compiler_params=pltpu.CompilerParams(
            dimension_semantics=("arbitrary",), vmem_limit_bytes=VMEM_LIMIT),
        name="post_mlp",
    )(*ins)


LOG2E = 1.4426950408889634
AUG_BASE = (D_HB, 0)


def _split3(x):
    h1 = x.astype(BF16).astype(F32)
    r1 = x - h1
    h2 = r1.astype(BF16).astype(F32)
    h3 = (r1 - h2).astype(BF16).astype(F32)
    return h1, h2, h3


def _aug_selectors():
    rows = jnp.arange(LANES)[:, None]
    cols = jnp.arange(H_B * LANES)[None, :]
    head, lane = cols // LANES, cols % LANES
    off = lane - jnp.where(head % 2 == 0, AUG_BASE[0], AUG_BASE[1])
    piece_row = (rows == head + H_B * off) & (off >= 0) & (off < 3)
    piece_row_k = (rows == head + H_B * (off - 3)) & (off >= 3) & (off < 6)
    ones_q = (rows == 3 * H_B) & (off >= 3) & (off < 6)
    ones_k = (rows == 3 * H_B) & (off >= 0) & (off < 3)
    e_q = piece_row.astype(F32) + ones_q.astype(F32)
    e_k = ones_k.astype(F32) - piece_row_k.astype(F32)
    return e_q.astype(BF16), e_k.astype(BF16)


TOKEN_PITCH = 24


def _head_pitch(tm):
    return tm + 8 if (tm // 8) % 2 == 0 else tm


def _store_head_major(val, scr, out_ref):
    tm = val.shape[0]
    pitch = _head_pitch(tm)
    for h in range(H_B):
        pair = val[:, (h // 2) * LANES:(h // 2 + 1) * LANES]
        scr[h * pitch:h * pitch + tm, :] = pair if h % 2 == 0 else pltpu.roll(pair, D_HB, axis=1)

    for t in range(tm):
        for grp in range(H_B // 8):
            rows = scr[pl.ds(grp * 8 * pitch + t, 8, stride=pitch), :]
            out_ref[t, grp * 8:(grp + 1) * 8, :] = rows[:, :D_HB]


def _proj_kernel(*refs, aug):
    if aug:
        (x_ref, nkv_ref, nb_ref, wk_ref, wv_ref, wf_ref, bf_ref, wq_ref, wg_ref, eq_ref, ek_ref,
         k_ref, v_ref, lf_ref, g_ref, qa_ref, ka_ref, va_ref, k_scr, v_scr, carry) = refs
    else:
        (x_ref, nkv_ref, nb_ref, wk_ref, wv_ref, wf_ref, bf_ref, wq_ref, wg_ref,
         k_ref, v_ref, lf_ref, g_ref, q_ref, kf_ref, vf_ref, k_scr, v_scr) = refs
    tm = x_ref.shape[0]
    x = x_ref[...]
    xs = x * lax.rsqrt(jnp.mean(x * x, axis=-1, keepdims=True) + EPS)
    xk = (xs * nkv_ref[...]).astype(BF16)
    xq = (xs * nb_ref[...]).astype(BF16)
    k = _dot(xk, wk_ref[...])
    v = _dot(xk, wv_ref[...])
    _store_head_major(k, k_scr, k_ref)
    _store_head_major(v, v_scr, v_ref)
    z = _dot(xk, wf_ref[...]) + bf_ref[...]
    lf = jnp.minimum(z, 0.0) - jnp.log(1.0 + jnp.exp(-jnp.abs(z)))
    lf_ref[...] = lf[:, :H_B]
    q = _dot(xq, wq_ref[...])
    g_ref[...] = _dot(xq, wg_ref[...])
    if not aug:
        q_ref[...] = q
        kf_ref[...] = k
        vf_ref[...] = v
        return

    @pl.when(pl.program_id(1) == 0)
    def _():
        carry[...] = jnp.zeros_like(carry)

    row = lax.broadcasted_iota(jnp.int32, (tm, LANES), 0)
    lane = lax.broadcasted_iota(jnp.int32, (tm, LANES), 1)
    f = lf
    s = 1
    while s < tm:
        f = f + jnp.where(row >= s, pltpu.roll(f, s, axis=0), 0.0)
        s *= 2
    f = f + carry[...]
    carry[...] = f[tm - 1:tm, :]
    h1, h2, h3 = _split3(f * LOG2E)
    pieces = jnp.where(lane < H_B, h1,
                       jnp.where(lane < 2 * H_B, pltpu.roll(h2, H_B, axis=1),
                                 jnp.where(lane < 3 * H_B, pltpu.roll(h3, 2 * H_B, axis=1),
                                           jnp.where(lane == 3 * H_B, 1.0, 0.0)))).astype(BF16)
    tail_q = _dot(pieces, eq_ref[...])
    tail_k = _dot(pieces, ek_ref[...])
    low = lane < D_HB
    one_col = [jnp.where(lane == AUG_BASE[par], 1.0, 0.0) for par in range(2)]
    qscale = (D_HB ** -0.5) * LOG2E
    for j in range(H_B // 2):
        sl = slice(j * LANES, (j + 1) * LANES)
        q2, k2, v2 = q[:, sl] * qscale, k[:, sl], v[:, sl]
        ev, od = 2 * j, 2 * j + 1
        ev_sl, od_sl = slice(ev * LANES, (ev + 1) * LANES), slice(od * LANES, (od + 1) * LANES)
        qa_ref[ev] = jnp.where(low, q2, tail_q[:, ev_sl]).astype(BF16)
        qa_ref[od] = jnp.where(low, tail_q[:, od_sl], q2).astype(BF16)
        ka_ref[ev] = jnp.where(low, k2, tail_k[:, ev_sl]).astype(BF16)
        ka_ref[od] = jnp.where(low, tail_k[:, od_sl], k2).astype(BF16)
        va_ref[ev] = jnp.where(low, v2, one_col[0]).astype(BF16)
        va_ref[od] = jnp.where(low, one_col[1], v2).astype(BF16)


def _proj(x2d, batch, seq, aug, norm_kv, norm_b, w_k, w_v, w_f, b_f, w_qq, w_qg):
    ntok, d = x2d.shape
    hd = w_k.shape[1]
    if aug:
        tm = min(seq, 256)
        n_t = seq // tm
        grid = (batch, n_t)
    else:
        tm = min(ntok, 512)
        n_t = ntok // tm
        grid = (1, n_t)
    tok = lambda b, t: (b * n_t + t, 0)
    tok4 = lambda b, t: (b * n_t + t, 0, 0)
    head_major = lambda b, t: (0, b * n_t + t, 0)
    ins = [x2d, norm_kv, norm_b, w_k, w_v, w_f, b_f, w_qq, w_qg]
    out_specs = [pl.BlockSpec((tm, H_B, D_HB), tok4), pl.BlockSpec((tm, H_B, D_HB), tok4),
                 pl.BlockSpec((tm, H_B), tok), pl.BlockSpec((tm, hd), tok)]
    out_shape = [jax.ShapeDtypeStruct((ntok, H_B, D_HB), F32), jax.ShapeDtypeStruct((ntok, H_B, D_HB), F32),
                 jax.ShapeDtypeStruct((ntok, H_B), F32), jax.ShapeDtypeStruct((ntok, hd), F32)]
    scratch = [pltpu.VMEM((H_B * _head_pitch(tm), LANES), F32)] * 2
    if aug:
        ins += list(_aug_selectors())
        out_specs += [pl.BlockSpec((H_B, tm, LANES), head_major)] * 3
        out_shape += [jax.ShapeDtypeStruct((H_B, ntok, LANES), BF16)] * 3
        scratch += [pltpu.VMEM((1, LANES), F32)]
    else:
        out_specs += [pl.BlockSpec((tm, hd), tok)] * 3
        out_shape += [jax.ShapeDtypeStruct((ntok, hd), F32)] * 3
    return pl.pallas_call(
        functools.partial(_proj_kernel, aug=aug),
        grid=grid,
        in_specs=[pl.BlockSpec((tm, d), tok)] + [_const_spec(a.shape) for a in ins[1:]],
        out_specs=out_specs,
        out_shape=out_shape,
        scratch_shapes=scratch,
        compiler_params=pltpu.CompilerParams(
            dimension_semantics=("arbitrary", "arbitrary"), vmem_limit_bytes=VMEM_LIMIT),
        name="kvq_proj",
    )(*ins)


def _cumsum_kernel(lf_ref, f_ref):
    n_blk = lf_ref.shape[2] // LANES
    upper = (lax.broadcasted_iota(jnp.int32, (LANES, LANES), 0)
             <= lax.broadcasted_iota(jnp.int32, (LANES, LANES), 1)).astype(BF16)
    run = jnp.zeros((lf_ref.shape[1], 1), F32)
    for c in range(n_blk):
        x = lf_ref[0, :, c * LANES:(c + 1) * LANES]
        h1 = x.astype(BF16)
        r1 = x - h1.astype(F32)
        h2 = r1.astype(BF16)
        h3 = (r1 - h2.astype(F32)).astype(BF16)
        cs = (_dot(h1, upper) + _dot(h2, upper)) + _dot(h3, upper) + run
        f_ref[0, :, c * LANES:(c + 1) * LANES] = cs
        run = cs[:, LANES - 1:LANES]


def _cumsum_time(lf_t):
    b, h, l = lf_t.shape
    return pl.pallas_call(
        _cumsum_kernel,
        grid=(b,),
        in_specs=[pl.BlockSpec((1, h, l), lambda i: (i, 0, 0))],
        out_specs=pl.BlockSpec((1, h, l), lambda i: (i, 0, 0)),
        out_shape=jax.ShapeDtypeStruct((b, h, l), F32),
        compiler_params=pltpu.CompilerParams(dimension_semantics=("arbitrary",)),
        name="logf_cumsum",
    )(lf_t)


def _head_column(f_tile, head):
    lane = lax.broadcasted_iota(jnp.int32, f_tile.shape, 1)
    return jnp.sum(jnp.where(lane == head, f_tile, 0.0), axis=-1, keepdims=True)


def _attn_prompt_kernel(qa_ref, ka_ref, va_ref, g_ref, o_ref, m_scr, acc_scr, *, tq):
    seq = qa_ref.shape[1]
    n_q = seq // tq
    half = tq // 2
    lane_q = lax.broadcasted_iota(jnp.int32, (tq, LANES), 1)

    def causal(rows, cols):
        return (lax.broadcasted_iota(jnp.int32, (rows, cols), 1)
                <= lax.broadcasted_iota(jnp.int32, (rows, cols), 0))

    items = []
    for qt in range(n_q):
        for kt in range(qt):
            items.append((qt, 0, tq, kt * tq, tq, None))
        items.append((qt, 0, tq, qt * tq, half, causal(tq, half)))
        items.append((qt, half, half, qt * tq + half, half, causal(half, half)))

    def scores(item):
        qt, r_lo, r_len, c0, c_len, _ = item
        return [_dot_nt(qa_ref[hf, qt * tq + r_lo:qt * tq + r_lo + r_len, :],
                        ka_ref[hf, c0:c0 + c_len, :]) for hf in range(2)]

    def update(item, s_pair):
        _, r_lo, r_len, c0, c_len, mask = item
        rows = slice(r_lo, r_lo + r_len)
        for hf in range(2):
            s2 = s_pair[hf]
            if mask is not None:
                s2 = jnp.where(mask, s2, -jnp.inf)
            cols = [s2[:, c * LANES:(c + 1) * LANES] for c in range(c_len // LANES)]
            m_old = m_scr[hf, rows, :]
            m_new = jnp.maximum(
                m_old, jnp.max(functools.reduce(jnp.maximum, cols), axis=-1, keepdims=True))
            p = jnp.concatenate([jnp.exp2(c - m_new).astype(BF16) for c in cols], axis=1)
            acc_scr[hf, rows, :] = (jnp.exp2(m_old - m_new) * acc_scr[hf, rows, :]
                                    + _dot(p, va_ref[hf, c0:c0 + c_len, :]))
            m_scr[hf, rows, :] = m_new

    def finish(qt):
        outs = []
        for hf in range(2):
            acc = acc_scr[hf]
            denom = jnp.sum(jnp.where(lane_q == AUG_BASE[hf], acc, 0.0), axis=-1, keepdims=True)
            outs.append(acc / denom)
        o = jnp.where(lane_q < D_HB, outs[0], outs[1])
        o = o * _sigmoid(g_ref[0, qt * tq:(qt + 1) * tq, :])
        o_ref[0, qt * tq:(qt + 1) * tq, :] = o.astype(BF16)

    s_next = scores(items[0])
    for i, item in enumerate(items):
        s_cur = s_next
        if i + 1 < len(items):
            s_next = scores(items[i + 1])
        if i == 0 or items[i - 1][0] != item[0]:
            m_scr[...] = jnp.full(m_scr.shape, -jnp.inf, F32)
            acc_scr[...] = jnp.zeros(acc_scr.shape, F32)
        update(item, s_cur)
        if i + 1 == len(items) or items[i + 1][0] != item[0]:
            finish(item[0])


def _attn_prompt(qa, ka, va, g):
    b, t, hd = g.shape
    tq = min(t, 512)
    n_pairs = hd // LANES
    heads = pl.BlockSpec((2, t, LANES), lambda i, j: (j, i, 0))
    col = pl.BlockSpec((1, t, LANES), lambda i, j: (i, 0, j))
    kern = functools.partial(_attn_prompt_kernel, tq=tq)
    return pl.pallas_call(
        kern,
        grid=(b, n_pairs),
        in_specs=[heads, heads, heads, col],
        out_specs=col,
        out_shape=jax.ShapeDtypeStruct((b, t, hd), BF16),
        scratch_shapes=[pltpu.VMEM((2, tq, LANES), F32), pltpu.VMEM((2, tq, LANES), F32)],
        compiler_params=pltpu.CompilerParams(
            dimension_semantics=("arbitrary", "arbitrary"), vmem_limit_bytes=VMEM_LIMIT),
        name="fox_attn_prompt",
    )(qa, ka, va, g)


def _pair_block(qs, k2, v2, fq, fk, mask, m_ref, l_ref, acc_ref):
    tq = qs.shape[0] // 2
    tk = k2.shape[0]
    width = min(tk, LANES)
    s = _dot_nt(qs, k2)
    alphas, probs = [], []
    for hf in range(2):
        sh = s[hf * tq:(hf + 1) * tq] + (fq[hf] - fk[hf])
        if mask is not None:
            sh = jnp.where(mask, sh, -jnp.inf)
        cols = [sh[:, c * width:(c + 1) * width] for c in range(tk // width)]
        m_old = m_ref[hf]
        m_new = jnp.maximum(
            m_old, jnp.max(functools.reduce(jnp.maximum, cols), axis=-1, keepdims=True))
        p_cols = [jnp.exp(c - m_new[:, :width]) for c in cols]
        alpha = jnp.exp(m_old - m_new)
        l_ref[hf] = alpha * l_ref[hf] + jnp.sum(
            functools.reduce(lambda a, b: a + b, p_cols), axis=-1, keepdims=True)
        m_ref[hf] = m_new
        probs.append(jnp.concatenate([p.astype(BF16) for p in p_cols], axis=1))
        alphas.append(alpha)
    pv = _dot(jnp.concatenate(probs, axis=0), v2)
    acc_ref[...] = jnp.concatenate(alphas, axis=0) * acc_ref[...] + pv


def _split_heads(q2, lane_a):
    zero = jnp.zeros_like(q2)
    return jnp.concatenate([jnp.where(lane_a, q2, zero), jnp.where(lane_a, zero, q2)], axis=0)


def _attn_sample_kernel(q_ref, g_ref, kc_ref, vc_ref, kn_ref, vn_ref, fq_ref, fkc_ref, fkn_ref,
                        o_ref, m_scr, l_scr, acc_scr, k_rows, v_rows, *, n_pairs):
    kt = pl.program_id(1)
    n_kt = pl.num_programs(1)
    tq = q_ref.shape[1]
    tk = kc_ref.shape[1]
    scale = D_HB ** -0.5
    lane_a = lax.broadcasted_iota(jnp.int32, (tq, LANES), 1) < D_HB
    lane_k = lax.broadcasted_iota(jnp.int32, (tk, LANES), 1) < D_HB
    causal = (lax.broadcasted_iota(jnp.int32, (tq, tq), 1)
              <= lax.broadcasted_iota(jnp.int32, (tq, tq), 0))
    f_tile = fq_ref[0]

    @pl.when((pl.program_id(0) == 0) & (kt == 0))
    def _():
        k_rows[...] = jnp.zeros_like(k_rows)
        v_rows[...] = jnp.zeros_like(v_rows)

    @pl.when(kt == 0)
    def _():
        m_scr[...] = jnp.full(m_scr.shape, -jnp.inf, F32)
        l_scr[...] = jnp.zeros(l_scr.shape, F32)
        acc_scr[...] = jnp.zeros(acc_scr.shape, F32)

    def copy_slab(t, c_):
        r0 = pl.multiple_of(t * TOKEN_PITCH, 8)
        k_rows[pl.ds(r0, H_B), 0:D_HB] = kc_ref[0, t]
        v_rows[pl.ds(r0, H_B), 0:D_HB] = vc_ref[0, t]
        return c_

    lax.fori_loop(0, tk, copy_slab, 0, unroll=8)

    def cache_pair(rows_ref, hp):
        ev = rows_ref[pl.ds(2 * hp, tk, stride=TOKEN_PITCH), :]
        od = rows_ref[pl.ds(2 * hp + 1, tk, stride=TOKEN_PITCH), :]
        return jnp.where(lane_k, ev, pltpu.roll(od, D_HB, axis=1)).astype(BF16)

    def pair_inputs(hp):
        cols = slice(hp * LANES, (hp + 1) * LANES)
        qs = _split_heads(q_ref[0, :, cols] * scale, lane_a).astype(BF16)
        fq = [_head_column(f_tile, 2 * hp + hf) for hf in range(2)]
        return cols, qs, fq

    for hp in range(n_pairs):
        cols, qs, fq = pair_inputs(hp)
        fk = [fkc_ref[0, 2 * hp + hf:2 * hp + hf + 1, :] for hf in range(2)]
        _pair_block(qs, cache_pair(k_rows, hp), cache_pair(v_rows, hp),
                    fq, fk, None, m_scr.at[hp], l_scr.at[hp], acc_scr.at[hp])

    @pl.when(kt == n_kt - 1)
    def _():
        for hp in range(n_pairs):
            cols, qs, fq = pair_inputs(hp)
            fk = [fkn_ref[0, 2 * hp + hf:2 * hp + hf + 1, 0:tq] for hf in range(2)]
            _pair_block(qs, kn_ref[0, :, cols].astype(BF16), vn_ref[0, :, cols].astype(BF16),
                        fq, fk, causal, m_scr.at[hp], l_scr.at[hp], acc_scr.at[hp])
            acc = acc_scr[hp]
            o = jnp.where(lane_a, acc[:tq] / l_scr[hp, 0], acc[tq:] / l_scr[hp, 1])
            o = o * _sigmoid(g_ref[0, :, cols])
            o_ref[0, :, cols] = o.astype(BF16)


def _attn_sample(q, g, k_cache, v_cache, k_new, v_new, f, f_t, past):
    b, t, hd = q.shape
    n_pairs = hd // LANES
    tk = min(past, 512)
    n_kt = past // tk
    row = pl.BlockSpec((1, t, hd), lambda i, j: (i, 0, 0))
    cache = pl.BlockSpec((1, tk, H_B, D_HB), lambda i, j: (i, j, 0, 0))
    kern = functools.partial(_attn_sample_kernel, n_pairs=n_pairs)
    return pl.pallas_call(
        kern,
        grid=(b, n_kt),
        in_specs=[row, row, cache, cache, row, row,
                  pl.BlockSpec((1, t, H_B), lambda i, j: (i, past // t, 0)),
                  pl.BlockSpec((1, H_B, tk), lambda i, j: (i, 0, j)),
                  pl.BlockSpec((1, H_B, LANES), lambda i, j: (i, 0, past // LANES))],
        out_specs=row,
        out_shape=jax.ShapeDtypeStruct((b, t, hd), BF16),
        scratch_shapes=[pltpu.VMEM((n_pairs, 2, t, LANES), F32), pltpu.VMEM((n_pairs, 2, t, LANES), F32),
                        pltpu.VMEM((n_pairs, 2 * t, LANES), F32),
                        pltpu.VMEM((tk * TOKEN_PITCH, LANES), F32),
                        pltpu.VMEM((tk * TOKEN_PITCH, LANES), F32)],
        compiler_params=pltpu.CompilerParams(
            dimension_semantics=("arbitrary", "arbitrary"), vmem_limit_bytes=VMEM_LIMIT),
        name="fox_attn_sample",
    )(q, g, k_cache, v_cache, k_new, v_new, f, f_t, f_t)


def _trunk(x, state0, k_past, v_past, logf_past, w):
    batch, seq, d = x.shape
    ntok = batch * seq
    hd = H_B * D_HB
    x2d = x.reshape(ntok, d)
    s0 = None if state0 is None else state0[:, 0]
    o_a, state = _hgrn(x2d, batch, seq, w["norm_a"], w["w_in_h"], w["lb_logits_h"], w["g_norm_a"], s0)
    x2d = _post(x2d, o_a, w["w_o_a"], w["norm_mlp0"], w["w_up0"], w["w_down0"])
    proj_w = (w["norm_kv"], w["norm_b"], w["w_k"], w["w_v"], w["w_f"], w["b_f"], w["w_qq"], w["w_qg"])
    if k_past is None:
        k, v, logf, g, qa, ka, va = _proj(x2d, batch, seq, True, *proj_w)
        o_b = _attn_prompt(qa, ka, va, g.reshape(batch, seq, hd))
    else:
        k, v, logf, g, q, k_flat, v_flat = _proj(x2d, batch, seq, False, *proj_w)
        past = k_past.shape[1]
        total = past + seq
        padded = -(-total // LANES) * LANES
        lf_all = jnp.concatenate(
            [jnp.swapaxes(logf_past.astype(F32), 1, 2),
             jnp.swapaxes(logf.reshape(batch, seq, H_B), 1, 2),
             jnp.zeros((batch, H_B, padded - total), F32)], axis=2)
        f_t = _cumsum_time(lf_all)
        o_b = _attn_sample(q.reshape(batch, seq, hd), g.reshape(batch, seq, hd),
                           k_past.astype(F32), v_past.astype(F32),
                           k_flat.reshape(batch, seq, hd), v_flat.reshape(batch, seq, hd),
                           jnp.swapaxes(f_t, 1, 2), f_t, past)
    y = _post(x2d, o_b.reshape(ntok, hd), w["w_o_b"], w["norm_mlp1"], w["w_up1"], w["w_down1"],
              w["norm_f"])
    return (y.reshape(batch, seq, d), state[:, None],
            k.reshape(batch, seq, H_B, D_HB), v.reshape(batch, seq, H_B, D_HB),
            logf.reshape(batch, seq, H_B))


def kernel(x_prompt, x_sample, state_hgrn, cache_k, cache_v, cache_logf, norm_a, w_in_a, lb_logits, g_norm_a, w_o_a, norm_kv, w_kv, b_f, norm_b, w_q_b, w_o_b, norm_mlp, w_up, w_down, norm_f):
    d = x_prompt.shape[-1]
    assert w_in_a.shape[0] == 1 and w_q_b.shape[0] == 1, "one HGRN2 layer and one FoX layer"
    n_heads = w_in_a.shape[2] // (4 * K_A)
    hd = H_B * D_HB
    row = lambda a: a.reshape(1, -1).astype(F32)
    w_f = jnp.zeros((d, LANES), F32).at[:, :H_B].set(w_kv[:, 2 * hd:])
    w = {
        "norm_a": row(norm_a[0]),
        "w_in_h": w_in_a[0].reshape(d, 4, n_heads, K_A).transpose(2, 0, 1, 3)
                           .reshape(n_heads, d, 4 * K_A).astype(BF16),
        "lb_logits_h": lb_logits.astype(F32).reshape(lb_logits.shape[0], n_heads, 1, K_A),
        "g_norm_a": row(g_norm_a[0]),
        "w_o_a": w_o_a[0].astype(BF16),
        "norm_mlp0": row(norm_mlp[0]), "w_up0": w_up[0].astype(BF16), "w_down0": w_down[0].astype(BF16),
        "norm_mlp1": row(norm_mlp[1]), "w_up1": w_up[1].astype(BF16), "w_down1": w_down[1].astype(BF16),
        "norm_kv": row(norm_kv), "norm_b": row(norm_b[0]),
        "w_k": w_kv[:, :hd].astype(BF16), "w_v": w_kv[:, hd:2 * hd].astype(BF16),
        "w_f": w_f.astype(BF16), "b_f": jnp.zeros((1, LANES), F32).at[0, :H_B].set(b_f.astype(F32)),
        "w_qq": w_q_b[0][:, :hd].astype(BF16), "w_qg": w_q_b[0][:, hd:].astype(BF16),
        "w_o_b": w_o_b[0].astype(BF16),
        "norm_f": row(norm_f),
    }
    y_p, st_p, k_p, v_p, lf_p = _trunk(x_prompt, None, None, None, None, w)
    y_s, st_s, k_s, v_s, lf_s = _trunk(x_sample, state_hgrn, cache_k, cache_v, cache_logf, w)
    return (y_p, y_s, st_p, k_p, v_p, lf_p, st_s, k_s, v_s, lf_s)
```

```python
import functools

import jax
import jax.numpy as jnp
from jax import lax
from jax.experimental import pallas as pl
from jax.experimental.pallas import tpu as pltpu

EPS = 1e-6
CHUNK = 64
K_A = 128
V_A = 128
H_B = 16
D_HB = 64
LANES = 128
VMEM_LIMIT = 56 * 1024 * 1024

F32 = jnp.float32
BF16 = jnp.bfloat16


def _dot(a, b):
    return jnp.dot(a, b, preferred_element_type=F32)


def _dot_nt(a, b):
    return lax.dot_general(a, b, (((1,), (1,)), ((), ())), preferred_element_type=F32)


def _dot_tn(a, b):
    return lax.dot_general(a, b, (((0,), (0,)), ((), ())), preferred_element_type=F32)


def _sigmoid(x):
    return 1.0 / (1.0 + jnp.exp(-x))


def _const_spec(shape):
    nd = len(shape)
    return pl.BlockSpec(shape, lambda *_: (0,) * nd, pipeline_mode=pl.Buffered(1))


def _hgrn_kernel(*refs, n_heads, n_chunks, carry):
    if carry:
        (x_ref, nrm_ref, win_ref, lbl_ref, gn_ref, o_ref, sout_ref, xn_scr, st_scr) = refs
        s0_ref = None
    else:
        (x_ref, nrm_ref, win_ref, lbl_ref, gn_ref, s0_ref, o_ref, sout_ref, xn_scr) = refs
        st_scr = None
    tm = n_chunks * CHUNK

    x = x_ref[...]
    inv = lax.rsqrt(jnp.mean(x * x, axis=-1, keepdims=True) + EPS)
    xn_scr[...] = (x * inv * nrm_ref[...]).astype(BF16)

    if carry:
        @pl.when(pl.program_id(1) == 0)
        def _():
            st_scr[...] = jnp.zeros_like(st_scr)

    row_in_chunk = lax.broadcasted_iota(jnp.int32, (tm, K_A), 0) % CHUNK
    tri = (lax.broadcasted_iota(jnp.int32, (CHUNK, CHUNK), 1)
           <= lax.broadcasted_iota(jnp.int32, (CHUNK, CHUNK), 0))

    def project(h):
        return _dot(xn_scr[...], win_ref[h])

    def mix(h, p):
        pq, pz = p[:, 0:K_A], p[:, K_A:2 * K_A]
        pi, pg = p[:, 2 * K_A:3 * K_A], p[:, 3 * K_A:4 * K_A]
        n_rows = lbl_ref.shape[0]
        lrows = [lbl_ref[r, h] for r in range(n_rows)]
        lmax = functools.reduce(jnp.maximum, lrows)
        lexp = [jnp.exp(l - lmax) for l in lrows]
        lb = lexp[0] / functools.reduce(lambda a, b: a + b, lexp)

        q = pq * _sigmoid(pq)
        f = lb + (1.0 - lb) * _sigmoid(pz)
        logf = jnp.log(f)
        k = 1.0 - f
        bc = logf
        s = 1
        while s < CHUNK:
            bc = bc + jnp.where(row_in_chunk >= s, pltpu.roll(bc, s, axis=0), 0.0)
            s *= 2
        chunked = lambda a: a.reshape(n_chunks, CHUNK, K_A)
        bc3 = chunked(bc)
        btot = bc3[:, CHUNK - 1:CHUNK, :]
        q_dec = chunked(q * jnp.exp(bc)).astype(BF16)
        k_inv = chunked(k * jnp.exp(-bc)).astype(BF16)
        k_end = (chunked(k) * jnp.exp(btot - bc3)).astype(BF16)
        decay = jnp.exp(btot)
        v = chunked(pi).astype(BF16)

        sc = jnp.einsum('cqk,csk->cqs', q_dec, k_inv, preferred_element_type=F32)
        sc = jnp.where(tri, sc, 0.0).astype(BF16)
        o_intra = jnp.einsum('cqs,csv->cqv', sc, v, preferred_element_type=F32)
        upd = jnp.einsum('csv,csk->cvk', v, k_end, preferred_element_type=F32)
        states = []
        if carry:
            st = st_scr[h]
            for c in range(n_chunks):
                states.append(st)
                st = st * decay[c] + upd[c]
            st_scr[h] = st
            sout_ref[0, h] = st.T
        else:
            for c in range(n_chunks):
                st = s0_ref[c, h].T
                states.append(st)
                sout_ref[c, h] = (st * decay[c] + upd[c]).T
        s_in = jnp.stack(states, axis=0).astype(BF16)
        o_inter = jnp.einsum('cqk,cvk->cqv', q_dec, s_in, preferred_element_type=F32)
        o = (o_intra + o_inter).reshape(tm, V_A)
        o = o * lax.rsqrt(jnp.mean(o * o, axis=-1, keepdims=True) + EPS) * gn_ref[...]
        o = o * _sigmoid(pg)
        o_ref[:, h * V_A:(h + 1) * V_A] = o.astype(BF16)

    p_next = project(0)
    for h in range(n_heads):
        p_cur = p_next
        if h + 1 < n_heads:
            p_next = project(h + 1)
        mix(h, p_cur)


def _hgrn(x2d, batch, seq, norm_w, w_in_h, lb_logits_h, g_norm, s0):
    ntok, d = x2d.shape
    n_heads = w_in_h.shape[0]
    carry = s0 is None
    if carry:
        tm = min(seq, 512)
        n_t = seq // tm
        grid = (batch, n_t)
        tok_map = lambda b, t: (b * n_t + t, 0)
        st_spec = pl.BlockSpec((1, n_heads, K_A, V_A), lambda b, t: (b, 0, 0, 0))
        scratch = [pltpu.VMEM((tm, d), BF16), pltpu.VMEM((n_heads, V_A, K_A), F32)]
        extra_in, extra_specs = [], []
    else:
        assert seq == CHUNK
        bt = min(batch, 8)
        tm = bt * CHUNK
        grid = (batch // bt, 1)
        tok_map = lambda b, t: (b, 0)
        st_spec = pl.BlockSpec((bt, n_heads, K_A, V_A), lambda b, t: (b, 0, 0, 0))
        scratch = [pltpu.VMEM((tm, d), BF16)]
        extra_in, extra_specs = [s0], [st_spec]
    n_chunks = tm // CHUNK
    kern = functools.partial(_hgrn_kernel, n_heads=n_heads, n_chunks=n_chunks, carry=carry)
    return pl.pallas_call(
        kern,
        grid=grid,
        in_specs=[pl.BlockSpec((tm, d), tok_map),
                  _const_spec(norm_w.shape), _const_spec(w_in_h.shape),
                  _const_spec(lb_logits_h.shape), _const_spec(g_norm.shape)] + extra_specs,
        out_specs=[pl.BlockSpec((tm, n_heads * V_A), tok_map), st_spec],
        out_shape=[jax.ShapeDtypeStruct((ntok, n_heads * V_A), BF16),
                   jax.ShapeDtypeStruct((batch, n_heads, K_A, V_A), F32)],
        scratch_shapes=scratch,
        compiler_params=pltpu.CompilerParams(
            dimension_semantics=("arbitrary", "arbitrary"), vmem_limit_bytes=VMEM_LIMIT),
        name="hgrn_mixer",
    )(x2d, norm_w, w_in_h, lb_logits_h, g_norm, *extra_in)


def _post_kernel(*refs, ff_block, final_norm):
    if final_norm:
        x_ref, o_ref, wo_ref, nm_ref, wup_ref, wdn_ref, nf_ref, y_ref = refs
    else:
        x_ref, o_ref, wo_ref, nm_ref, wup_ref, wdn_ref, y_ref = refs
    x1 = x_ref[...] + _dot(o_ref[...], wo_ref[...])
    inv = lax.rsqrt(jnp.mean(x1 * x1, axis=-1, keepdims=True) + EPS)
    xn = (x1 * inv * nm_ref[...]).astype(BF16)
    acc = x1
    d_ff = wup_ref.shape[1]
    for j in range(d_ff // ff_block):
        hcol = jnp.maximum(_dot(xn, wup_ref[:, j * ff_block:(j + 1) * ff_block]), 0.0)
        acc = acc + _dot((hcol * hcol).astype(BF16), wdn_ref[j * ff_block:(j + 1) * ff_block, :])
    if final_norm:
        inv = lax.rsqrt(jnp.mean(acc * acc, axis=-1, keepdims=True) + EPS)
        acc = acc * inv * nf_ref[...]
    y_ref[...] = acc


def _post(x2d, o2d, w_o, norm_mlp, w_up, w_down, norm_f=None):
    ntok, d = x2d.shape
    tm = min(ntok, 512)
    final_norm = norm_f is not None
    tok = lambda i: (i, 0)
    ins = [x2d, o2d, w_o, norm_mlp, w_up, w_down] + ([norm_f] if final_norm else [])
    specs = [pl.BlockSpec((tm, d), tok), pl.BlockSpec((tm, o2d.shape[1]), tok)]
    specs += [_const_spec(a.shape) for a in ins[2:]]
    kern = functools.partial(_post_kernel, ff_block=1024, final_norm=final_norm)
    return pl.pallas_call(
        kern,
        grid=(ntok // tm,),
        in_specs=specs,
        out_specs=pl.BlockSpec((tm, d), tok),
        out_shape=jax.ShapeDtypeStruct((ntok, d), F32),
        compiler_params=pltpu.CompilerParams(
            dimension_semantics=("arbitrary",), vmem_limit_bytes=VMEM_LIMIT),
        name="post_mlp",
    )(*ins)


LOG2E = 1.4426950408889634
AUG_BASE = (D_HB, 0)


def _split3(x):
    h1 = x.astype(BF16).astype(F32)
    r1 = x - h1
    h2 = r1.astype(BF16).astype(F32)
    h3 = (r1 - h2).astype(BF16).astype(F32)
    return h1, h2, h3


def _aug_selectors():
    rows = jnp.arange(LANES)[:, None]
    cols = jnp.arange(H_B * LANES)[None, :]
    head, lane = cols // LANES, cols % LANES
    off = lane - jnp.where(head % 2 == 0, AUG_BASE[0], AUG_BASE[1])
    piece_row = (rows == head + H_B * off) & (off >= 0) & (off < 3)
    piece_row_k = (rows == head + H_B * (off - 3)) & (off >= 3) & (off < 6)
    ones_q = (rows == 3 * H_B) & (off >= 3) & (off < 6)
    ones_k = (rows == 3 * H_B) & (off >= 0) & (off < 3)
    e_q = piece_row.astype(F32) + ones_q.astype(F32)
    e_k = ones_k.astype(F32) - piece_row_k.astype(F32)
    return e_q.astype(BF16), e_k.astype(BF16)


TOKEN_PITCH = 24


def _head_pitch(tm):
    return tm + 8 if (tm // 8) % 2 == 0 else tm


def _store_head_major(val, scr, out_ref):
    tm = val.shape[0]
    pitch = _head_pitch(tm)
    for h in range(H_B):
        pair = val[:, (h // 2) * LANES:(h // 2 + 1) * LANES]
        scr[h * pitch:h * pitch + tm, :] = pair if h % 2 == 0 else pltpu.roll(pair, D_HB, axis=1)

    for t in range(tm):
        for grp in range(H_B // 8):
            rows = scr[pl.ds(grp * 8 * pitch + t, 8, stride=pitch), :]
            out_ref[t, grp * 8:(grp + 1) * 8, :] = rows[:, :D_HB]


def _proj_kernel(*refs, aug):
    if aug:
        (x_ref, nkv_ref, nb_ref, wk_ref, wv_ref, wf_ref, bf_ref, wq_ref, wg_ref, eq_ref, ek_ref,
         k_ref, v_ref, lf_ref, g_ref, qa_ref, ka_ref, va_ref, carry) = refs
    else:
        (x_ref, nkv_ref, nb_ref, wk_ref, wv_ref, wf_ref, bf_ref, wq_ref, wg_ref,
         k_ref, v_ref, lf_ref, g_ref, q_ref, kf_ref, vf_ref, k_scr, v_scr) = refs
    tm = x_ref.shape[0]
    x = x_ref[...]
    xs = x * lax.rsqrt(jnp.mean(x * x, axis=-1, keepdims=True) + EPS)
    xk = (xs * nkv_ref[...]).astype(BF16)
    xq = (xs * nb_ref[...]).astype(BF16)
    k = _dot(xk, wk_ref[...])
    v = _dot(xk, wv_ref[...])
    z = _dot(xk, wf_ref[...]) + bf_ref[...]
    lf = jnp.minimum(z, 0.0) - jnp.log(1.0 + jnp.exp(-jnp.abs(z)))
    q = _dot(xq, wq_ref[...])
    g_ref[...] = _dot(xq, wg_ref[...])
    if not aug:
        _store_head_major(k, k_scr, k_ref)
        _store_head_major(v, v_scr, v_ref)
        lf_ref[...] = lf[:, :H_B]
        q_ref[...] = q
        kf_ref[...] = k
        vf_ref[...] = v
        return

    k_ref[0] = k.T.reshape(H_B, D_HB, tm)
    v_ref[0] = v.T.reshape(H_B, D_HB, tm)
    lf_ref[0] = lf.T[:H_B, :]

    @pl.when(pl.program_id(1) == 0)
    def _():
        carry[...] = jnp.zeros_like(carry)

    row = lax.broadcasted_iota(jnp.int32, (tm, LANES), 0)
    lane = lax.broadcasted_iota(jnp.int32, (tm, LANES), 1)
    f = lf
    s = 1
    while s < tm:
        f = f + jnp.where(row >= s, pltpu.roll(f, s, axis=0), 0.0)
        s *= 2
    f = f + carry[...]
    carry[...] = f[tm - 1:tm, :]
    h1, h2, h3 = _split3(f * LOG2E)
    pieces = jnp.where(lane < H_B, h1,
                       jnp.where(lane < 2 * H_B, pltpu.roll(h2, H_B, axis=1),
                                 jnp.where(lane < 3 * H_B, pltpu.roll(h3, 2 * H_B, axis=1),
                                           jnp.where(lane == 3 * H_B, 1.0, 0.0)))).astype(BF16)
    tail_q = _dot(pieces, eq_ref[...])
    tail_k = _dot(pieces, ek_ref[...])
    low = lane < D_HB
    one_col = [jnp.where(lane == AUG_BASE[par], 1.0, 0.0) for par in range(2)]
    qscale = (D_HB ** -0.5) * LOG2E
    for j in range(H_B // 2):
        sl = slice(j * LANES, (j + 1) * LANES)
        q2, k2, v2 = q[:, sl] * qscale, k[:, sl], v[:, sl]
        ev, od = 2 * j, 2 * j + 1
        ev_sl, od_sl = slice(ev * LANES, (ev + 1) * LANES), slice(od * LANES, (od + 1) * LANES)
        qa_ref[ev] = jnp.where(low, q2, tail_q[:, ev_sl]).astype(BF16)
        qa_ref[od] = jnp.where(low, tail_q[:, od_sl], q2).astype(BF16)
        ka_ref[ev] = jnp.where(low, k2, tail_k[:, ev_sl]).astype(BF16)
        ka_ref[od] = jnp.where(low, tail_k[:, od_sl], k2).astype(BF16)
        va_ref[ev] = jnp.where(low, v2, one_col[0]).astype(BF16)
        va_ref[od] = jnp.where(low, one_col[1], v2).astype(BF16)


def _proj(x2d, batch, seq, aug, norm_kv, norm_b, w_k, w_v, w_f, b_f, w_qq, w_qg):
    ntok, d = x2d.shape
    hd = w_k.shape[1]
    ins = [x2d, norm_kv, norm_b, w_k, w_v, w_f, b_f, w_qq, w_qg]
    if aug:
        tm = min(seq, 256)
        n_t = seq // tm
        grid = (batch, n_t)
        tok = lambda b, t: (b * n_t + t, 0)
        head_major = lambda b, t: (0, b * n_t + t, 0)
        time_minor = pl.BlockSpec((1, H_B, D_HB, tm), lambda b, t: (b, 0, 0, t))
        ins += list(_aug_selectors())
        out_specs = [time_minor, time_minor, pl.BlockSpec((1, H_B, tm), lambda b, t: (b, 0, t)),
                     pl.BlockSpec((tm, hd), tok)] + [pl.BlockSpec((H_B, tm, LANES), head_major)] * 3
        out_shape = ([jax.ShapeDtypeStruct((batch, H_B, D_HB, seq), F32)] * 2
                     + [jax.ShapeDtypeStruct((batch, H_B, seq), F32),
                        jax.ShapeDtypeStruct((ntok, hd), F32)]
                     + [jax.ShapeDtypeStruct((H_B, ntok, LANES), BF16)] * 3)
        scratch = [pltpu.VMEM((1, LANES), F32)]
    else:
        tm = min(ntok, 512)
        n_t = ntok // tm
        grid = (1, n_t)
        tok = lambda b, t: (t, 0)
        tok4 = lambda b, t: (t, 0, 0)
        out_specs = ([pl.BlockSpec((tm, H_B, D_HB), tok4)] * 2 + [pl.BlockSpec((tm, H_B), tok)]
                     + [pl.BlockSpec((tm, hd), tok)] * 4)
        out_shape = ([jax.ShapeDtypeStruct((ntok, H_B, D_HB), F32)] * 2
                     + [jax.ShapeDtypeStruct((ntok, H_B), F32)]
                     + [jax.ShapeDtypeStruct((ntok, hd), F32)] * 4)
        scratch = [pltpu.VMEM((H_B * _head_pitch(tm), LANES), F32)] * 2
    return pl.pallas_call(
        functools.partial(_proj_kernel, aug=aug),
        grid=grid,
        in_specs=[pl.BlockSpec((tm, d), tok)] + [_const_spec(a.shape) for a in ins[1:]],
        out_specs=out_specs,
        out_shape=out_shape,
        scratch_shapes=scratch,
        compiler_params=pltpu.CompilerParams(
            dimension_semantics=("arbitrary", "arbitrary"), vmem_limit_bytes=VMEM_LIMIT),
        name="kvq_proj",
    )(*ins)


def _cumsum_kernel(lf_ref, f_ref):
    n_blk = lf_ref.shape[2] // LANES
    upper = (lax.broadcasted_iota(jnp.int32, (LANES, LANES), 0)
             <= lax.broadcasted_iota(jnp.int32, (LANES, LANES), 1)).astype(BF16)
    run = jnp.zeros((lf_ref.shape[1], 1), F32)
    for c in range(n_blk):
        x = lf_ref[0, :, c * LANES:(c + 1) * LANES]
        h1 = x.astype(BF16)
        r1 = x - h1.astype(F32)
        h2 = r1.astype(BF16)
        h3 = (r1 - h2.astype(F32)).astype(BF16)
        cs = (_dot(h1, upper) + _dot(h2, upper)) + _dot(h3, upper) + run
        f_ref[0, :, c * LANES:(c + 1) * LANES] = cs
        run = cs[:, LANES - 1:LANES]


def _cumsum_time(lf_t):
    b, h, l = lf_t.shape
    return pl.pallas_call(
        _cumsum_kernel,
        grid=(b,),
        in_specs=[pl.BlockSpec((1, h, l), lambda i: (i, 0, 0))],
        out_specs=pl.BlockSpec((1, h, l), lambda i: (i, 0, 0)),
        out_shape=jax.ShapeDtypeStruct((b, h, l), F32),
        compiler_params=pltpu.CompilerParams(dimension_semantics=("arbitrary",)),
        name="logf_cumsum",
    )(lf_t)


def _head_column(f_tile, head):
    lane = lax.broadcasted_iota(jnp.int32, f_tile.shape, 1)
    return jnp.sum(jnp.where(lane == head, f_tile, 0.0), axis=-1, keepdims=True)


def _attn_prompt_kernel(qa_ref, ka_ref, va_ref, g_ref, o_ref, m_scr, acc_scr, *, tq):
    seq = qa_ref.shape[1]
    n_q = seq // tq
    half = tq // 2
    lane_q = lax.broadcasted_iota(jnp.int32, (tq, LANES), 1)

    def causal(rows, cols):
        return (lax.broadcasted_iota(jnp.int32, (rows, cols), 1)
                <= lax.broadcasted_iota(jnp.int32, (rows, cols), 0))

    items = []
    for qt in range(n_q):
        for kt in range(qt):
            items.append((qt, 0, tq, kt * tq, tq, None))
        items.append((qt, 0, tq, qt * tq, half, causal(tq, half)))
        items.append((qt, half, half, qt * tq + half, half, causal(half, half)))

    def scores(item):
        qt, r_lo, r_len, c0, c_len, _ = item
        return [_dot_nt(qa_ref[hf, qt * tq + r_lo:qt * tq + r_lo + r_len, :],
                        ka_ref[hf, c0:c0 + c_len, :]) for hf in range(2)]

    def update(item, s_pair):
        _, r_lo, r_len, c0, c_len, mask = item
        rows = slice(r_lo, r_lo + r_len)
        for hf in range(2):
            s2 = s_pair[hf]
            if mask is not None:
                s2 = jnp.where(mask, s2, -jnp.inf)
            cols = [s2[:, c * LANES:(c + 1) * LANES] for c in range(c_len // LANES)]
            m_old = m_scr[hf, rows, :]
            m_new = jnp.maximum(
                m_old, jnp.max(functools.reduce(jnp.maximum, cols), axis=-1, keepdims=True))
            p = jnp.concatenate([jnp.exp2(c - m_new).astype(BF16) for c in cols], axis=1)
            acc_scr[hf, rows, :] = (jnp.exp2(m_old - m_new) * acc_scr[hf, rows, :]
                                    + _dot(p, va_ref[hf, c0:c0 + c_len, :]))
            m_scr[hf, rows, :] = m_new

    def finish(qt):
        outs = []
        for hf in range(2):
            acc = acc_scr[hf]
            denom = jnp.sum(jnp.where(lane_q == AUG_BASE[hf], acc, 0.0), axis=-1, keepdims=True)
            outs.append(acc / denom)
        o = jnp.where(lane_q < D_HB, outs[0], outs[1])
        o = o * _sigmoid(g_ref[0, qt * tq:(qt + 1) * tq, :])
        o_ref[0, qt * tq:(qt + 1) * tq, :] = o.astype(BF16)

    s_next = scores(items[0])
    for i, item in enumerate(items):
        s_cur = s_next
        if i + 1 < len(items):
            s_next = scores(items[i + 1])
        if i == 0 or items[i - 1][0] != item[0]:
            m_scr[...] = jnp.full(m_scr.shape, -jnp.inf, F32)
            acc_scr[...] = jnp.zeros(acc_scr.shape, F32)
        update(item, s_cur)
        if i + 1 == len(items) or items[i + 1][0] != item[0]:
            finish(item[0])


def _attn_prompt(qa, ka, va, g):
    b, t, hd = g.shape
    tq = min(t, 512)
    n_pairs = hd // LANES
    heads = pl.BlockSpec((2, t, LANES), lambda i, j: (j, i, 0))
    col = pl.BlockSpec((1, t, LANES), lambda i, j: (i, 0, j))
    kern = functools.partial(_attn_prompt_kernel, tq=tq)
    return pl.pallas_call(
        kern,
        grid=(b, n_pairs),
        in_specs=[heads, heads, heads, col],
        out_specs=col,
        out_shape=jax.ShapeDtypeStruct((b, t, hd), BF16),
        scratch_shapes=[pltpu.VMEM((2, tq, LANES), F32), pltpu.VMEM((2, tq, LANES), F32)],
        compiler_params=pltpu.CompilerParams(
            dimension_semantics=("arbitrary", "arbitrary"), vmem_limit_bytes=VMEM_LIMIT),
        name="fox_attn_prompt",
    )(qa, ka, va, g)


def _pair_block(qs, k2, v2, fq, fk, mask, m_ref, l_ref, acc_ref, keys_on_lanes=False):
    tq = qs.shape[0] // 2
    tk = k2.shape[1] if keys_on_lanes else k2.shape[0]
    width = min(tk, LANES)
    s = _dot(qs, k2) if keys_on_lanes else _dot_nt(qs, k2)
    alphas, probs = [], []
    for hf in range(2):
        sh = s[hf * tq:(hf + 1) * tq] + (fq[hf] - fk[hf])
        if mask is not None:
            sh = jnp.where(mask, sh, -jnp.inf)
        cols = [sh[:, c * width:(c + 1) * width] for c in range(tk // width)]
        m_old = m_ref[hf]
        m_new = jnp.maximum(
            m_old, jnp.max(functools.reduce(jnp.maximum, cols), axis=-1, keepdims=True))
        p_cols = [jnp.exp(c - m_new[:, :width]) for c in cols]
        alpha = jnp.exp(m_old - m_new)
        l_ref[hf] = alpha * l_ref[hf] + jnp.sum(
            functools.reduce(lambda a, b: a + b, p_cols), axis=-1, keepdims=True)
        m_ref[hf] = m_new
        probs.append(jnp.concatenate([p.astype(BF16) for p in p_cols], axis=1))
        alphas.append(alpha)
    p_all = jnp.concatenate(probs, axis=0)
    pv = _dot_nt(p_all, v2) if keys_on_lanes else _dot(p_all, v2)
    acc_ref[...] = jnp.concatenate(alphas, axis=0) * acc_ref[...] + pv


def _split_heads(q2, lane_a):
    zero = jnp.zeros_like(q2)
    return jnp.concatenate([jnp.where(lane_a, q2, zero), jnp.where(lane_a, zero, q2)], axis=0)


def _attn_sample_kernel(q_ref, g_ref, kc_ref, vc_ref, kn_ref, vn_ref, fq_ref, fkc_ref, fkn_ref,
                        o_ref, m_scr, l_scr, acc_scr, *, n_pairs):
    kt = pl.program_id(1)
    n_kt = pl.num_programs(1)
    tq = q_ref.shape[1]
    scale = D_HB ** -0.5
    lane_a = lax.broadcasted_iota(jnp.int32, (tq, LANES), 1) < D_HB
    causal = (lax.broadcasted_iota(jnp.int32, (tq, tq), 1)
              <= lax.broadcasted_iota(jnp.int32, (tq, tq), 0))
    f_tile = fq_ref[0]

    @pl.when(kt == 0)
    def _():
        m_scr[...] = jnp.full(m_scr.shape, -jnp.inf, F32)
        l_scr[...] = jnp.zeros(l_scr.shape, F32)
        acc_scr[...] = jnp.zeros(acc_scr.shape, F32)

    def cache_pair(ref, hp):
        return ref[0, 2 * hp:2 * hp + 2].reshape(2 * D_HB, ref.shape[3]).astype(BF16)

    def pair_inputs(hp):
        cols = slice(hp * LANES, (hp + 1) * LANES)
        qs = _split_heads(q_ref[0, :, cols] * scale, lane_a).astype(BF16)
        fq = [_head_column(f_tile, 2 * hp + hf) for hf in range(2)]
        return cols, qs, fq

    for hp in range(n_pairs):
        cols, qs, fq = pair_inputs(hp)
        fk = [fkc_ref[0, 2 * hp + hf:2 * hp + hf + 1, :] for hf in range(2)]
        _pair_block(qs, cache_pair(kc_ref, hp), cache_pair(vc_ref, hp),
                    fq, fk, None, m_scr.at[hp], l_scr.at[hp], acc_scr.at[hp], keys_on_lanes=True)

    @pl.when(kt == n_kt - 1)
    def _():
        for hp in range(n_pairs):
            cols, qs, fq = pair_inputs(hp)
            fk = [fkn_ref[0, 2 * hp + hf:2 * hp + hf + 1, 0:tq] for hf in range(2)]
            _pair_block(qs, kn_ref[0, :, cols].astype(BF16), vn_ref[0, :, cols].astype(BF16),
                        fq, fk, causal, m_scr.at[hp], l_scr.at[hp], acc_scr.at[hp])
            acc = acc_scr[hp]
            o = jnp.where(lane_a, acc[:tq] / l_scr[hp, 0], acc[tq:] / l_scr[hp, 1])
            o = o * _sigmoid(g_ref[0, :, cols])
            o_ref[0, :, cols] = o.astype(BF16)


def _attn_sample(q, g, k_cache, v_cache, k_new, v_new, f, f_t, past):
    b, t, hd = q.shape
    n_pairs = hd // LANES
    tk = min(past, 1024)
    n_kt = past // tk
    row = pl.BlockSpec((1, t, hd), lambda i, j: (i, 0, 0))
    cache = pl.BlockSpec((1, H_B, D_HB, tk), lambda i, j: (i, 0, 0, j))
    kern = functools.partial(_attn_sample_kernel, n_pairs=n_pairs)
    return pl.pallas_call(
        kern,
        grid=(b, n_kt),
        in_specs=[row, row, cache, cache, row, row,
                  pl.BlockSpec((1, t, H_B), lambda i, j: (i, past // t, 0)),
                  pl.BlockSpec((1, H_B, tk), lambda i, j: (i, 0, j)),
                  pl.BlockSpec((1, H_B, LANES), lambda i, j: (i, 0, past // LANES))],
        out_specs=row,
        out_shape=jax.ShapeDtypeStruct((b, t, hd), BF16),
        scratch_shapes=[pltpu.VMEM((n_pairs, 2, t, LANES), F32), pltpu.VMEM((n_pairs, 2, t, LANES), F32),
                        pltpu.VMEM((n_pairs, 2 * t, LANES), F32)],
        compiler_params=pltpu.CompilerParams(
            dimension_semantics=("arbitrary", "arbitrary"), vmem_limit_bytes=VMEM_LIMIT),
        name="fox_attn_sample",
    )(q, g, k_cache, v_cache, k_new, v_new, f, f_t, f_t)


def _trunk(x, state0, k_past, v_past, logf_past, w):
    batch, seq, d = x.shape
    ntok = batch * seq
    hd = H_B * D_HB
    x2d = x.reshape(ntok, d)
    s0 = None if state0 is None else state0[:, 0]
    o_a, state = _hgrn(x2d, batch, seq, w["norm_a"], w["w_in_h"], w["lb_logits_h"], w["g_norm_a"], s0)
    x2d = _post(x2d, o_a, w["w_o_a"], w["norm_mlp0"], w["w_up0"], w["w_down0"])
    proj_w = (w["norm_kv"], w["norm_b"], w["w_k"], w["w_v"], w["w_f"], w["b_f"], w["w_qq"], w["w_qg"])
    if k_past is None:
        k_t, v_t, logf_t, g, qa, ka, va = _proj(x2d, batch, seq, True, *proj_w)
        k, v = jnp.transpose(k_t, (0, 3, 1, 2)), jnp.transpose(v_t, (0, 3, 1, 2))
        logf = jnp.swapaxes(logf_t, 1, 2)
        o_b = _attn_prompt(qa, ka, va, g.reshape(batch, seq, hd))
    else:
        k, v, logf, g, q, k_flat, v_flat = _proj(x2d, batch, seq, False, *proj_w)
        past = k_past.shape[1]
        total = past + seq
        padded = -(-total // LANES) * LANES
        lf_all = jnp.concatenate(
            [jnp.swapaxes(logf_past.astype(F32), 1, 2),
             jnp.swapaxes(logf.reshape(batch, seq, H_B), 1, 2),
             jnp.zeros((batch, H_B, padded - total), F32)], axis=2)
        f_t = _cumsum_time(lf_all)
        o_b = _attn_sample(q.reshape(batch, seq, hd), g.reshape(batch, seq, hd),
                           jnp.transpose(k_past, (0, 2, 3, 1)).astype(F32),
                           jnp.transpose(v_past, (0, 2, 3, 1)).astype(F32),
                           k_flat.reshape(batch, seq, hd), v_flat.reshape(batch, seq, hd),
                           jnp.swapaxes(f_t, 1, 2), f_t, past)
    y = _post(x2d, o_b.reshape(ntok, hd), w["w_o_b"], w["norm_mlp1"], w["w_up1"], w["w_down1"],
              w["norm_f"])
    return (y.reshape(batch, seq, d), state[:, None],
            k.reshape(batch, seq, H_B, D_HB), v.reshape(batch, seq, H_B, D_HB),
            logf.reshape(batch, seq, H_B))


def kernel(x_prompt, x_sample, state_hgrn, cache_k, cache_v, cache_logf, norm_a, w_in_a, lb_logits, g_norm_a, w_o_a, norm_kv, w_kv, b_f, norm_b, w_q_b, w_o_b, norm_mlp, w_up, w_down, norm_f):
    d = x_prompt.shape[-1]
    assert w_in_a.shape[0] == 1 and w_q_b.shape[0] == 1, "one HGRN2 layer and one FoX layer"
    n_heads = w_in_a.shape[2] // (4 * K_A)
    hd = H_B * D_HB
    row = lambda a: a.reshape(1, -1).astype(F32)
    w_f = jnp.zeros((d, LANES), F32).at[:, :H_B].set(w_kv[:, 2 * hd:])
    w = {
        "norm_a": row(norm_a[0]),
        "w_in_h": w_in_a[0].reshape(d, 4, n_heads, K_A).transpose(2, 0, 1, 3)
                           .reshape(n_heads, d, 4 * K_A).astype(BF16),
        "lb_logits_h": lb_logits.astype(F32).reshape(lb_logits.shape[0], n_heads, 1, K_A),
        "g_norm_a": row(g_norm_a[0]),
        "w_o_a": w_o_a[0].astype(BF16),
        "norm_mlp0": row(norm_mlp[0]), "w_up0": w_up[0].astype(BF16), "w_down0": w_down[0].astype(BF16),
        "norm_mlp1": row(norm_mlp[1]), "w_up1": w_up[1].astype(BF16), "w_down1": w_down[1].astype(BF16),
        "norm_kv": row(norm_kv), "norm_b": row(norm_b[0]),
        "w_k": w_kv[:, :hd].astype(BF16), "w_v": w_kv[:, hd:2 * hd].astype(BF16),
        "w_f": w_f.astype(BF16), "b_f": jnp.zeros((1, LANES), F32).at[0, :H_B].set(b_f.astype(F32)),
        "w_qq": w_q_b[0][:, :hd].astype(BF16), "w_qg": w_q_b[0][:, hd:].astype(BF16),
        "w_o_b": w_o_b[0].astype(BF16),
        "norm_f": row(norm_f),
    }
    y_p, st_p, k_p, v_p, lf_p = _trunk(x_prompt, None, None, None, None, w)
    y_s, st_s, k_s, v_s, lf_s = _trunk(x_sample, state_hgrn, cache_k, cache_v, cache_logf, w)
    return (y_p, y_s, st_p, k_p, v_p, lf_p, st_s, k_s, v_s, lf_s)
```

```python
import functools

import jax
import jax.numpy as jnp
from jax import lax
from jax.experimental import pallas as pl
from jax.experimental.pallas import tpu as pltpu

EPS = 1e-6
CHUNK = 64
K_A = 128
V_A = 128
H_B = 16
D_HB = 64
LANES = 128
VMEM_LIMIT = 56 * 1024 * 1024

F32 = jnp.float32
BF16 = jnp.bfloat16


def _dot(a, b):
    return jnp.dot(a, b, preferred_element_type=F32)


def _dot_nt(a, b):
    return lax.dot_general(a, b, (((1,), (1,)), ((), ())), preferred_element_type=F32)


def _dot_tn(a, b):
    return lax.dot_general(a, b, (((0,), (0,)), ((), ())), preferred_element_type=F32)


def _sigmoid(x):
    return 0.5 * jnp.tanh(0.5 * x) + 0.5


def _const_spec(shape):
    nd = len(shape)
    return pl.BlockSpec(shape, lambda *_: (0,) * nd, pipeline_mode=pl.Buffered(1))


def _hgrn_kernel(*refs, n_heads, n_chunks, carry):
    if carry:
        (x_ref, nrm_ref, win_ref, lbl_ref, gn_ref, o_ref, sout_ref, xn_scr, st_scr) = refs
        s0_ref = None
    else:
        (x_ref, nrm_ref, win_ref, lbl_ref, gn_ref, s0_ref, o_ref, sout_ref, xn_scr) = refs
        st_scr = None
    tm = n_chunks * CHUNK

    x = x_ref[...]
    inv = lax.rsqrt(jnp.mean(x * x, axis=-1, keepdims=True) + EPS)
    xn_scr[...] = (x * inv * nrm_ref[...]).astype(BF16)

    if carry:
        @pl.when(pl.program_id(1) == 0)
        def _():
            st_scr[...] = jnp.zeros_like(st_scr)

    row_in_chunk = lax.broadcasted_iota(jnp.int32, (tm, K_A), 0) % CHUNK
    tri = (lax.broadcasted_iota(jnp.int32, (CHUNK, CHUNK), 1)
           <= lax.broadcasted_iota(jnp.int32, (CHUNK, CHUNK), 0))

    def project(h):
        return _dot(xn_scr[...], win_ref[h])

    def gates(h, p):
        pq, pz = p[:, 0:K_A], p[:, K_A:2 * K_A]
        pi, pg = p[:, 2 * K_A:3 * K_A], p[:, 3 * K_A:4 * K_A]
        n_rows = lbl_ref.shape[0]
        lrows = [lbl_ref[r, h] for r in range(n_rows)]
        lmax = functools.reduce(jnp.maximum, lrows)
        lexp = [jnp.exp(l - lmax) for l in lrows]
        lb = lexp[0] / functools.reduce(lambda a, b: a + b, lexp)

        q = pq * _sigmoid(pq)
        f = lb + (1.0 - lb) * _sigmoid(pz)
        logf = jnp.log(f)
        k = 1.0 - f
        bc = logf
        s = 1
        while s < CHUNK:
            bc = bc + jnp.where(row_in_chunk >= s, pltpu.roll(bc, s, axis=0), 0.0)
            s *= 2
        chunked = lambda a: a.reshape(n_chunks, CHUNK, K_A)
        bc3 = chunked(bc)
        btot = bc3[:, CHUNK - 1:CHUNK, :]
        q_dec = chunked(q * jnp.exp(bc)).astype(BF16)
        k_inv = chunked(k * jnp.exp(-bc)).astype(BF16)
        k_end = (chunked(k) * jnp.exp(btot - bc3)).astype(BF16)
        decay = jnp.exp(btot)
        v = chunked(pi).astype(BF16)
        return q_dec, k_inv, k_end, decay, v, _sigmoid(pg)

    def recur(h, q_dec, k_inv, k_end, decay, v, gate):
        sc = jnp.einsum('cqk,csk->cqs', q_dec, k_inv, preferred_element_type=F32)
        sc = jnp.where(tri, sc, 0.0).astype(BF16)
        o_intra = jnp.einsum('cqs,csv->cqv', sc, v, preferred_element_type=F32)
        upd = jnp.einsum('csv,csk->cvk', v, k_end, preferred_element_type=F32)
        states = []
        if carry:
            st = st_scr[h]
            for c in range(n_chunks):
                states.append(st)
                st = st * decay[c] + upd[c]
            st_scr[h] = st
            sout_ref[0, h] = st.T
        else:
            for c in range(n_chunks):
                st = s0_ref[c, h].T
                states.append(st)
                sout_ref[c, h] = (st * decay[c] + upd[c]).T
        s_in = jnp.stack(states, axis=0).astype(BF16)
        o_inter = jnp.einsum('cqk,cvk->cqv', q_dec, s_in, preferred_element_type=F32)
        o = (o_intra + o_inter).reshape(tm, V_A)
        o = o * lax.rsqrt(jnp.mean(o * o, axis=-1, keepdims=True) + EPS) * gn_ref[...]
        o_ref[:, h * V_A:(h + 1) * V_A] = (o * gate).astype(BF16)

    p_next = project(0)
    staged = None
    for h in range(n_heads):
        p_cur = p_next
        if h + 1 < n_heads:
            p_next = project(h + 1)
        ready = gates(h, p_cur)
        if staged is not None:
            recur(h - 1, *staged)
        staged = ready
    recur(n_heads - 1, *staged)


def _hgrn(x2d, batch, seq, norm_w, w_in_h, lb_logits_h, g_norm, s0):
    ntok, d = x2d.shape
    n_heads = w_in_h.shape[0]
    carry = s0 is None
    if carry:
        tm = min(seq, 512)
        n_t = seq // tm
        grid = (batch, n_t)
        tok_map = lambda b, t: (b * n_t + t, 0)
        st_spec = pl.BlockSpec((1, n_heads, K_A, V_A), lambda b, t: (b, 0, 0, 0))
        scratch = [pltpu.VMEM((tm, d), BF16), pltpu.VMEM((n_heads, V_A, K_A), F32)]
        extra_in, extra_specs = [], []
    else:
        assert seq == CHUNK
        bt = min(batch, 8)
        tm = bt * CHUNK
        grid = (batch // bt, 1)
        tok_map = lambda b, t: (b, 0)
        st_spec = pl.BlockSpec((bt, n_heads, K_A, V_A), lambda b, t: (b, 0, 0, 0))
        scratch = [pltpu.VMEM((tm, d), BF16)]
        extra_in, extra_specs = [s0], [st_spec]
    n_chunks = tm // CHUNK
    kern = functools.partial(_hgrn_kernel, n_heads=n_heads, n_chunks=n_chunks, carry=carry)
    return pl.pallas_call(
        kern,
        grid=grid,
        in_specs=[pl.BlockSpec((tm, d), tok_map),
                  _const_spec(norm_w.shape), _const_spec(w_in_h.shape),
                  _const_spec(lb_logits_h.shape), _const_spec(g_norm.shape)] + extra_specs,
        out_specs=[pl.BlockSpec((tm, n_heads * V_A), tok_map), st_spec],
        out_shape=[jax.ShapeDtypeStruct((ntok, n_heads * V_A), BF16),
                   jax.ShapeDtypeStruct((batch, n_heads, K_A, V_A), F32)],
        scratch_shapes=scratch,
        compiler_params=pltpu.CompilerParams(
            dimension_semantics=("arbitrary", "arbitrary"), vmem_limit_bytes=VMEM_LIMIT),
        name="hgrn_mixer",
    )(x2d, norm_w, w_in_h, lb_logits_h, g_norm, *extra_in)


def _post_kernel(*refs, ff_block, final_norm):
    if final_norm:
        x_ref, o_ref, wo_ref, nm_ref, wup_ref, wdn_ref, nf_ref, y_ref = refs
    else:
        x_ref, o_ref, wo_ref, nm_ref, wup_ref, wdn_ref, y_ref = refs
    x1 = x_ref[...] + _dot(o_ref[...], wo_ref[...])
    inv = lax.rsqrt(jnp.mean(x1 * x1, axis=-1, keepdims=True) + EPS)
    xn = (x1 * inv * nm_ref[...]).astype(BF16)
    acc = x1
    d_ff = wup_ref.shape[1]
    for j in range(d_ff // ff_block):
        hcol = jnp.maximum(_dot(xn, wup_ref[:, j * ff_block:(j + 1) * ff_block]), 0.0)
        acc = acc + _dot((hcol * hcol).astype(BF16), wdn_ref[j * ff_block:(j + 1) * ff_block, :])
    if final_norm:
        inv = lax.rsqrt(jnp.mean(acc * acc, axis=-1, keepdims=True) + EPS)
        acc = acc * inv * nf_ref[...]
    y_ref[...] = acc


def _post(x2d, o2d, w_o, norm_mlp, w_up, w_down, norm_f=None):
    ntok, d = x2d.shape
    tm = min(ntok, 512)
    final_norm = norm_f is not None
    tok = lambda i: (i, 0)
    ins = [x2d, o2d, w_o, norm_mlp, w_up, w_down] + ([norm_f] if final_norm else [])
    specs = [pl.BlockSpec((tm, d), tok), pl.BlockSpec((tm, o2d.shape[1]), tok)]
    specs += [_const_spec(a.shape) for a in ins[2:]]
    kern = functools.partial(_post_kernel, ff_block=1024, final_norm=final_norm)
    return pl.pallas_call(
        kern,
        grid=(ntok // tm,),
        in_specs=specs,
        out_specs=pl.BlockSpec((tm, d), tok),
        out_shape=jax.ShapeDtypeStruct((ntok, d), F32),
        compiler_params=pltpu.CompilerParams(
            dimension_semantics=("arbitrary",), vmem_limit_bytes=VMEM_LIMIT),
        name="post_mlp",
    )(*ins)


LOG2E = 1.4426950408889634
AUG_BASE = (D_HB, 0)


def _split3(x):
    h1 = x.astype(BF16).astype(F32)
    r1 = x - h1
    h2 = r1.astype(BF16).astype(F32)
    h3 = (r1 - h2).astype(BF16).astype(F32)
    return h1, h2, h3


def _aug_selectors():
    rows = jnp.arange(LANES)[:, None]
    cols = jnp.arange(H_B * LANES)[None, :]
    head, lane = cols // LANES, cols % LANES
    off = lane - jnp.where(head % 2 == 0, AUG_BASE[0], AUG_BASE[1])
    piece_row = (rows == head + H_B * off) & (off >= 0) & (off < 3)
    piece_row_k = (rows == head + H_B * (off - 3)) & (off >= 3) & (off < 6)
    ones_q = (rows == 3 * H_B) & (off >= 3) & (off < 6)
    ones_k = (rows == 3 * H_B) & (off >= 0) & (off < 3)
    e_q = piece_row.astype(F32) + ones_q.astype(F32)
    e_k = ones_k.astype(F32) - piece_row_k.astype(F32)
    return e_q.astype(BF16), e_k.astype(BF16)


TOKEN_PITCH = 24


def _head_pitch(tm):
    return tm + 8 if (tm // 8) % 2 == 0 else tm


def _store_head_major(val, scr, out_ref):
    tm = val.shape[0]
    pitch = _head_pitch(tm)
    for h in range(H_B):
        pair = val[:, (h // 2) * LANES:(h // 2 + 1) * LANES]
        scr[h * pitch:h * pitch + tm, :] = pair if h % 2 == 0 else pltpu.roll(pair, D_HB, axis=1)

    for t in range(tm):
        for grp in range(H_B // 8):
            rows = scr[pl.ds(grp * 8 * pitch + t, 8, stride=pitch), :]
            out_ref[t, grp * 8:(grp + 1) * 8, :] = rows[:, :D_HB]


def _proj_kernel(*refs, aug):
    if aug:
        (x_ref, nkv_ref, nb_ref, wk_ref, wv_ref, wf_ref, bf_ref, wq_ref, wg_ref, eq_ref, ek_ref,
         k_ref, v_ref, lf_ref, g_ref, qa_ref, ka_ref, va_ref, carry) = refs
    else:
        (x_ref, nkv_ref, nb_ref, wk_ref, wv_ref, wf_ref, bf_ref, wq_ref, wg_ref,
         k_ref, v_ref, lf_ref, g_ref, q_ref, kf_ref, vf_ref, k_scr, v_scr) = refs
    tm = x_ref.shape[0]
    x = x_ref[...]
    xs = x * lax.rsqrt(jnp.mean(x * x, axis=-1, keepdims=True) + EPS)
    xk = (xs * nkv_ref[...]).astype(BF16)
    xq = (xs * nb_ref[...]).astype(BF16)
    k = _dot(xk, wk_ref[...])
    v = _dot(xk, wv_ref[...])
    z = _dot(xk, wf_ref[...]) + bf_ref[...]
    lf = jnp.minimum(z, 0.0) - jnp.log(1.0 + jnp.exp(-jnp.abs(z)))
    q = _dot(xq, wq_ref[...])
    g_ref[...] = _dot(xq, wg_ref[...])
    if not aug:
        _store_head_major(k, k_scr, k_ref)
        _store_head_major(v, v_scr, v_ref)
        lf_ref[...] = lf[:, :H_B]
        q_ref[...] = q
        kf_ref[...] = k
        vf_ref[...] = v
        return

    k_ref[0] = k.T.reshape(H_B, D_HB, tm)
    v_t = v.T
    v_ref[0] = v_t.reshape(H_B, D_HB, tm)
    row_t = lax.broadcasted_iota(jnp.int32, (LANES, tm), 0)
    for j in range(H_B // 2):
        v_pair = v_t[j * LANES:(j + 1) * LANES, :]
        va_ref[2 * j] = jnp.where(row_t < D_HB, v_pair,
                                  jnp.where(row_t == AUG_BASE[0], 1.0, 0.0)).astype(BF16)
        va_ref[2 * j + 1] = jnp.where(row_t < D_HB, jnp.where(row_t == AUG_BASE[1], 1.0, 0.0),
                                      v_pair).astype(BF16)
    lf_ref[0] = lf.T[:H_B, :]

    @pl.when(pl.program_id(1) == 0)
    def _():
        carry[...] = jnp.zeros_like(carry)

    row = lax.broadcasted_iota(jnp.int32, (tm, LANES), 0)
    lane = lax.broadcasted_iota(jnp.int32, (tm, LANES), 1)
    f = lf
    s = 1
    while s < tm:
        f = f + jnp.where(row >= s, pltpu.roll(f, s, axis=0), 0.0)
        s *= 2
    f = f + carry[...]
    carry[...] = f[tm - 1:tm, :]
    h1, h2, h3 = _split3(f * LOG2E)
    pieces = jnp.where(lane < H_B, h1,
                       jnp.where(lane < 2 * H_B, pltpu.roll(h2, H_B, axis=1),
                                 jnp.where(lane < 3 * H_B, pltpu.roll(h3, 2 * H_B, axis=1),
                                           jnp.where(lane == 3 * H_B, 1.0, 0.0)))).astype(BF16)
    tail_q = _dot(pieces, eq_ref[...])
    tail_k = _dot(pieces, ek_ref[...])
    low = lane < D_HB
    qscale = (D_HB ** -0.5) * LOG2E
    for j in range(H_B // 2):
        sl = slice(j * LANES, (j + 1) * LANES)
        q2, k2 = q[:, sl] * qscale, k[:, sl]
        ev, od = 2 * j, 2 * j + 1
        ev_sl, od_sl = slice(ev * LANES, (ev + 1) * LANES), slice(od * LANES, (od + 1) * LANES)
        qa_ref[ev] = jnp.where(low, q2, tail_q[:, ev_sl]).astype(BF16)
        qa_ref[od] = jnp.where(low, tail_q[:, od_sl], q2).astype(BF16)
        ka_ref[ev] = jnp.where(low, k2, tail_k[:, ev_sl]).astype(BF16)
        ka_ref[od] = jnp.where(low, tail_k[:, od_sl], k2).astype(BF16)


def _proj(x2d, batch, seq, aug, norm_kv, norm_b, w_k, w_v, w_f, b_f, w_qq, w_qg):
    ntok, d = x2d.shape
    hd = w_k.shape[1]
    ins = [x2d, norm_kv, norm_b, w_k, w_v, w_f, b_f, w_qq, w_qg]
    if aug:
        tm = min(seq, 256)
        n_t = seq // tm
        grid = (batch, n_t)
        tok = lambda b, t: (b * n_t + t, 0)
        head_major = lambda b, t: (0, b * n_t + t, 0)
        time_minor = pl.BlockSpec((1, H_B, D_HB, tm), lambda b, t: (b, 0, 0, t))
        ins += list(_aug_selectors())
        out_specs = [time_minor, time_minor, pl.BlockSpec((1, H_B, tm), lambda b, t: (b, 0, t)),
                     pl.BlockSpec((tm, hd), tok)] + [pl.BlockSpec((H_B, tm, LANES), head_major)] * 2
        out_specs += [pl.BlockSpec((H_B, LANES, tm), lambda b, t: (0, 0, b * n_t + t))]
        out_shape = ([jax.ShapeDtypeStruct((batch, H_B, D_HB, seq), F32)] * 2
                     + [jax.ShapeDtypeStruct((batch, H_B, seq), F32),
                        jax.ShapeDtypeStruct((ntok, hd), F32)]
                     + [jax.ShapeDtypeStruct((H_B, ntok, LANES), BF16)] * 2
                     + [jax.ShapeDtypeStruct((H_B, LANES, ntok), BF16)])
        scratch = [pltpu.VMEM((1, LANES), F32)]
    else:
        tm = min(ntok, 512)
        n_t = ntok // tm
        grid = (1, n_t)
        tok = lambda b, t: (t, 0)
        tok4 = lambda b, t: (t, 0, 0)
        out_specs = ([pl.BlockSpec((tm, H_B, D_HB), tok4)] * 2 + [pl.BlockSpec((tm, H_B), tok)]
                     + [pl.BlockSpec((tm, hd), tok)] * 4)
        out_shape = ([jax.ShapeDtypeStruct((ntok, H_B, D_HB), F32)] * 2
                     + [jax.ShapeDtypeStruct((ntok, H_B), F32)]
                     + [jax.ShapeDtypeStruct((ntok, hd), F32)] * 4)
        scratch = [pltpu.VMEM((H_B * _head_pitch(tm), LANES), F32)] * 2
    return pl.pallas_call(
        functools.partial(_proj_kernel, aug=aug),
        grid=grid,
        in_specs=[pl.BlockSpec((tm, d), tok)] + [_const_spec(a.shape) for a in ins[1:]],
        out_specs=out_specs,
        out_shape=out_shape,
        scratch_shapes=scratch,
        compiler_params=pltpu.CompilerParams(
            dimension_semantics=("arbitrary", "arbitrary"), vmem_limit_bytes=VMEM_LIMIT),
        name="kvq_proj",
    )(*ins)


def _cumsum_kernel(lf_ref, f_ref):
    n_blk = lf_ref.shape[2] // LANES
    upper = (lax.broadcasted_iota(jnp.int32, (LANES, LANES), 0)
             <= lax.broadcasted_iota(jnp.int32, (LANES, LANES), 1)).astype(BF16)
    run = jnp.zeros((lf_ref.shape[1], 1), F32)
    for c in range(n_blk):
        x = lf_ref[0, :, c * LANES:(c + 1) * LANES]
        h1 = x.astype(BF16)
        r1 = x - h1.astype(F32)
        h2 = r1.astype(BF16)
        h3 = (r1 - h2.astype(F32)).astype(BF16)
        cs = (_dot(h1, upper) + _dot(h2, upper)) + _dot(h3, upper) + run
        f_ref[0, :, c * LANES:(c + 1) * LANES] = cs
        run = cs[:, LANES - 1:LANES]


def _cumsum_time(lf_t):
    b, h, l = lf_t.shape
    return pl.pallas_call(
        _cumsum_kernel,
        grid=(b,),
        in_specs=[pl.BlockSpec((1, h, l), lambda i: (i, 0, 0))],
        out_specs=pl.BlockSpec((1, h, l), lambda i: (i, 0, 0)),
        out_shape=jax.ShapeDtypeStruct((b, h, l), F32),
        compiler_params=pltpu.CompilerParams(dimension_semantics=("arbitrary",)),
        name="logf_cumsum",
    )(lf_t)


def _head_column(f_tile, head):
    lane = lax.broadcasted_iota(jnp.int32, f_tile.shape, 1)
    return jnp.sum(jnp.where(lane == head, f_tile, 0.0), axis=-1, keepdims=True)


def _attn_prompt_kernel(qa_ref, ka_ref, vat_ref, g_ref, o_ref, m_scr, acc_scr, *, tq):
    seq = qa_ref.shape[1]
    n_q = seq // tq
    half = tq // 2
    row_o = lax.broadcasted_iota(jnp.int32, (LANES, tq), 0)

    def visible(keys, queries):
        return (lax.broadcasted_iota(jnp.int32, (keys, queries), 0)
                <= lax.broadcasted_iota(jnp.int32, (keys, queries), 1))

    items = []
    for qt in range(n_q):
        for kt in range(qt):
            items.append((qt, 0, tq, kt * tq, tq, None))
        items.append((qt, 0, tq, qt * tq, half, visible(half, tq)))
        items.append((qt, half, half, qt * tq + half, half, visible(half, half)))

    def scores(item):
        qt, q_lo, q_len, c0, c_len, _ = item
        return [_dot_nt(ka_ref[hf, c0:c0 + c_len, :],
                        qa_ref[hf, qt * tq + q_lo:qt * tq + q_lo + q_len, :]) for hf in range(2)]

    def update(item, s_pair):
        _, q_lo, q_len, c0, c_len, mask = item
        cols = slice(q_lo, q_lo + q_len)
        for hf in range(2):
            st = s_pair[hf]
            if mask is not None:
                st = jnp.where(mask, st, -jnp.inf)
            m_old = m_scr[hf, :, cols]
            m_new = jnp.maximum(m_old, jnp.max(st, axis=0, keepdims=True))
            pt = jnp.exp2(st - m_new).astype(BF16)
            acc_scr[hf, :, cols] = (jnp.exp2(m_old - m_new) * acc_scr[hf, :, cols]
                                    + _dot(vat_ref[hf, :, c0:c0 + c_len], pt))
            m_scr[hf, :, cols] = m_new

    def finish(qt):
        outs = []
        for hf in range(2):
            acc = acc_scr[hf]
            outs.append(acc / acc[AUG_BASE[hf]:AUG_BASE[hf] + 1, :])
        o = jnp.where(row_o < D_HB, outs[0], outs[1]).T
        o = o * _sigmoid(g_ref[0, qt * tq:(qt + 1) * tq, :])
        o_ref[0, qt * tq:(qt + 1) * tq, :] = o.astype(BF16)

    s_next = scores(items[0])
    for i, item in enumerate(items):
        s_cur = s_next
        if i + 1 < len(items):
            s_next = scores(items[i + 1])
        if i == 0 or items[i - 1][0] != item[0]:
            m_scr[...] = jnp.full(m_scr.shape, -jnp.inf, F32)
            acc_scr[...] = jnp.zeros(acc_scr.shape, F32)
        update(item, s_cur)
        if i + 1 == len(items) or items[i + 1][0] != item[0]:
            finish(item[0])


def _attn_prompt(qa, ka, vat, g):
    b, t, hd = g.shape
    tq = min(t, 512)
    n_pairs = hd // LANES
    heads = pl.BlockSpec((2, t, LANES), lambda i, j: (j, i, 0))
    heads_t = pl.BlockSpec((2, LANES, t), lambda i, j: (j, 0, i))
    col = pl.BlockSpec((1, t, LANES), lambda i, j: (i, 0, j))
    kern = functools.partial(_attn_prompt_kernel, tq=tq)
    return pl.pallas_call(
        kern,
        grid=(b, n_pairs),
        in_specs=[heads, heads, heads_t, col],
        out_specs=col,
        out_shape=jax.ShapeDtypeStruct((b, t, hd), BF16),
        scratch_shapes=[pltpu.VMEM((2, 1, tq), F32), pltpu.VMEM((2, LANES, tq), F32)],
        compiler_params=pltpu.CompilerParams(
            dimension_semantics=("arbitrary", "arbitrary"), vmem_limit_bytes=VMEM_LIMIT),
        name="fox_attn_prompt",
    )(qa, ka, vat, g)


def _pair_block(qs, k2, v2, fq, fk, mask, m_ref, l_ref, acc_ref, keys_on_lanes=False):
    tq = qs.shape[0] // 2
    tk = k2.shape[1] if keys_on_lanes else k2.shape[0]
    width = min(tk, LANES)
    s = _dot(qs, k2) if keys_on_lanes else _dot_nt(qs, k2)
    alphas, probs = [], []
    for hf in range(2):
        sh = s[hf * tq:(hf + 1) * tq] + (fq[hf] - fk[hf])
        if mask is not None:
            sh = jnp.where(mask, sh, -jnp.inf)
        cols = [sh[:, c * width:(c + 1) * width] for c in range(tk // width)]
        m_old = m_ref[hf]
        m_new = jnp.maximum(
            m_old, jnp.max(functools.reduce(jnp.maximum, cols), axis=-1, keepdims=True))
        p_cols = [jnp.exp(c - m_new[:, :width]) for c in cols]
        alpha = jnp.exp(m_old - m_new)
        l_ref[hf] = alpha * l_ref[hf] + jnp.sum(
            functools.reduce(lambda a, b: a + b, p_cols), axis=-1, keepdims=True)
        m_ref[hf] = m_new
        probs.append(jnp.concatenate([p.astype(BF16) for p in p_cols], axis=1))
        alphas.append(alpha)
    p_all = jnp.concatenate(probs, axis=0)
    pv = _dot_nt(p_all, v2) if keys_on_lanes else _dot(p_all, v2)
    acc_ref[...] = jnp.concatenate(alphas, axis=0) * acc_ref[...] + pv


def _split_heads(q2, lane_a):
    zero = jnp.zeros_like(q2)
    return jnp.concatenate([jnp.where(lane_a, q2, zero), jnp.where(lane_a, zero, q2)], axis=0)


def _attn_sample_kernel(q_ref, g_ref, kc_ref, vc_ref, kn_ref, vn_ref, fq_ref, fkc_ref, fkn_ref,
                        o_ref, m_scr, l_scr, acc_scr, *, n_pairs):
    kt = pl.program_id(1)
    n_kt = pl.num_programs(1)
    tq = q_ref.shape[1]
    scale = D_HB ** -0.5
    lane_a = lax.broadcasted_iota(jnp.int32, (tq, LANES), 1) < D_HB
    causal = (lax.broadcasted_iota(jnp.int32, (tq, tq), 1)
              <= lax.broadcasted_iota(jnp.int32, (tq, tq), 0))
    f_tile = fq_ref[0]

    @pl.when(kt == 0)
    def _():
        m_scr[...] = jnp.full(m_scr.shape, -jnp.inf, F32)
        l_scr[...] = jnp.zeros(l_scr.shape, F32)
        acc_scr[...] = jnp.zeros(acc_scr.shape, F32)

    def cache_pair(ref, hp):
        return ref[0, 2 * hp:2 * hp + 2].reshape(2 * D_HB, ref.shape[3]).astype(BF16)

    def pair_inputs(hp):
        cols = slice(hp * LANES, (hp + 1) * LANES)
        qs = _split_heads(q_ref[0, :, cols] * scale, lane_a).astype(BF16)
        fq = [_head_column(f_tile, 2 * hp + hf) for hf in range(2)]
        return cols, qs, fq

    for hp in range(n_pairs):
        cols, qs, fq = pair_inputs(hp)
        fk = [fkc_ref[0, 2 * hp + hf:2 * hp + hf + 1, :] for hf in range(2)]
        _pair_block(qs, cache_pair(kc_ref, hp), cache_pair(vc_ref, hp),
                    fq, fk, None, m_scr.at[hp], l_scr.at[hp], acc_scr.at[hp], keys_on_lanes=True)

    @pl.when(kt == n_kt - 1)
    def _():
        for hp in range(n_pairs):
            cols, qs, fq = pair_inputs(hp)
            fk = [fkn_ref[0, 2 * hp + hf:2 * hp + hf + 1, 0:tq] for hf in range(2)]
            _pair_block(qs, kn_ref[0, :, cols].astype(BF16), vn_ref[0, :, cols].astype(BF16),
                        fq, fk, causal, m_scr.at[hp], l_scr.at[hp], acc_scr.at[hp])
            acc = acc_scr[hp]
            o = jnp.where(lane_a, acc[:tq] / l_scr[hp, 0], acc[tq:] / l_scr[hp, 1])
            o = o * _sigmoid(g_ref[0, :, cols])
            o_ref[0, :, cols] = o.astype(BF16)


def _attn_sample(q, g, k_cache, v_cache, k_new, v_new, f, f_t, past):
    b, t, hd = q.shape
    n_pairs = hd // LANES
    tk = min(past, 1024)
    n_kt = past // tk
    row = pl.BlockSpec((1, t, hd), lambda i, j: (i, 0, 0))
    cache = pl.BlockSpec((1, H_B, D_HB, tk), lambda i, j: (i, 0, 0, j))
    kern = functools.partial(_attn_sample_kernel, n_pairs=n_pairs)
    return pl.pallas_call(
        kern,
        grid=(b, n_kt),
        in_specs=[row, row, cache, cache, row, row,
                  pl.BlockSpec((1, t, H_B), lambda i, j: (i, past // t, 0)),
                  pl.BlockSpec((1, H_B, tk), lambda i, j: (i, 0, j)),
                  pl.BlockSpec((1, H_B, LANES), lambda i, j: (i, 0, past // LANES))],
        out_specs=row,
        out_shape=jax.ShapeDtypeStruct((b, t, hd), BF16),
        scratch_shapes=[pltpu.VMEM((n_pairs, 2, t, LANES), F32), pltpu.VMEM((n_pairs, 2, t, LANES), F32),
                        pltpu.VMEM((n_pairs, 2 * t, LANES), F32)],
        compiler_params=pltpu.CompilerParams(
            dimension_semantics=("arbitrary", "arbitrary"), vmem_limit_bytes=VMEM_LIMIT),
        name="fox_attn_sample",
    )(q, g, k_cache, v_cache, k_new, v_new, f, f_t, f_t)


def _trunk(x, state0, k_past, v_past, logf_past, w):
    batch, seq, d = x.shape
    ntok = batch * seq
    hd = H_B * D_HB
    x2d = x.reshape(ntok, d)
    s0 = None if state0 is None else state0[:, 0]
    o_a, state = _hgrn(x2d, batch, seq, w["norm_a"], w["w_in_h"], w["lb_logits_h"], w["g_norm_a"], s0)
    x2d = _post(x2d, o_a, w["w_o_a"], w["norm_mlp0"], w["w_up0"], w["w_down0"])
    proj_w = (w["norm_kv"], w["norm_b"], w["w_k"], w["w_v"], w["w_f"], w["b_f"], w["w_qq"], w["w_qg"])
    if k_past is None:
        k_t, v_t, logf_t, g, qa, ka, va = _proj(x2d, batch, seq, True, *proj_w)
        k, v = jnp.transpose(k_t, (0, 3, 1, 2)), jnp.transpose(v_t, (0, 3, 1, 2))
        logf = jnp.swapaxes(logf_t, 1, 2)
        o_b = _attn_prompt(qa, ka, va, g.reshape(batch, seq, hd))
    else:
        k, v, logf, g, q, k_flat, v_flat = _proj(x2d, batch, seq, False, *proj_w)
        past = k_past.shape[1]
        total = past + seq
        padded = -(-total // LANES) * LANES
        lf_all = jnp.concatenate(
            [jnp.swapaxes(logf_past.astype(F32), 1, 2),
             jnp.swapaxes(logf.reshape(batch, seq, H_B), 1, 2),
             jnp.zeros((batch, H_B, padded - total), F32)], axis=2)
        f_t = _cumsum_time(lf_all)
        o_b = _attn_sample(q.reshape(batch, seq, hd), g.reshape(batch, seq, hd),
                           jnp.transpose(k_past, (0, 2, 3, 1)).astype(F32),
                           jnp.transpose(v_past, (0, 2, 3, 1)).astype(F32),
                           k_flat.reshape(batch, seq, hd), v_flat.reshape(batch, seq, hd),
                           jnp.swapaxes(f_t, 1, 2), f_t, past)
    y = _post(x2d, o_b.reshape(ntok, hd), w["w_o_b"], w["norm_mlp1"], w["w_up1"], w["w_down1"],
              w["norm_f"])
    return (y.reshape(batch, seq, d), state[:, None],
            k.reshape(batch, seq, H_B, D_HB), v.reshape(batch, seq, H_B, D_HB),
            logf.reshape(batch, seq, H_B))


def kernel(x_prompt, x_sample, state_hgrn, cache_k, cache_v, cache_logf, norm_a, w_in_a, lb_logits, g_norm_a, w_o_a, norm_kv, w_kv, b_f, norm_b, w_q_b, w_o_b, norm_mlp, w_up, w_down, norm_f):
    d = x_prompt.shape[-1]
    assert w_in_a.shape[0] == 1 and w_q_b.shape[0] == 1, "one HGRN2 layer and one FoX layer"
    n_heads = w_in_a.shape[2] // (4 * K_A)
    hd = H_B * D_HB
    row = lambda a: a.reshape(1, -1).astype(F32)
    w_f = jnp.zeros((d, LANES), F32).at[:, :H_B].set(w_kv[:, 2 * hd:])
    w = {
        "norm_a": row(norm_a[0]),
        "w_in_h": w_in_a[0].reshape(d, 4, n_heads, K_A).transpose(2, 0, 1, 3)
                           .reshape(n_heads, d, 4 * K_A).astype(BF16),
        "lb_logits_h": lb_logits.astype(F32).reshape(lb_logits.shape[0], n_heads, 1, K_A),
        "g_norm_a": row(g_norm_a[0]),
        "w_o_a": w_o_a[0].astype(BF16),
        "norm_mlp0": row(norm_mlp[0]), "w_up0": w_up[0].astype(BF16), "w_down0": w_down[0].astype(BF16),
        "norm_mlp1": row(norm_mlp[1]), "w_up1": w_up[1].astype(BF16), "w_down1": w_down[1].astype(BF16),
        "norm_kv": row(norm_kv), "norm_b": row(norm_b[0]),
        "w_k": w_kv[:, :hd].astype(BF16), "w_v": w_kv[:, hd:2 * hd].astype(BF16),
        "w_f": w_f.astype(BF16), "b_f": jnp.zeros((1, LANES), F32).at[0, :H_B].set(b_f.astype(F32)),
        "w_qq": w_q_b[0][:, :hd].astype(BF16), "w_qg": w_q_b[0][:, hd:].astype(BF16),
        "w_o_b": w_o_b[0].astype(BF16),
        "norm_f": row(norm_f),
    }
    y_p, st_p, k_p, v_p, lf_p = _trunk(x_prompt, None, None, None, None, w)
    y_s, st_s, k_s, v_s, lf_s = _trunk(x_sample, state_hgrn, cache_k, cache_v, cache_logf, w)
    return (y_p, y_s, st_p, k_p, v_p, lf_p, st_s, k_s, v_s, lf_s)
```

```python
import functools

import jax
import jax.numpy as jnp
from jax import lax
from jax.experimental import pallas as pl
from jax.experimental.pallas import tpu as pltpu

EPS = 1e-6
CHUNK = 64
K_A = 128
V_A = 128
H_B = 16
D_HB = 64
LANES = 128
VMEM_LIMIT = 56 * 1024 * 1024

F32 = jnp.float32
BF16 = jnp.bfloat16


def _dot(a, b):
    return jnp.dot(a, b, preferred_element_type=F32)


def _dot_nt(a, b):
    return lax.dot_general(a, b, (((1,), (1,)), ((), ())), preferred_element_type=F32)


def _dot_tn(a, b):
    return lax.dot_general(a, b, (((0,), (0,)), ((), ())), preferred_element_type=F32)


def _sigmoid(x):
    return 0.5 * jnp.tanh(0.5 * x) + 0.5


def _const_spec(shape):
    nd = len(shape)
    return pl.BlockSpec(shape, lambda *_: (0,) * nd, pipeline_mode=pl.Buffered(1))


def _hgrn_kernel(*refs, n_heads, n_chunks, carry):
    if carry:
        (x_ref, nrm_ref, win_ref, lbl_ref, gn_ref, o_ref, sout_ref, xn_scr, st_scr) = refs
        s0_ref = None
    else:
        (x_ref, nrm_ref, win_ref, lbl_ref, gn_ref, s0_ref, o_ref, sout_ref, xn_scr) = refs
        st_scr = None
    tm = n_chunks * CHUNK

    x = x_ref[...]
    inv = lax.rsqrt(jnp.mean(x * x, axis=-1, keepdims=True) + EPS)
    xn_scr[...] = (x * inv * nrm_ref[...]).astype(BF16)

    if carry:
        @pl.when(pl.program_id(1) == 0)
        def _():
            st_scr[...] = jnp.zeros_like(st_scr)

    row_in_chunk = lax.broadcasted_iota(jnp.int32, (tm, K_A), 0) % CHUNK
    tri = (lax.broadcasted_iota(jnp.int32, (CHUNK, CHUNK), 1)
           <= lax.broadcasted_iota(jnp.int32, (CHUNK, CHUNK), 0))

    def project(h):
        return _dot(xn_scr[...], win_ref[h])

    def gates(h, p):
        pq, pz = p[:, 0:K_A], p[:, K_A:2 * K_A]
        pi, pg = p[:, 2 * K_A:3 * K_A], p[:, 3 * K_A:4 * K_A]
        n_rows = lbl_ref.shape[0]
        lrows = [lbl_ref[r, h] for r in range(n_rows)]
        lmax = functools.reduce(jnp.maximum, lrows)
        lexp = [jnp.exp(l - lmax) for l in lrows]
        lb = lexp[0] / functools.reduce(lambda a, b: a + b, lexp)

        q = pq * _sigmoid(pq)
        f = lb + (1.0 - lb) * _sigmoid(pz)
        logf = jnp.log(f)
        k = 1.0 - f
        bc = logf
        s = 1
        while s < CHUNK:
            bc = bc + jnp.where(row_in_chunk >= s, pltpu.roll(bc, s, axis=0), 0.0)
            s *= 2
        chunked = lambda a: a.reshape(n_chunks, CHUNK, K_A)
        bc3 = chunked(bc)
        btot = bc3[:, CHUNK - 1:CHUNK, :]
        q_dec = chunked(q * jnp.exp(bc)).astype(BF16)
        k_inv = chunked(k * jnp.exp(-bc)).astype(BF16)
        k_end = (chunked(k) * jnp.exp(btot - bc3)).astype(BF16)
        decay = jnp.exp(btot)
        v = chunked(pi).astype(BF16)
        return q_dec, k_inv, k_end, decay, v, _sigmoid(pg)

    def recur(h, q_dec, k_inv, k_end, decay, v, gate):
        sc = jnp.einsum('cqk,csk->cqs', q_dec, k_inv, preferred_element_type=F32)
        sc = jnp.where(tri, sc, 0.0).astype(BF16)
        o_intra = jnp.einsum('cqs,csv->cqv', sc, v, preferred_element_type=F32)
        upd = jnp.einsum('csv,csk->cvk', v, k_end, preferred_element_type=F32)
        states = []
        if carry:
            st = st_scr[h]
            for c in range(n_chunks):
                states.append(st)
                st = st * decay[c] + upd[c]
            st_scr[h] = st
            sout_ref[0, h] = st.T
        else:
            for c in range(n_chunks):
                st = s0_ref[c, h].T
                states.append(st)
                sout_ref[c, h] = (st * decay[c] + upd[c]).T
        s_in = jnp.stack(states, axis=0).astype(BF16)
        o_inter = jnp.einsum('cqk,cvk->cqv', q_dec, s_in, preferred_element_type=F32)
        o = (o_intra + o_inter).reshape(tm, V_A)
        o = o * lax.rsqrt(jnp.mean(o * o, axis=-1, keepdims=True) + EPS) * gn_ref[...]
        o_ref[:, h * V_A:(h + 1) * V_A] = (o * gate).astype(BF16)

    p_next = project(0)
    staged = None
    for h in range(n_heads):
        p_cur = p_next
        if h + 1 < n_heads:
            p_next = project(h + 1)
        ready = gates(h, p_cur)
        if staged is not None:
            recur(h - 1, *staged)
        staged = ready
    recur(n_heads - 1, *staged)


def _hgrn(x2d, batch, seq, norm_w, w_in_h, lb_logits_h, g_norm, s0):
    ntok, d = x2d.shape
    n_heads = w_in_h.shape[0]
    carry = s0 is None
    if carry:
        tm = min(seq, 512)
        n_t = seq // tm
        grid = (batch, n_t)
        tok_map = lambda b, t: (b * n_t + t, 0)
        st_spec = pl.BlockSpec((1, n_heads, K_A, V_A), lambda b, t: (b, 0, 0, 0))
        scratch = [pltpu.VMEM((tm, d), BF16), pltpu.VMEM((n_heads, V_A, K_A), F32)]
        extra_in, extra_specs = [], []
    else:
        assert seq == CHUNK
        bt = min(batch, 8)
        tm = bt * CHUNK
        grid = (batch // bt, 1)
        tok_map = lambda b, t: (b, 0)
        st_spec = pl.BlockSpec((bt, n_heads, K_A, V_A), lambda b, t: (b, 0, 0, 0))
        scratch = [pltpu.VMEM((tm, d), BF16)]
        extra_in, extra_specs = [s0], [st_spec]
    n_chunks = tm // CHUNK
    kern = functools.partial(_hgrn_kernel, n_heads=n_heads, n_chunks=n_chunks, carry=carry)
    return pl.pallas_call(
        kern,
        grid=grid,
        in_specs=[pl.BlockSpec((tm, d), tok_map),
                  _const_spec(norm_w.shape), _const_spec(w_in_h.shape),
                  _const_spec(lb_logits_h.shape), _const_spec(g_norm.shape)] + extra_specs,
        out_specs=[pl.BlockSpec((tm, n_heads * V_A), tok_map), st_spec],
        out_shape=[jax.ShapeDtypeStruct((ntok, n_heads * V_A), BF16),
                   jax.ShapeDtypeStruct((batch, n_heads, K_A, V_A), F32)],
        scratch_shapes=scratch,
        compiler_params=pltpu.CompilerParams(
            dimension_semantics=("arbitrary", "arbitrary"), vmem_limit_bytes=VMEM_LIMIT),
        name="hgrn_mixer",
    )(x2d, norm_w, w_in_h, lb_logits_h, g_norm, *extra_in)


def _post_kernel(*refs, ff_block, final_norm):
    if final_norm:
        x_ref, o_ref, wo_ref, nm_ref, wup_ref, wdn_ref, nf_ref, y_ref = refs
    else:
        x_ref, o_ref, wo_ref, nm_ref, wup_ref, wdn_ref, y_ref = refs
    x1 = x_ref[...] + _dot(o_ref[...], wo_ref[...])
    inv = lax.rsqrt(jnp.mean(x1 * x1, axis=-1, keepdims=True) + EPS)
    xn = (x1 * inv * nm_ref[...]).astype(BF16)
    acc = x1
    d_ff = wup_ref.shape[1]
    for j in range(d_ff // ff_block):
        hcol = jnp.maximum(_dot(xn, wup_ref[:, j * ff_block:(j + 1) * ff_block]), 0.0)
        acc = acc + _dot((hcol * hcol).astype(BF16), wdn_ref[j * ff_block:(j + 1) * ff_block, :])
    if final_norm:
        inv = lax.rsqrt(jnp.mean(acc * acc, axis=-1, keepdims=True) + EPS)
        acc = acc * inv * nf_ref[...]
    y_ref[...] = acc


def _post(x2d, o2d, w_o, norm_mlp, w_up, w_down, norm_f=None):
    ntok, d = x2d.shape
    tm = min(ntok, 512)
    final_norm = norm_f is not None
    tok = lambda i: (i, 0)
    ins = [x2d, o2d, w_o, norm_mlp, w_up, w_down] + ([norm_f] if final_norm else [])
    specs = [pl.BlockSpec((tm, d), tok), pl.BlockSpec((tm, o2d.shape[1]), tok)]
    specs += [_const_spec(a.shape) for a in ins[2:]]
    kern = functools.partial(_post_kernel, ff_block=1024, final_norm=final_norm)
    return pl.pallas_call(
        kern,
        grid=(ntok // tm,),
        in_specs=specs,
        out_specs=pl.BlockSpec((tm, d), tok),
        out_shape=jax.ShapeDtypeStruct((ntok, d), F32),
        compiler_params=pltpu.CompilerParams(
            dimension_semantics=("arbitrary",), vmem_limit_bytes=VMEM_LIMIT),
        name="post_mlp",
    )(*ins)


LOG2E = 1.4426950408889634
AUG_BASE = (D_HB, 0)


def _split3(x):
    h1 = x.astype(BF16).astype(F32)
    r1 = x - h1
    h2 = r1.astype(BF16).astype(F32)
    h3 = (r1 - h2).astype(BF16).astype(F32)
    return h1, h2, h3


def _head_pitch(tm):
    return tm + 8 if (tm // 8) % 2 == 0 else tm


def _store_head_major(val, scr, out_ref):
    tm = val.shape[0]
    pitch = _head_pitch(tm)
    for h in range(H_B):
        pair = val[:, (h // 2) * LANES:(h // 2 + 1) * LANES]
        scr[h * pitch:h * pitch + tm, :] = pair if h % 2 == 0 else pltpu.roll(pair, D_HB, axis=1)

    for t in range(tm):
        for grp in range(H_B // 8):
            rows = scr[pl.ds(grp * 8 * pitch + t, 8, stride=pitch), :]
            out_ref[t, grp * 8:(grp + 1) * 8, :] = rows[:, :D_HB]


def _proj_kernel(*refs, aug):
    if aug:
        (x_ref, nkv_ref, nb_ref, wk_ref, wv_ref, wf_ref, bf_ref, wq_ref, wg_ref,
         k_ref, v_ref, lf_ref, g_ref, qa_ref, ka_ref, va_ref, carry) = refs
    else:
        (x_ref, nkv_ref, nb_ref, wk_ref, wv_ref, wf_ref, bf_ref, wq_ref, wg_ref,
         k_ref, v_ref, lf_ref, g_ref, q_ref, kf_ref, vf_ref, k_scr, v_scr) = refs
    tm = x_ref.shape[0]
    x = x_ref[...]
    xs = x * lax.rsqrt(jnp.mean(x * x, axis=-1, keepdims=True) + EPS)
    xk = (xs * nkv_ref[...]).astype(BF16)
    xq = (xs * nb_ref[...]).astype(BF16)
    z = _dot(xk, wf_ref[...]) + bf_ref[...]
    lf = jnp.minimum(z, 0.0) - jnp.log(1.0 + jnp.exp(-jnp.abs(z)))
    if not aug:
        k = _dot(xk, wk_ref[...])
        v = _dot(xk, wv_ref[...])
        g_ref[...] = _dot(xq, wg_ref[...])
        _store_head_major(k, k_scr, k_ref)
        _store_head_major(v, v_scr, v_ref)
        lf_ref[...] = lf[:, :H_B]
        q_ref[...] = _dot(xq, wq_ref[...])
        kf_ref[...] = k
        vf_ref[...] = v
        return

    lf_ref[0] = lf.T[:H_B, :]

    @pl.when(pl.program_id(1) == 0)
    def _():
        carry[...] = jnp.zeros_like(carry)

    row = lax.broadcasted_iota(jnp.int32, (tm, LANES), 0)
    lane = lax.broadcasted_iota(jnp.int32, (tm, LANES), 1)
    f = lf
    s = 1
    while s < tm:
        f = f + jnp.where(row >= s, pltpu.roll(f, s, axis=0), 0.0)
        s *= 2
    f = f + carry[...]
    carry[...] = f[tm - 1:tm, :]
    h1, h2, h3 = _split3(f * LOG2E)
    pieces = jnp.where(lane < H_B, h1,
                       jnp.where(lane < 2 * H_B, pltpu.roll(h2, H_B, axis=1),
                                 jnp.where(lane < 3 * H_B, pltpu.roll(h3, 2 * H_B, axis=1), 0.0)))
    both = pieces + pltpu.roll(pieces, D_HB, axis=1)
    shifted = pltpu.roll(both, 1, axis=1)
    neg_both = -both
    low = lane < D_HB

    def tails(h):
        base = AUG_BASE[h % 2]
        at = (lane == base + h) | (lane == base + H_B + h) | (lane == base + 2 * H_B + h)
        up = (lane == base + h + 1) | (lane == base + H_B + h + 1) | (lane == base + 2 * H_B + h + 1)
        q_tail = jnp.where(up, shifted, jnp.where(at, 1.0, 0.0))
        k_tail = jnp.where(at, neg_both, jnp.where(up, 1.0, 0.0))
        return q_tail, k_tail

    def put(ref, h, data, tail):
        ref[h] = (jnp.where(low, data, tail) if h % 2 == 0 else jnp.where(low, tail, data)).astype(BF16)

    k = _dot(xk, wk_ref[...])
    k_ref[0] = k.T.reshape(H_B, D_HB, tm)
    for h in range(H_B):
        put(ka_ref, h, k[:, (h // 2) * LANES:(h // 2 + 1) * LANES], tails(h)[1])

    v_t = _dot(xk, wv_ref[...]).T
    v_ref[0] = v_t.reshape(H_B, D_HB, tm)
    row_t = lax.broadcasted_iota(jnp.int32, (LANES, tm), 0)
    for j in range(H_B // 2):
        v_pair = v_t[j * LANES:(j + 1) * LANES, :]
        va_ref[2 * j] = jnp.where(row_t < D_HB, v_pair,
                                  jnp.where(row_t == AUG_BASE[0], 1.0, 0.0)).astype(BF16)
        va_ref[2 * j + 1] = jnp.where(row_t < D_HB, jnp.where(row_t == AUG_BASE[1], 1.0, 0.0),
                                      v_pair).astype(BF16)

    q = _dot(xq, wq_ref[...]) * ((D_HB ** -0.5) * LOG2E)
    for h in range(H_B):
        put(qa_ref, h, q[:, (h // 2) * LANES:(h // 2 + 1) * LANES], tails(h)[0])

    g_ref[...] = _dot(xq, wg_ref[...])


def _proj(x2d, batch, seq, aug, norm_kv, norm_b, w_k, w_v, w_f, b_f, w_qq, w_qg):
    ntok, d = x2d.shape
    hd = w_k.shape[1]
    ins = [x2d, norm_kv, norm_b, w_k, w_v, w_f, b_f, w_qq, w_qg]
    if aug:
        tm = min(seq, 256)
        n_t = seq // tm
        grid = (batch, n_t)
        tok = lambda b, t: (b * n_t + t, 0)
        head_major = lambda b, t: (0, b * n_t + t, 0)
        time_minor = pl.BlockSpec((1, H_B, D_HB, tm), lambda b, t: (b, 0, 0, t))
        out_specs = [time_minor, time_minor, pl.BlockSpec((1, H_B, tm), lambda b, t: (b, 0, t)),
                     pl.BlockSpec((tm, hd), tok),
                     pl.BlockSpec((H_B, tm, LANES), head_major),
                     pl.BlockSpec((H_B, tm, LANES), head_major),
                     pl.BlockSpec((H_B, LANES, tm), lambda b, t: (0, 0, b * n_t + t))]
        out_shape = ([jax.ShapeDtypeStruct((batch, H_B, D_HB, seq), F32)] * 2
                     + [jax.ShapeDtypeStruct((batch, H_B, seq), F32),
                        jax.ShapeDtypeStruct((ntok, hd), F32),
                        jax.ShapeDtypeStruct((H_B, ntok, LANES), BF16),
                        jax.ShapeDtypeStruct((H_B, ntok, LANES), BF16),
                        jax.ShapeDtypeStruct((H_B, LANES, ntok), BF16)])
        scratch = [pltpu.VMEM((1, LANES), F32)]
    else:
        tm = min(ntok, 512)
        n_t = ntok // tm
        grid = (1, n_t)
        tok = lambda b, t: (t, 0)
        tok4 = lambda b, t: (t, 0, 0)
        out_specs = ([pl.BlockSpec((tm, H_B, D_HB), tok4)] * 2 + [pl.BlockSpec((tm, H_B), tok)]
                     + [pl.BlockSpec((tm, hd), tok)] * 4)
        out_shape = ([jax.ShapeDtypeStruct((ntok, H_B, D_HB), F32)] * 2
                     + [jax.ShapeDtypeStruct((ntok, H_B), F32)]
                     + [jax.ShapeDtypeStruct((ntok, hd), F32)] * 4)
        scratch = [pltpu.VMEM((H_B * _head_pitch(tm), LANES), F32)] * 2
    return pl.pallas_call(
        functools.partial(_proj_kernel, aug=aug),
        grid=grid,
        in_specs=[pl.BlockSpec((tm, d), tok)] + [_const_spec(a.shape) for a in ins[1:]],
        out_specs=out_specs,
        out_shape=out_shape,
        scratch_shapes=scratch,
        compiler_params=pltpu.CompilerParams(
            dimension_semantics=("arbitrary", "arbitrary"), vmem_limit_bytes=VMEM_LIMIT),
        name="kvq_proj",
    )(*ins)


def _cumsum_kernel(lf_ref, f_ref):
    n_blk = lf_ref.shape[2] // LANES
    upper = (lax.broadcasted_iota(jnp.int32, (LANES, LANES), 0)
             <= lax.broadcasted_iota(jnp.int32, (LANES, LANES), 1)).astype(BF16)
    run = jnp.zeros((lf_ref.shape[1], 1), F32)
    for c in range(n_blk):
        x = lf_ref[0, :, c * LANES:(c + 1) * LANES]
        h1 = x.astype(BF16)
        r1 = x - h1.astype(F32)
        h2 = r1.astype(BF16)
        h3 = (r1 - h2.astype(F32)).astype(BF16)
        cs = (_dot(h1, upper) + _dot(h2, upper)) + _dot(h3, upper) + run
        f_ref[0, :, c * LANES:(c + 1) * LANES] = cs
        run = cs[:, LANES - 1:LANES]


def _cumsum_time(lf_t):
    b, h, l = lf_t.shape
    return pl.pallas_call(
        _cumsum_kernel,
        grid=(b,),
        in_specs=[pl.BlockSpec((1, h, l), lambda i: (i, 0, 0))],
        out_specs=pl.BlockSpec((1, h, l), lambda i: (i, 0, 0)),
        out_shape=jax.ShapeDtypeStruct((b, h, l), F32),
        compiler_params=pltpu.CompilerParams(dimension_semantics=("arbitrary",)),
        name="logf_cumsum",
    )(lf_t)


def _head_column(f_tile, head):
    lane = lax.broadcasted_iota(jnp.int32, f_tile.shape, 1)
    return jnp.sum(jnp.where(lane == head, f_tile, 0.0), axis=-1, keepdims=True)


def _attn_prompt_kernel(qa_ref, ka_ref, vat_ref, g_ref, o_ref, m_scr, acc_scr, *, tq):
    seq = qa_ref.shape[1]
    n_q = seq // tq
    half = tq // 2
    row_o = lax.broadcasted_iota(jnp.int32, (LANES, tq), 0)

    def visible(keys, queries):
        return (lax.broadcasted_iota(jnp.int32, (keys, queries), 0)
                <= lax.broadcasted_iota(jnp.int32, (keys, queries), 1))

    items = []
    for qt in range(n_q):
        for kt in range(qt):
            items.append((qt, 0, tq, kt * tq, tq, None))
        items.append((qt, 0, tq, qt * tq, half, visible(half, tq)))
        items.append((qt, half, half, qt * tq + half, half, visible(half, half)))

    def scores(item):
        qt, q_lo, q_len, c0, c_len, _ = item
        return [_dot_nt(ka_ref[hf, c0:c0 + c_len, :],
                        qa_ref[hf, qt * tq + q_lo:qt * tq + q_lo + q_len, :]) for hf in range(2)]

    def update(item, s_pair):
        _, q_lo, q_len, c0, c_len, mask = item
        cols = slice(q_lo, q_lo + q_len)
        for hf in range(2):
            st = s_pair[hf]
            if mask is not None:
                st = jnp.where(mask, st, -jnp.inf)
            m_old = m_scr[hf, :, cols]
            m_new = jnp.maximum(m_old, jnp.max(st, axis=0, keepdims=True))
            pt = jnp.exp2(st - m_new).astype(BF16)
            acc_scr[hf, :, cols] = (jnp.exp2(m_old - m_new) * acc_scr[hf, :, cols]
                                    + _dot(vat_ref[hf, :, c0:c0 + c_len], pt))
            m_scr[hf, :, cols] = m_new

    def finish(qt):
        outs = []
        for hf in range(2):
            acc = acc_scr[hf]
            outs.append(acc / acc[AUG_BASE[hf]:AUG_BASE[hf] + 1, :])
        o = jnp.where(row_o < D_HB, outs[0], outs[1]).T
        o = o * _sigmoid(g_ref[0, qt * tq:(qt + 1) * tq, :])
        o_ref[0, qt * tq:(qt + 1) * tq, :] = o.astype(BF16)

    s_next = scores(items[0])
    for i, item in enumerate(items):
        s_cur = s_next
        if i + 1 < len(items):
            s_next = scores(items[i + 1])
        if i == 0 or items[i - 1][0] != item[0]:
            m_scr[...] = jnp.full(m_scr.shape, -jnp.inf, F32)
            acc_scr[...] = jnp.zeros(acc_scr.shape, F32)
        update(item, s_cur)
        if i + 1 == len(items) or items[i + 1][0] != item[0]:
            finish(item[0])


def _attn_prompt(qa, ka, vat, g):
    b, t, hd = g.shape
    tq = min(t, 512)
    n_pairs = hd // LANES
    heads = pl.BlockSpec((2, t, LANES), lambda i, j: (j, i, 0))
    heads_t = pl.BlockSpec((2, LANES, t), lambda i, j: (j, 0, i))
    col = pl.BlockSpec((1, t, LANES), lambda i, j: (i, 0, j))
    kern = functools.partial(_attn_prompt_kernel, tq=tq)
    return pl.pallas_call(
        kern,
        grid=(b, n_pairs),
        in_specs=[heads, heads, heads_t, col],
        out_specs=col,
        out_shape=jax.ShapeDtypeStruct((b, t, hd), BF16),
        scratch_shapes=[pltpu.VMEM((2, 1, tq), F32), pltpu.VMEM((2, LANES, tq), F32)],
        compiler_params=pltpu.CompilerParams(
            dimension_semantics=("arbitrary", "arbitrary"), vmem_limit_bytes=VMEM_LIMIT),
        name="fox_attn_prompt",
    )(qa, ka, vat, g)


def _pair_block(qs, k2, v2, fq, fk, mask, m_ref, l_ref, acc_ref, keys_on_lanes=False):
    tq = qs.shape[0] // 2
    tk = k2.shape[1] if keys_on_lanes else k2.shape[0]
    width = min(tk, LANES)
    s = _dot(qs, k2) if keys_on_lanes else _dot_nt(qs, k2)
    alphas, probs = [], []
    for hf in range(2):
        sh = s[hf * tq:(hf + 1) * tq] + (fq[hf] - fk[hf])
        if mask is not None:
            sh = jnp.where(mask, sh, -jnp.inf)
        cols = [sh[:, c * width:(c + 1) * width] for c in range(tk // width)]
        m_old = m_ref[hf]
        m_new = jnp.maximum(
            m_old, jnp.max(functools.reduce(jnp.maximum, cols), axis=-1, keepdims=True))
        p_cols = [jnp.exp(c - m_new[:, :width]) for c in cols]
        alpha = jnp.exp(m_old - m_new)
        l_ref[hf] = alpha * l_ref[hf] + jnp.sum(
            functools.reduce(lambda a, b: a + b, p_cols), axis=-1, keepdims=True)
        m_ref[hf] = m_new
        probs.append(jnp.concatenate([p.astype(BF16) for p in p_cols], axis=1))
        alphas.append(alpha)
    p_all = jnp.concatenate(probs, axis=0)
    pv = _dot_nt(p_all, v2) if keys_on_lanes else _dot(p_all, v2)
    acc_ref[...] = jnp.concatenate(alphas, axis=0) * acc_ref[...] + pv


def _split_heads(q2, lane_a):
    zero = jnp.zeros_like(q2)
    return jnp.concatenate([jnp.where(lane_a, q2, zero), jnp.where(lane_a, zero, q2)], axis=0)


def _attn_sample_kernel(q_ref, g_ref, kc_ref, vc_ref, kn_ref, vn_ref, fq_ref, fkc_ref, fkn_ref,
                        o_ref, m_scr, l_scr, acc_scr, *, n_pairs):
    kt = pl.program_id(1)
    n_kt = pl.num_programs(1)
    tq = q_ref.shape[1]
    scale = D_HB ** -0.5
    lane_a = lax.broadcasted_iota(jnp.int32, (tq, LANES), 1) < D_HB
    causal = (lax.broadcasted_iota(jnp.int32, (tq, tq), 1)
              <= lax.broadcasted_iota(jnp.int32, (tq, tq), 0))
    f_tile = fq_ref[0]

    @pl.when(kt == 0)
    def _():
        m_scr[...] = jnp.full(m_scr.shape, -jnp.inf, F32)
        l_scr[...] = jnp.zeros(l_scr.shape, F32)
        acc_scr[...] = jnp.zeros(acc_scr.shape, F32)

    def cache_pair(ref, hp):
        return ref[0, 2 * hp:2 * hp + 2].reshape(2 * D_HB, ref.shape[3]).astype(BF16)

    def pair_inputs(hp):
        cols = slice(hp * LANES, (hp + 1) * LANES)
        qs = _split_heads(q_ref[0, :, cols] * scale, lane_a).astype(BF16)
        fq = [_head_column(f_tile, 2 * hp + hf) for hf in range(2)]
        return cols, qs, fq

    for hp in range(n_pairs):
        cols, qs, fq = pair_inputs(hp)
        fk = [fkc_ref[0, 2 * hp + hf:2 * hp + hf + 1, :] for hf in range(2)]
        _pair_block(qs, cache_pair(kc_ref, hp), cache_pair(vc_ref, hp),
                    fq, fk, None, m_scr.at[hp], l_scr.at[hp], acc_scr.at[hp], keys_on_lanes=True)

    @pl.when(kt == n_kt - 1)
    def _():
        for hp in range(n_pairs):
            cols, qs, fq = pair_inputs(hp)
            fk = [fkn_ref[0, 2 * hp + hf:2 * hp + hf + 1, 0:tq] for hf in range(2)]
            _pair_block(qs, kn_ref[0, :, cols].astype(BF16), vn_ref[0, :, cols].astype(BF16),
                        fq, fk, causal, m_scr.at[hp], l_scr.at[hp], acc_scr.at[hp])
            acc = acc_scr[hp]
            o = jnp.where(lane_a, acc[:tq] / l_scr[hp, 0], acc[tq:] / l_scr[hp, 1])
            o = o * _sigmoid(g_ref[0, :, cols])
            o_ref[0, :, cols] = o.astype(BF16)


def _attn_sample(q, g, k_cache, v_cache, k_new, v_new, f, f_t, past):
    b, t, hd = q.shape
    n_pairs = hd // LANES
    tk = min(past, 1024)
    n_kt = past // tk
    row = pl.BlockSpec((1, t, hd), lambda i, j: (i, 0, 0))
    cache = pl.BlockSpec((1, H_B, D_HB, tk), lambda i, j: (i, 0, 0, j))
    kern = functools.partial(_attn_sample_kernel, n_pairs=n_pairs)
    return pl.pallas_call(
        kern,
        grid=(b, n_kt),
        in_specs=[row, row, cache, cache, row, row,
                  pl.BlockSpec((1, t, H_B), lambda i, j: (i, past // t, 0)),
                  pl.BlockSpec((1, H_B, tk), lambda i, j: (i, 0, j)),
                  pl.BlockSpec((1, H_B, LANES), lambda i, j: (i, 0, past // LANES))],
        out_specs=row,
        out_shape=jax.ShapeDtypeStruct((b, t, hd), BF16),
        scratch_shapes=[pltpu.VMEM((n_pairs, 2, t, LANES), F32), pltpu.VMEM((n_pairs, 2, t, LANES), F32),
                        pltpu.VMEM((n_pairs, 2 * t, LANES), F32)],
        compiler_params=pltpu.CompilerParams(
            dimension_semantics=("arbitrary", "arbitrary"), vmem_limit_bytes=VMEM_LIMIT),
        name="fox_attn_sample",
    )(q, g, k_cache, v_cache, k_new, v_new, f, f_t, f_t)


def _trunk(x, state0, k_past, v_past, logf_past, w):
    batch, seq, d = x.shape
    ntok = batch * seq
    hd = H_B * D_HB
    x2d = x.reshape(ntok, d)
    s0 = None if state0 is None else state0[:, 0]
    o_a, state = _hgrn(x2d, batch, seq, w["norm_a"], w["w_in_h"], w["lb_logits_h"], w["g_norm_a"], s0)
    x2d = _post(x2d, o_a, w["w_o_a"], w["norm_mlp0"], w["w_up0"], w["w_down0"])
    proj_w = (w["norm_kv"], w["norm_b"], w["w_k"], w["w_v"], w["w_f"], w["b_f"], w["w_qq"], w["w_qg"])
    if k_past is None:
        k_t, v_t, logf_t, g, qa, ka, va = _proj(x2d, batch, seq, True, *proj_w)
        k, v = jnp.transpose(k_t, (0, 3, 1, 2)), jnp.transpose(v_t, (0, 3, 1, 2))
        logf = jnp.swapaxes(logf_t, 1, 2)
        o_b = _attn_prompt(qa, ka, va, g.reshape(batch, seq, hd))
    else:
        k, v, logf, g, q, k_flat, v_flat = _proj(x2d, batch, seq, False, *proj_w)
        past = k_past.shape[1]
        total = past + seq
        padded = -(-total // LANES) * LANES
        lf_all = jnp.concatenate(
            [jnp.swapaxes(logf_past.astype(F32), 1, 2),
             jnp.swapaxes(logf.reshape(batch, seq, H_B), 1, 2),
             jnp.zeros((batch, H_B, padded - total), F32)], axis=2)
        f_t = _cumsum_time(lf_all)
        o_b = _attn_sample(q.reshape(batch, seq, hd), g.reshape(batch, seq, hd),
                           jnp.transpose(k_past, (0, 2, 3, 1)).astype(F32),
                           jnp.transpose(v_past, (0, 2, 3, 1)).astype(F32),
                           k_flat.reshape(batch, seq, hd), v_flat.reshape(batch, seq, hd),
                           jnp.swapaxes(f_t, 1, 2), f_t, past)
    y = _post(x2d, o_b.reshape(ntok, hd), w["w_o_b"], w["norm_mlp1"], w["w_up1"], w["w_down1"],
              w["norm_f"])
    return (y.reshape(batch, seq, d), state[:, None],
            k.reshape(batch, seq, H_B, D_HB), v.reshape(batch, seq, H_B, D_HB),
            logf.reshape(batch, seq, H_B))


def kernel(x_prompt, x_sample, state_hgrn, cache_k, cache_v, cache_logf, norm_a, w_in_a, lb_logits, g_norm_a, w_o_a, norm_kv, w_kv, b_f, norm_b, w_q_b, w_o_b, norm_mlp, w_up, w_down, norm_f):
    d = x_prompt.shape[-1]
    assert w_in_a.shape[0] == 1 and w_q_b.shape[0] == 1, "one HGRN2 layer and one FoX layer"
    n_heads = w_in_a.shape[2] // (4 * K_A)
    hd = H_B * D_HB
    row = lambda a: a.reshape(1, -1).astype(F32)
    w_f = jnp.zeros((d, LANES), F32).at[:, :H_B].set(w_kv[:, 2 * hd:])
    w = {
        "norm_a": row(norm_a[0]),
        "w_in_h": w_in_a[0].reshape(d, 4, n_heads, K_A).transpose(2, 0, 1, 3)
                           .reshape(n_heads, d, 4 * K_A).astype(BF16),
        "lb_logits_h": lb_logits.astype(F32).reshape(lb_logits.shape[0], n_heads, 1, K_A),
        "g_norm_a": row(g_norm_a[0]),
        "w_o_a": w_o_a[0].astype(BF16),
        "norm_mlp0": row(norm_mlp[0]), "w_up0": w_up[0].astype(BF16), "w_down0": w_down[0].astype(BF16),
        "norm_mlp1": row(norm_mlp[1]), "w_up1": w_up[1].astype(BF16), "w_down1": w_down[1].astype(BF16),
        "norm_kv": row(norm_kv), "norm_b": row(norm_b[0]),
        "w_k": w_kv[:, :hd].astype(BF16), "w_v": w_kv[:, hd:2 * hd].astype(BF16),
        "w_f": w_f.astype(BF16), "b_f": jnp.zeros((1, LANES), F32).at[0, :H_B].set(b_f.astype(F32)),
        "w_qq": w_q_b[0][:, :hd].astype(BF16), "w_qg": w_q_b[0][:, hd:].astype(BF16),
        "w_o_b": w_o_b[0].astype(BF16),
        "norm_f": row(norm_f),
    }
    y_p, st_p, k_p, v_p, lf_p = _trunk(x_prompt, None, None, None, None, w)
    y_s, st_s, k_s, v_s, lf_s = _trunk(x_sample, state_hgrn, cache_k, cache_v, cache_logf, w)
    return (y_p, y_s, st_p, k_p, v_p, lf_p, st_s, k_s, v_s, lf_s)
```

```python
import functools

import jax
import jax.numpy as jnp
from jax import lax
from jax.experimental import pallas as pl
from jax.experimental.pallas import tpu as pltpu

EPS = 1e-6
CHUNK = 64
K_A = 128
V_A = 128
H_B = 16
D_HB = 64
LANES = 128
VMEM_LIMIT = 56 * 1024 * 1024

F32 = jnp.float32
BF16 = jnp.bfloat16


def _dot(a, b):
    return jnp.dot(a, b, preferred_element_type=F32)


def _dot_nt(a, b):
    return lax.dot_general(a, b, (((1,), (1,)), ((), ())), preferred_element_type=F32)


def _dot_tn(a, b):
    return lax.dot_general(a, b, (((0,), (0,)), ((), ())), preferred_element_type=F32)


def _sigmoid(x):
    return 0.5 * jnp.tanh(0.5 * x) + 0.5


def _const_spec(shape):
    nd = len(shape)
    return pl.BlockSpec(shape, lambda *_: (0,) * nd, pipeline_mode=pl.Buffered(1))


def _hgrn_kernel(*refs, n_heads, n_chunks, carry):
    if carry:
        (x_ref, nrm_ref, win_ref, lbl_ref, gn_ref, o_ref, sout_ref, xn_scr, st_scr) = refs
        s0_ref = None
    else:
        (x_ref, nrm_ref, win_ref, lbl_ref, gn_ref, s0_ref, o_ref, sout_ref, xn_scr) = refs
        st_scr = None
    tm = n_chunks * CHUNK

    x = x_ref[...]
    inv = lax.rsqrt(jnp.mean(x * x, axis=-1, keepdims=True) + EPS)
    xn_scr[...] = (x * inv * nrm_ref[...]).astype(BF16)

    if carry:
        @pl.when(pl.program_id(1) == 0)
        def _():
            st_scr[...] = jnp.zeros_like(st_scr)

    row_in_chunk = lax.broadcasted_iota(jnp.int32, (tm, K_A), 0) % CHUNK
    tri = (lax.broadcasted_iota(jnp.int32, (CHUNK, CHUNK), 1)
           <= lax.broadcasted_iota(jnp.int32, (CHUNK, CHUNK), 0))

    d_a = n_heads * K_A

    def project(pair):
        xn = xn_scr[...]
        return [_dot(xn, win_ref[:, part * d_a + pair * 2 * K_A:part * d_a + (pair + 1) * 2 * K_A])
                for part in range(4)]

    def gates(h, p4):
        half = slice((h % 2) * K_A, (h % 2 + 1) * K_A)
        pq, pz, pi, pg = (p[:, half] for p in p4)
        n_rows = lbl_ref.shape[0]
        lrows = [lbl_ref[r, h] for r in range(n_rows)]
        lmax = functools.reduce(jnp.maximum, lrows)
        lexp = [jnp.exp(l - lmax) for l in lrows]
        lb = lexp[0] / functools.reduce(lambda a, b: a + b, lexp)

        q = pq * _sigmoid(pq)
        f = lb + (1.0 - lb) * _sigmoid(pz)
        logf = jnp.log(f)
        k = 1.0 - f
        bc = logf
        s = 1
        while s < CHUNK:
            bc = bc + jnp.where(row_in_chunk >= s, pltpu.roll(bc, s, axis=0), 0.0)
            s *= 2
        chunked = lambda a: a.reshape(n_chunks, CHUNK, K_A)
        bc3 = chunked(bc)
        btot = bc3[:, CHUNK - 1:CHUNK, :]
        q_dec = chunked(q * jnp.exp(bc)).astype(BF16)
        k_inv = chunked(k * jnp.exp(-bc)).astype(BF16)
        k_end = (chunked(k) * jnp.exp(btot - bc3)).astype(BF16)
        decay = jnp.exp(btot)
        v = chunked(pi).astype(BF16)
        return q_dec, k_inv, k_end, decay, v, _sigmoid(pg)

    def recur(h, q_dec, k_inv, k_end, decay, v, gate):
        sc = jnp.einsum('cqk,csk->cqs', q_dec, k_inv, preferred_element_type=F32)
        sc = jnp.where(tri, sc, 0.0).astype(BF16)
        o_intra = jnp.einsum('cqs,csv->cqv', sc, v, preferred_element_type=F32)
        upd = jnp.einsum('csv,csk->cvk', v, k_end, preferred_element_type=F32)
        states = []
        if carry:
            st = st_scr[h]
            for c in range(n_chunks):
                states.append(st)
                st = st * decay[c] + upd[c]
            st_scr[h] = st
            sout_ref[0, h] = st.T
        else:
            for c in range(n_chunks):
                st = s0_ref[c, h].T
                states.append(st)
                sout_ref[c, h] = (st * decay[c] + upd[c]).T
        s_in = jnp.stack(states, axis=0).astype(BF16)
        o_inter = jnp.einsum('cqk,cvk->cqv', q_dec, s_in, preferred_element_type=F32)
        o = (o_intra + o_inter).reshape(tm, V_A)
        o = o * lax.rsqrt(jnp.mean(o * o, axis=-1, keepdims=True) + EPS) * gn_ref[...]
        o_ref[:, h * V_A:(h + 1) * V_A] = (o * gate).astype(BF16)

    p_next = project(0)
    staged = None
    for h in range(n_heads):
        if h % 2 == 0:
            p_cur = p_next
            if h + 2 < n_heads:
                p_next = project(h // 2 + 1)
        ready = gates(h, p_cur)
        if staged is not None:
            recur(h - 1, *staged)
        staged = ready
    recur(n_heads - 1, *staged)


def _hgrn(x2d, batch, seq, norm_w, w_in, lb_logits_h, g_norm, s0):
    ntok, d = x2d.shape
    n_heads = w_in.shape[1] // (4 * K_A)
    carry = s0 is None
    if carry:
        tm = min(seq, 512)
        n_t = seq // tm
        grid = (batch, n_t)
        tok_map = lambda b, t: (b * n_t + t, 0)
        st_spec = pl.BlockSpec((1, n_heads, K_A, V_A), lambda b, t: (b, 0, 0, 0))
        scratch = [pltpu.VMEM((tm, d), BF16), pltpu.VMEM((n_heads, V_A, K_A), F32)]
        extra_in, extra_specs = [], []
    else:
        assert seq == CHUNK
        bt = min(batch, 8)
        tm = bt * CHUNK
        grid = (batch // bt, 1)
        tok_map = lambda b, t: (b, 0)
        st_spec = pl.BlockSpec((bt, n_heads, K_A, V_A), lambda b, t: (b, 0, 0, 0))
        scratch = [pltpu.VMEM((tm, d), BF16)]
        extra_in, extra_specs = [s0], [st_spec]
    n_chunks = tm // CHUNK
    kern = functools.partial(_hgrn_kernel, n_heads=n_heads, n_chunks=n_chunks, carry=carry)
    return pl.pallas_call(
        kern,
        grid=grid,
        in_specs=[pl.BlockSpec((tm, d), tok_map),
                  _const_spec(norm_w.shape), _const_spec(w_in.shape),
                  _const_spec(lb_logits_h.shape), _const_spec(g_norm.shape)] + extra_specs,
        out_specs=[pl.BlockSpec((tm, n_heads * V_A), tok_map), st_spec],
        out_shape=[jax.ShapeDtypeStruct((ntok, n_heads * V_A), BF16),
                   jax.ShapeDtypeStruct((batch, n_heads, K_A, V_A), F32)],
        scratch_shapes=scratch,
        compiler_params=pltpu.CompilerParams(
            dimension_semantics=("arbitrary", "arbitrary"), vmem_limit_bytes=VMEM_LIMIT),
        name="hgrn_mixer",
    )(x2d, norm_w, w_in, lb_logits_h, g_norm, *extra_in)


def _post_kernel(*refs, ff_block, final_norm):
    if final_norm:
        x_ref, o_ref, wo_ref, nm_ref, wup_ref, wdn_ref, nf_ref, y_ref = refs
    else:
        x_ref, o_ref, wo_ref, nm_ref, wup_ref, wdn_ref, y_ref = refs
    x1 = x_ref[...] + _dot(o_ref[...], wo_ref[...])
    inv = lax.rsqrt(jnp.mean(x1 * x1, axis=-1, keepdims=True) + EPS)
    xn = (x1 * inv * nm_ref[...]).astype(BF16)
    acc = x1
    d_ff = wup_ref.shape[1]
    for j in range(d_ff // ff_block):
        hcol = jnp.maximum(_dot(xn, wup_ref[:, j * ff_block:(j + 1) * ff_block]), 0.0)
        acc = acc + _dot((hcol * hcol).astype(BF16), wdn_ref[j * ff_block:(j + 1) * ff_block, :])
    if final_norm:
        inv = lax.rsqrt(jnp.mean(acc * acc, axis=-1, keepdims=True) + EPS)
        acc = acc * inv * nf_ref[...]
    y_ref[...] = acc


def _post(x2d, o2d, w_o, norm_mlp, w_up, w_down, norm_f=None):
    ntok, d = x2d.shape
    tm = min(ntok, 512)
    final_norm = norm_f is not None
    tok = lambda i: (i, 0)
    ins = [x2d, o2d, w_o, norm_mlp, w_up, w_down] + ([norm_f] if final_norm else [])
    specs = [pl.BlockSpec((tm, d), tok), pl.BlockSpec((tm, o2d.shape[1]), tok)]
    specs += [_const_spec(a.shape) for a in ins[2:]]
    kern = functools.partial(_post_kernel, ff_block=1024, final_norm=final_norm)
    return pl.pallas_call(
        kern,
        grid=(ntok // tm,),
        in_specs=specs,
        out_specs=pl.BlockSpec((tm, d), tok),
        out_shape=jax.ShapeDtypeStruct((ntok, d), F32),
        compiler_params=pltpu.CompilerParams(
            dimension_semantics=("arbitrary",), vmem_limit_bytes=VMEM_LIMIT),
        name="post_mlp",
    )(*ins)


LOG2E = 1.4426950408889634
AUG_BASE = (D_HB, 0)


def _split3(x):
    h1 = x.astype(BF16).astype(F32)
    r1 = x - h1
    h2 = r1.astype(BF16).astype(F32)
    h3 = (r1 - h2).astype(BF16).astype(F32)
    return h1, h2, h3


def _head_pitch(tm):
    return tm + 8 if (tm // 8) % 2 == 0 else tm


def _store_head_major(val, scr, out_ref):
    tm = val.shape[0]
    pitch = _head_pitch(tm)
    for h in range(H_B):
        pair = val[:, (h // 2) * LANES:(h // 2 + 1) * LANES]
        scr[h * pitch:h * pitch + tm, :] = pair if h % 2 == 0 else pltpu.roll(pair, D_HB, axis=1)

    for t in range(tm):
        for grp in range(H_B // 8):
            rows = scr[pl.ds(grp * 8 * pitch + t, 8, stride=pitch), :]
            out_ref[t, grp * 8:(grp + 1) * 8, :] = rows[:, :D_HB]


def _proj_kernel(*refs, aug):
    if aug:
        (x_ref, nkv_ref, nb_ref, wk_ref, wv_ref, wf_ref, bf_ref, wq_ref, wg_ref,
         k_ref, v_ref, lf_ref, g_ref, qa_ref, ka_ref, va_ref, carry) = refs
    else:
        (x_ref, nkv_ref, nb_ref, wk_ref, wv_ref, wf_ref, bf_ref, wq_ref, wg_ref,
         k_ref, v_ref, lf_ref, g_ref, q_ref, kf_ref, vf_ref, k_scr, v_scr) = refs
    tm = x_ref.shape[0]
    x = x_ref[...]
    xs = x * lax.rsqrt(jnp.mean(x * x, axis=-1, keepdims=True) + EPS)
    xk = (xs * nkv_ref[...]).astype(BF16)
    xq = (xs * nb_ref[...]).astype(BF16)
    z = _dot(xk, wf_ref[...]) + bf_ref[...]
    lf = jnp.minimum(z, 0.0) - jnp.log(1.0 + jnp.exp(-jnp.abs(z)))
    if not aug:
        k = _dot(xk, wk_ref[...])
        v = _dot(xk, wv_ref[...])
        g_ref[...] = _dot(xq, wg_ref[...])
        _store_head_major(k, k_scr, k_ref)
        _store_head_major(v, v_scr, v_ref)
        lf_ref[...] = lf[:, :H_B]
        q_ref[...] = _dot(xq, wq_ref[...])
        kf_ref[...] = k
        vf_ref[...] = v
        return

    lf_ref[0] = lf.T[:H_B, :]

    @pl.when(pl.program_id(1) == 0)
    def _():
        carry[...] = jnp.zeros_like(carry)

    row = lax.broadcasted_iota(jnp.int32, (tm, LANES), 0)
    lane = lax.broadcasted_iota(jnp.int32, (tm, LANES), 1)
    f = lf
    s = 1
    while s < tm:
        f = f + jnp.where(row >= s, pltpu.roll(f, s, axis=0), 0.0)
        s *= 2
    f = f + carry[...]
    carry[...] = f[tm - 1:tm, :]
    h1, h2, h3 = _split3(f * LOG2E)
    pieces = jnp.where(lane < H_B, h1,
                       jnp.where(lane < 2 * H_B, pltpu.roll(h2, H_B, axis=1),
                                 jnp.where(lane < 3 * H_B, pltpu.roll(h3, 2 * H_B, axis=1), 0.0)))
    both = pieces + pltpu.roll(pieces, D_HB, axis=1)
    shifted = pltpu.roll(both, 1, axis=1)
    neg_both = -both
    low = lane < D_HB

    def tails(h):
        base = AUG_BASE[h % 2]
        at = (lane == base + h) | (lane == base + H_B + h) | (lane == base + 2 * H_B + h)
        up = (lane == base + h + 1) | (lane == base + H_B + h + 1) | (lane == base + 2 * H_B + h + 1)
        q_tail = jnp.where(up, shifted, jnp.where(at, 1.0, 0.0))
        k_tail = jnp.where(at, neg_both, jnp.where(up, 1.0, 0.0))
        return q_tail, k_tail

    def put(ref, h, data, tail):
        ref[h] = (jnp.where(low, data, tail) if h % 2 == 0 else jnp.where(low, tail, data)).astype(BF16)

    k = _dot(xk, wk_ref[...])
    k_ref[0] = k.T.reshape(H_B, D_HB, tm)
    for h in range(H_B):
        put(ka_ref, h, k[:, (h // 2) * LANES:(h // 2 + 1) * LANES], tails(h)[1])

    v_t = _dot(xk, wv_ref[...]).T
    v_ref[0] = v_t.reshape(H_B, D_HB, tm)
    row_t = lax.broadcasted_iota(jnp.int32, (LANES, tm), 0)
    for j in range(H_B // 2):
        v_pair = v_t[j * LANES:(j + 1) * LANES, :]
        va_ref[2 * j] = jnp.where(row_t < D_HB, v_pair,
                                  jnp.where(row_t == AUG_BASE[0], 1.0, 0.0)).astype(BF16)
        va_ref[2 * j + 1] = jnp.where(row_t < D_HB, jnp.where(row_t == AUG_BASE[1], 1.0, 0.0),
                                      v_pair).astype(BF16)

    q = _dot(xq, wq_ref[...]) * ((D_HB ** -0.5) * LOG2E)
    for h in range(H_B):
        put(qa_ref, h, q[:, (h // 2) * LANES:(h // 2 + 1) * LANES], tails(h)[0])

    g_ref[...] = _dot(xq, wg_ref[...])


def _proj(x2d, batch, seq, aug, norm_kv, norm_b, w_k, w_v, w_f, b_f, w_qq, w_qg):
    ntok, d = x2d.shape
    hd = w_k.shape[1]
    ins = [x2d, norm_kv, norm_b, w_k, w_v, w_f, b_f, w_qq, w_qg]
    if aug:
        tm = min(seq, 512)
        n_t = seq // tm
        grid = (batch, n_t)
        tok = lambda b, t: (b * n_t + t, 0)
        head_major = lambda b, t: (0, b * n_t + t, 0)
        time_minor = pl.BlockSpec((1, H_B, D_HB, tm), lambda b, t: (b, 0, 0, t))
        out_specs = [time_minor, time_minor, pl.BlockSpec((1, H_B, tm), lambda b, t: (b, 0, t)),
                     pl.BlockSpec((tm, hd), tok),
                     pl.BlockSpec((H_B, tm, LANES), head_major),
                     pl.BlockSpec((H_B, tm, LANES), head_major),
                     pl.BlockSpec((H_B, LANES, tm), lambda b, t: (0, 0, b * n_t + t))]
        out_shape = ([jax.ShapeDtypeStruct((batch, H_B, D_HB, seq), F32)] * 2
                     + [jax.ShapeDtypeStruct((batch, H_B, seq), F32),
                        jax.ShapeDtypeStruct((ntok, hd), F32),
                        jax.ShapeDtypeStruct((H_B, ntok, LANES), BF16),
                        jax.ShapeDtypeStruct((H_B, ntok, LANES), BF16),
                        jax.ShapeDtypeStruct((H_B, LANES, ntok), BF16)])
        scratch = [pltpu.VMEM((1, LANES), F32)]
    else:
        tm = min(ntok, 512)
        n_t = ntok // tm
        grid = (1, n_t)
        tok = lambda b, t: (t, 0)
        tok4 = lambda b, t: (t, 0, 0)
        out_specs = ([pl.BlockSpec((tm, H_B, D_HB), tok4)] * 2 + [pl.BlockSpec((tm, H_B), tok)]
                     + [pl.BlockSpec((tm, hd), tok)] * 4)
        out_shape = ([jax.ShapeDtypeStruct((ntok, H_B, D_HB), F32)] * 2
                     + [jax.ShapeDtypeStruct((ntok, H_B), F32)]
                     + [jax.ShapeDtypeStruct((ntok, hd), F32)] * 4)
        scratch = [pltpu.VMEM((H_B * _head_pitch(tm), LANES), F32)] * 2
    return pl.pallas_call(
        functools.partial(_proj_kernel, aug=aug),
        grid=grid,
        in_specs=[pl.BlockSpec((tm, d), tok)] + [_const_spec(a.shape) for a in ins[1:]],
        out_specs=out_specs,
        out_shape=out_shape,
        scratch_shapes=scratch,
        compiler_params=pltpu.CompilerParams(
            dimension_semantics=("arbitrary", "arbitrary"), vmem_limit_bytes=VMEM_LIMIT),
        name="kvq_proj",
    )(*ins)


def _cumsum_kernel(lf_ref, f_ref):
    n_blk = lf_ref.shape[2] // LANES
    upper = (lax.broadcasted_iota(jnp.int32, (LANES, LANES), 0)
             <= lax.broadcasted_iota(jnp.int32, (LANES, LANES), 1)).astype(BF16)
    run = jnp.zeros((lf_ref.shape[1], 1), F32)
    for c in range(n_blk):
        x = lf_ref[0, :, c * LANES:(c + 1) * LANES]
        h1 = x.astype(BF16)
        r1 = x - h1.astype(F32)
        h2 = r1.astype(BF16)
        h3 = (r1 - h2.astype(F32)).astype(BF16)
        cs = (_dot(h1, upper) + _dot(h2, upper)) + _dot(h3, upper) + run
        f_ref[0, :, c * LANES:(c + 1) * LANES] = cs
        run = cs[:, LANES - 1:LANES]


def _cumsum_time(lf_t):
    b, h, l = lf_t.shape
    return pl.pallas_call(
        _cumsum_kernel,
        grid=(b,),
        in_specs=[pl.BlockSpec((1, h, l), lambda i: (i, 0, 0))],
        out_specs=pl.BlockSpec((1, h, l), lambda i: (i, 0, 0)),
        out_shape=jax.ShapeDtypeStruct((b, h, l), F32),
        compiler_params=pltpu.CompilerParams(dimension_semantics=("arbitrary",)),
        name="logf_cumsum",
    )(lf_t)


def _head_column(f_tile, head):
    lane = lax.broadcasted_iota(jnp.int32, f_tile.shape, 1)
    return jnp.sum(jnp.where(lane == head, f_tile, 0.0), axis=-1, keepdims=True)


def _attn_prompt_kernel(qa_ref, ka_ref, vat_ref, g_ref, o_ref, m_scr, acc_scr, *, tq):
    seq = qa_ref.shape[1]
    n_q = seq // tq
    half = tq // 2
    row_o = lax.broadcasted_iota(jnp.int32, (LANES, tq), 0)

    def visible(keys, queries):
        return (lax.broadcasted_iota(jnp.int32, (keys, queries), 0)
                <= lax.broadcasted_iota(jnp.int32, (keys, queries), 1))

    items = []
    for pair in range(qa_ref.shape[0] // 2):
        for qt in range(n_q):
            tile = (pair, qt)
            for kt in range(qt):
                items.append((tile, 0, tq, kt * tq, tq, None))
            items.append((tile, 0, tq, qt * tq, half, visible(half, tq)))
            items.append((tile, half, half, qt * tq + half, half, visible(half, half)))

    def scores(item):
        (pair, qt), q_lo, q_len, c0, c_len, _ = item
        return [_dot_nt(ka_ref[2 * pair + hf, c0:c0 + c_len, :],
                        qa_ref[2 * pair + hf, qt * tq + q_lo:qt * tq + q_lo + q_len, :])
                for hf in range(2)]

    def update(item, s_pair):
        (pair, _), q_lo, q_len, c0, c_len, mask = item
        cols = slice(q_lo, q_lo + q_len)
        for hf in range(2):
            st = s_pair[hf]
            if mask is not None:
                st = jnp.where(mask, st, -jnp.inf)
            m_old = m_scr[hf, :, cols]
            m_new = jnp.maximum(m_old, jnp.max(st, axis=0, keepdims=True))
            pt = jnp.exp2(st - m_new).astype(BF16)
            acc_scr[hf, :, cols] = (jnp.exp2(m_old - m_new) * acc_scr[hf, :, cols]
                                    + _dot(vat_ref[2 * pair + hf, :, c0:c0 + c_len], pt))
            m_scr[hf, :, cols] = m_new

    def finish(tile):
        pair, qt = tile
        rows, cols = slice(qt * tq, (qt + 1) * tq), slice(pair * LANES, (pair + 1) * LANES)
        outs = []
        for hf in range(2):
            acc = acc_scr[hf]
            outs.append(acc / acc[AUG_BASE[hf]:AUG_BASE[hf] + 1, :])
        o = jnp.where(row_o < D_HB, outs[0], outs[1]).T
        o = o * _sigmoid(g_ref[0, rows, cols])
        o_ref[0, rows, cols] = o.astype(BF16)

    s_next = scores(items[0])
    for i, item in enumerate(items):
        s_cur = s_next
        if i + 1 < len(items):
            s_next = scores(items[i + 1])
        if i == 0 or items[i - 1][0] != item[0]:
            m_scr[...] = jnp.full(m_scr.shape, -jnp.inf, F32)
            acc_scr[...] = jnp.zeros(acc_scr.shape, F32)
        update(item, s_cur)
        if i + 1 == len(items) or items[i + 1][0] != item[0]:
            finish(item[0])


def _attn_prompt(qa, ka, vat, g):
    b, t, hd = g.shape
    tq = min(t, 512)
    pairs = 2
    n_blk = hd // (pairs * LANES)
    heads = pl.BlockSpec((2 * pairs, t, LANES), lambda i, j: (j, i, 0))
    heads_t = pl.BlockSpec((2 * pairs, LANES, t), lambda i, j: (j, 0, i))
    col = pl.BlockSpec((1, t, pairs * LANES), lambda i, j: (i, 0, j))
    kern = functools.partial(_attn_prompt_kernel, tq=tq)
    return pl.pallas_call(
        kern,
        grid=(b, n_blk),
        in_specs=[heads, heads, heads_t, col],
        out_specs=col,
        out_shape=jax.ShapeDtypeStruct((b, t, hd), BF16),
        scratch_shapes=[pltpu.VMEM((2, 1, tq), F32), pltpu.VMEM((2, LANES, tq), F32)],
        compiler_params=pltpu.CompilerParams(
            dimension_semantics=("arbitrary", "arbitrary"), vmem_limit_bytes=VMEM_LIMIT),
        name="fox_attn_prompt",
    )(qa, ka, vat, g)


def _pair_block(qs, k2, v2, fq, fk, mask, m_ref, l_ref, acc_ref, keys_on_lanes=False):
    tq = qs.shape[0] // 2
    tk = k2.shape[1] if keys_on_lanes else k2.shape[0]
    width = min(tk, LANES)
    s = _dot(qs, k2) if keys_on_lanes else _dot_nt(qs, k2)
    alphas, probs = [], []
    for hf in range(2):
        sh = s[hf * tq:(hf + 1) * tq] + (fq[hf] - fk[hf])
        if mask is not None:
            sh = jnp.where(mask, sh, -jnp.inf)
        cols = [sh[:, c * width:(c + 1) * width] for c in range(tk // width)]
        m_old = m_ref[hf]
        m_new = jnp.maximum(
            m_old, jnp.max(functools.reduce(jnp.maximum, cols), axis=-1, keepdims=True))
        p_cols = [jnp.exp(c - m_new[:, :width]) for c in cols]
        alpha = jnp.exp(m_old - m_new)
        l_ref[hf] = alpha * l_ref[hf] + jnp.sum(
            functools.reduce(lambda a, b: a + b, p_cols), axis=-1, keepdims=True)
        m_ref[hf] = m_new
        probs.append(jnp.concatenate([p.astype(BF16) for p in p_cols], axis=1))
        alphas.append(alpha)
    p_all = jnp.concatenate(probs, axis=0)
    pv = _dot_nt(p_all, v2) if keys_on_lanes else _dot(p_all, v2)
    acc_ref[...] = jnp.concatenate(alphas, axis=0) * acc_ref[...] + pv


def _split_heads(q2, lane_a):
    zero = jnp.zeros_like(q2)
    return jnp.concatenate([jnp.where(lane_a, q2, zero), jnp.where(lane_a, zero, q2)], axis=0)


def _attn_sample_kernel(q_ref, g_ref, kc_ref, vc_ref, kn_ref, vn_ref, fq_ref, fkc_ref, fkn_ref,
                        o_ref, m_scr, l_scr, acc_scr, *, n_pairs):
    kt = pl.program_id(1)
    n_kt = pl.num_programs(1)
    tq = q_ref.shape[1]
    scale = D_HB ** -0.5
    lane_a = lax.broadcasted_iota(jnp.int32, (tq, LANES), 1) < D_HB
    causal = (lax.broadcasted_iota(jnp.int32, (tq, tq), 1)
              <= lax.broadcasted_iota(jnp.int32, (tq, tq), 0))
    f_tile = fq_ref[0]

    @pl.when(kt == 0)
    def _():
        m_scr[...] = jnp.full(m_scr.shape, -jnp.inf, F32)
        l_scr[...] = jnp.zeros(l_scr.shape, F32)
        acc_scr[...] = jnp.zeros(acc_scr.shape, F32)

    def cache_pair(ref, hp):
        return ref[0, 2 * hp:2 * hp + 2].reshape(2 * D_HB, ref.shape[3]).astype(BF16)

    def pair_inputs(hp):
        cols = slice(hp * LANES, (hp + 1) * LANES)
        qs = _split_heads(q_ref[0, :, cols] * scale, lane_a).astype(BF16)
        fq = [_head_column(f_tile, 2 * hp + hf) for hf in range(2)]
        return cols, qs, fq

    for hp in range(n_pairs):
        cols, qs, fq = pair_inputs(hp)
        fk = [fkc_ref[0, 2 * hp + hf:2 * hp + hf + 1, :] for hf in range(2)]
        _pair_block(qs, cache_pair(kc_ref, hp), cache_pair(vc_ref, hp),
                    fq, fk, None, m_scr.at[hp], l_scr.at[hp], acc_scr.at[hp], keys_on_lanes=True)

    @pl.when(kt == n_kt - 1)
    def _():
        for hp in range(n_pairs):
            cols, qs, fq = pair_inputs(hp)
            fk = [fkn_ref[0, 2 * hp + hf:2 * hp + hf + 1, 0:tq] for hf in range(2)]
            _pair_block(qs, kn_ref[0, :, cols].astype(BF16), vn_ref[0, :, cols].astype(BF16),
                        fq, fk, causal, m_scr.at[hp], l_scr.at[hp], acc_scr.at[hp])
            acc = acc_scr[hp]
            o = jnp.where(lane_a, acc[:tq] / l_scr[hp, 0], acc[tq:] / l_scr[hp, 1])
            o = o * _sigmoid(g_ref[0, :, cols])
            o_ref[0, :, cols] = o.astype(BF16)


def _attn_sample(q, g, k_cache, v_cache, k_new, v_new, f, f_t, past):
    b, t, hd = q.shape
    n_pairs = hd // LANES
    tk = min(past, 1024)
    n_kt = past // tk
    row = pl.BlockSpec((1, t, hd), lambda i, j: (i, 0, 0))
    cache = pl.BlockSpec((1, H_B, D_HB, tk), lambda i, j: (i, 0, 0, j))
    kern = functools.partial(_attn_sample_kernel, n_pairs=n_pairs)
    return pl.pallas_call(
        kern,
        grid=(b, n_kt),
        in_specs=[row, row, cache, cache, row, row,
                  pl.BlockSpec((1, t, H_B), lambda i, j: (i, past // t, 0)),
                  pl.BlockSpec((1, H_B, tk), lambda i, j: (i, 0, j)),
                  pl.BlockSpec((1, H_B, LANES), lambda i, j: (i, 0, past // LANES))],
        out_specs=row,
        out_shape=jax.ShapeDtypeStruct((b, t, hd), BF16),
        scratch_shapes=[pltpu.VMEM((n_pairs, 2, t, LANES), F32), pltpu.VMEM((n_pairs, 2, t, LANES), F32),
                        pltpu.VMEM((n_pairs, 2 * t, LANES), F32)],
        compiler_params=pltpu.CompilerParams(
            dimension_semantics=("arbitrary", "arbitrary"), vmem_limit_bytes=VMEM_LIMIT),
        name="fox_attn_sample",
    )(q, g, k_cache, v_cache, k_new, v_new, f, f_t, f_t)


def _trunk(x, state0, k_past, v_past, logf_past, w):
    batch, seq, d = x.shape
    ntok = batch * seq
    hd = H_B * D_HB
    x2d = x.reshape(ntok, d)
    s0 = None if state0 is None else state0[:, 0]
    o_a, state = _hgrn(x2d, batch, seq, w["norm_a"], w["w_in"], w["lb_logits_h"], w["g_norm_a"], s0)
    x2d = _post(x2d, o_a, w["w_o_a"], w["norm_mlp0"], w["w_up0"], w["w_down0"])
    proj_w = (w["norm_kv"], w["norm_b"], w["w_k"], w["w_v"], w["w_f"], w["b_f"], w["w_qq"], w["w_qg"])
    if k_past is None:
        k_t, v_t, logf_t, g, qa, ka, va = _proj(x2d, batch, seq, True, *proj_w)
        k, v = jnp.transpose(k_t, (0, 3, 1, 2)), jnp.transpose(v_t, (0, 3, 1, 2))
        logf = jnp.swapaxes(logf_t, 1, 2)
        o_b = _attn_prompt(qa, ka, va, g.reshape(batch, seq, hd))
    else:
        k, v, logf, g, q, k_flat, v_flat = _proj(x2d, batch, seq, False, *proj_w)
        past = k_past.shape[1]
        total = past + seq
        padded = -(-total // LANES) * LANES
        lf_all = jnp.concatenate(
            [jnp.swapaxes(logf_past.astype(F32), 1, 2),
             jnp.swapaxes(logf.reshape(batch, seq, H_B), 1, 2),
             jnp.zeros((batch, H_B, padded - total), F32)], axis=2)
        f_t = _cumsum_time(lf_all.reshape(1, batch * H_B, padded)).reshape(batch, H_B, padded)
        o_b = _attn_sample(q.reshape(batch, seq, hd), g.reshape(batch, seq, hd),
                           jnp.transpose(k_past, (0, 2, 3, 1)).astype(F32),
                           jnp.transpose(v_past, (0, 2, 3, 1)).astype(F32),
                           k_flat.reshape(batch, seq, hd), v_flat.reshape(batch, seq, hd),
                           jnp.swapaxes(f_t, 1, 2), f_t, past)
    y = _post(x2d, o_b.reshape(ntok, hd), w["w_o_b"], w["norm_mlp1"], w["w_up1"], w["w_down1"],
              w["norm_f"])
    return (y.reshape(batch, seq, d), state[:, None],
            k.reshape(batch, seq, H_B, D_HB), v.reshape(batch, seq, H_B, D_HB),
            logf.reshape(batch, seq, H_B))


def kernel(x_prompt, x_sample, state_hgrn, cache_k, cache_v, cache_logf, norm_a, w_in_a, lb_logits, g_norm_a, w_o_a, norm_kv, w_kv, b_f, norm_b, w_q_b, w_o_b, norm_mlp, w_up, w_down, norm_f):
    d = x_prompt.shape[-1]
    assert w_in_a.shape[0] == 1 and w_q_b.shape[0] == 1, "one HGRN2 layer and one FoX layer"
    n_heads = w_in_a.shape[2] // (4 * K_A)
    hd = H_B * D_HB
    row = lambda a: a.reshape(1, -1).astype(F32)
    w_f = jnp.zeros((d, LANES), F32).at[:, :H_B].set(w_kv[:, 2 * hd:])
    w = {
        "norm_a": row(norm_a[0]),
        "w_in": w_in_a[0].astype(BF16),
        "lb_logits_h": lb_logits.astype(F32).reshape(lb_logits.shape[0], n_heads, 1, K_A),
        "g_norm_a": row(g_norm_a[0]),
        "w_o_a": w_o_a[0].astype(BF16),
        "norm_mlp0": row(norm_mlp[0]), "w_up0": w_up[0].astype(BF16), "w_down0": w_down[0].astype(BF16),
        "norm_mlp1": row(norm_mlp[1]), "w_up1": w_up[1].astype(BF16), "w_down1": w_down[1].astype(BF16),
        "norm_kv": row(norm_kv), "norm_b": row(norm_b[0]),
        "w_k": w_kv[:, :hd].astype(BF16), "w_v": w_kv[:, hd:2 * hd].astype(BF16),
        "w_f": w_f.astype(BF16), "b_f": jnp.zeros((1, LANES), F32).at[0, :H_B].set(b_f.astype(F32)),
        "w_qq": w_q_b[0][:, :hd].astype(BF16), "w_qg": w_q_b[0][:, hd:].astype(BF16),
        "w_o_b": w_o_b[0].astype(BF16),
        "norm_f": row(norm_f),
    }
    y_p, st_p, k_p, v_p, lf_p = _trunk(x_prompt, None, None, None, None, w)
    y_s, st_s, k_s, v_s, lf_s = _trunk(x_sample, state_hgrn, cache_k, cache_v, cache_logf, w)
    return (y_p, y_s, st_p, k_p, v_p, lf_p, st_s, k_s, v_s, lf_s)
```

```python
import functools

import jax
import jax.numpy as jnp
from jax import lax
from jax.experimental import pallas as pl
from jax.experimental.pallas import tpu as pltpu

EPS = 1e-6
CHUNK = 64
K_A = 128
V_A = 128
H_B = 16
D_HB = 64
LANES = 128
VMEM_LIMIT = 56 * 1024 * 1024

F32 = jnp.float32
BF16 = jnp.bfloat16


def _dot(a, b):
    return jnp.dot(a, b, preferred_element_type=F32)


def _dot_nt(a, b):
    return lax.dot_general(a, b, (((1,), (1,)), ((), ())), preferred_element_type=F32)


def _dot_tn(a, b):
    return lax.dot_general(a, b, (((0,), (0,)), ((), ())), preferred_element_type=F32)


def _sigmoid(x):
    return 0.5 * jnp.tanh(0.5 * x) + 0.5


def _const_spec(shape):
    nd = len(shape)
    return pl.BlockSpec(shape, lambda *_: (0,) * nd, pipeline_mode=pl.Buffered(1))


def _hgrn_kernel(*refs, n_heads, n_chunks, carry):
    if carry:
        (x_ref, nrm_ref, win_ref, lbl_ref, gn_ref, o_ref, sout_ref, xn_scr, st_scr) = refs
        s0_ref = None
    else:
        (x_ref, nrm_ref, win_ref, lbl_ref, gn_ref, s0_ref, o_ref, sout_ref, xn_scr) = refs
        st_scr = None
    tm = n_chunks * CHUNK
    n_sub = x_ref.shape[0] // tm

    def normalise(sub):
        x = x_ref[sub * tm:(sub + 1) * tm, :]
        inv = lax.rsqrt(jnp.mean(x * x, axis=-1, keepdims=True) + EPS)
        xn_scr[sub] = (x * inv * nrm_ref[...]).astype(BF16)

    if carry:
        @pl.when(pl.program_id(1) == 0)
        def _():
            st_scr[...] = jnp.zeros_like(st_scr)

    row_in_chunk = lax.broadcasted_iota(jnp.int32, (tm, K_A), 0) % CHUNK
    tri = (lax.broadcasted_iota(jnp.int32, (CHUNK, CHUNK), 1)
           <= lax.broadcasted_iota(jnp.int32, (CHUNK, CHUNK), 0))

    d_a = n_heads * K_A

    def project(sub, pair):
        xn = xn_scr[sub]
        return [_dot(xn, win_ref[:, part * d_a + pair * 2 * K_A:part * d_a + (pair + 1) * 2 * K_A])
                for part in range(4)]

    def gates(h, p4):
        half = slice((h % 2) * K_A, (h % 2 + 1) * K_A)
        pq, pz, pi, pg = (p[:, half] for p in p4)
        n_rows = lbl_ref.shape[0]
        lrows = [lbl_ref[r, h] for r in range(n_rows)]
        lmax = functools.reduce(jnp.maximum, lrows)
        lexp = [jnp.exp(l - lmax) for l in lrows]
        lb = lexp[0] / functools.reduce(lambda a, b: a + b, lexp)

        q = pq * _sigmoid(pq)
        f = lb + (1.0 - lb) * _sigmoid(pz)
        logf = jnp.log(f)
        k = 1.0 - f
        bc = logf
        s = 1
        while s < CHUNK:
            bc = bc + jnp.where(row_in_chunk >= s, pltpu.roll(bc, s, axis=0), 0.0)
            s *= 2
        chunked = lambda a: a.reshape(n_chunks, CHUNK, K_A)
        bc3 = chunked(bc)
        btot = bc3[:, CHUNK - 1:CHUNK, :]
        q_dec = chunked(q * jnp.exp(bc)).astype(BF16)
        k_inv = chunked(k * jnp.exp(-bc)).astype(BF16)
        k_end = (chunked(k) * jnp.exp(btot - bc3)).astype(BF16)
        decay = jnp.exp(btot)
        v = chunked(pi).astype(BF16)
        return q_dec, k_inv, k_end, decay, v, _sigmoid(pg)

    def recur(sub, heads, staged_heads):
        rows = slice(sub * tm, (sub + 1) * tm)
        cat = lambda i: jnp.concatenate([s[i] for s in staged_heads], axis=0)
        q_dec, k_inv, k_end, v = cat(0), cat(1), cat(2), cat(4)
        sc = jnp.einsum('cqk,csk->cqs', q_dec, k_inv, preferred_element_type=F32)
        sc = jnp.where(tri, sc, 0.0).astype(BF16)
        o_intra = jnp.einsum('cqs,csv->cqv', sc, v, preferred_element_type=F32)
        upd = jnp.einsum('csv,csk->cvk', v, k_end, preferred_element_type=F32)
        states = []
        for i, h in enumerate(heads):
            decay = staged_heads[i][3]
            if carry:
                st = st_scr[h]
                for c in range(n_chunks):
                    states.append(st)
                    st = st * decay[c] + upd[i * n_chunks + c]
                st_scr[h] = st
                sout_ref[0, h] = st.T
            else:
                for c in range(n_chunks):
                    st = s0_ref[c, h].T
                    states.append(st)
                    sout_ref[c, h] = (st * decay[c] + upd[i * n_chunks + c]).T
        s_in = jnp.stack(states, axis=0).astype(BF16)
        o_inter = jnp.einsum('cqk,cvk->cqv', q_dec, s_in, preferred_element_type=F32)
        o_all = o_intra + o_inter
        for i, h in enumerate(heads):
            o = o_all[i * n_chunks:(i + 1) * n_chunks].reshape(tm, V_A)
            o = o * lax.rsqrt(jnp.mean(o * o, axis=-1, keepdims=True) + EPS) * gn_ref[...]
            o_ref[rows, h * V_A:(h + 1) * V_A] = (o * staged_heads[i][5]).astype(BF16)

    items = [(sub, pair) for sub in range(n_sub) for pair in range(n_heads // 2)]
    normalise(0)
    p_next = project(*items[0])
    staged = None
    for i, (sub, pair) in enumerate(items):
        p_cur = p_next
        if i + 1 < len(items):
            if items[i + 1][0] != sub:
                normalise(items[i + 1][0])
            p_next = project(*items[i + 1])
        if staged is not None:
            recur(*staged)
        staged = (sub, (2 * pair, 2 * pair + 1), [gates(2 * pair, p_cur), gates(2 * pair + 1, p_cur)])
    recur(*staged)


def _hgrn(x2d, batch, seq, norm_w, w_in, lb_logits_h, g_norm, s0):
    ntok, d = x2d.shape
    n_heads = w_in.shape[1] // (4 * K_A)
    carry = s0 is None
    if carry:
        sub = min(seq, 512)
        tm = min(seq, 2 * sub)
        n_t = seq // tm
        grid = (batch, n_t)
        tok_map = lambda b, t: (b * n_t + t, 0)
        st_spec = pl.BlockSpec((1, n_heads, K_A, V_A), lambda b, t: (b, 0, 0, 0))
        scratch = [pltpu.VMEM((tm // sub, sub, d), BF16), pltpu.VMEM((n_heads, V_A, K_A), F32)]
        extra_in, extra_specs = [], []
    else:
        assert seq == CHUNK
        bt = min(batch, 8)
        sub = tm = bt * CHUNK
        grid = (batch // bt, 1)
        tok_map = lambda b, t: (b, 0)
        st_spec = pl.BlockSpec((bt, n_heads, K_A, V_A), lambda b, t: (b, 0, 0, 0))
        scratch = [pltpu.VMEM((1, tm, d), BF16)]
        extra_in, extra_specs = [s0], [st_spec]
    n_chunks = sub // CHUNK
    kern = functools.partial(_hgrn_kernel, n_heads=n_heads, n_chunks=n_chunks, carry=carry)
    return pl.pallas_call(
        kern,
        grid=grid,
        in_specs=[pl.BlockSpec((tm, d), tok_map),
                  _const_spec(norm_w.shape), _const_spec(w_in.shape),
                  _const_spec(lb_logits_h.shape), _const_spec(g_norm.shape)] + extra_specs,
        out_specs=[pl.BlockSpec((tm, n_heads * V_A), tok_map), st_spec],
        out_shape=[jax.ShapeDtypeStruct((ntok, n_heads * V_A), BF16),
                   jax.ShapeDtypeStruct((batch, n_heads, K_A, V_A), F32)],
        scratch_shapes=scratch,
        compiler_params=pltpu.CompilerParams(
            dimension_semantics=("arbitrary", "arbitrary"), vmem_limit_bytes=VMEM_LIMIT),
        name="hgrn_mixer",
    )(x2d, norm_w, w_in, lb_logits_h, g_norm, *extra_in)


def _post_kernel(*refs, ff_block, final_norm):
    if final_norm:
        x_ref, o_ref, wo_ref, nm_ref, wup_ref, wdn_ref, nf_ref, y_ref = refs
    else:
        x_ref, o_ref, wo_ref, nm_ref, wup_ref, wdn_ref, y_ref = refs
    x1 = x_ref[...] + _dot(o_ref[...], wo_ref[...])
    inv = lax.rsqrt(jnp.mean(x1 * x1, axis=-1, keepdims=True) + EPS)
    xn = (x1 * inv * nm_ref[...]).astype(BF16)
    acc = x1
    d_ff = wup_ref.shape[1]
    for j in range(d_ff // ff_block):
        hcol = jnp.maximum(_dot(xn, wup_ref[:, j * ff_block:(j + 1) * ff_block]), 0.0)
        acc = acc + _dot((hcol * hcol).astype(BF16), wdn_ref[j * ff_block:(j + 1) * ff_block, :])
    if final_norm:
        inv = lax.rsqrt(jnp.mean(acc * acc, axis=-1, keepdims=True) + EPS)
        acc = acc * inv * nf_ref[...]
    y_ref[...] = acc


def _post(x2d, o2d, w_o, norm_mlp, w_up, w_down, norm_f=None):
    ntok, d = x2d.shape
    tm = min(ntok, 512)
    final_norm = norm_f is not None
    tok = lambda i: (i, 0)
    ins = [x2d, o2d, w_o, norm_mlp, w_up, w_down] + ([norm_f] if final_norm else [])
    specs = [pl.BlockSpec((tm, d), tok), pl.BlockSpec((tm, o2d.shape[1]), tok)]
    specs += [_const_spec(a.shape) for a in ins[2:]]
    kern = functools.partial(_post_kernel, ff_block=1024, final_norm=final_norm)
    return pl.pallas_call(
        kern,
        grid=(ntok // tm,),
        in_specs=specs,
        out_specs=pl.BlockSpec((tm, d), tok),
        out_shape=jax.ShapeDtypeStruct((ntok, d), F32),
        compiler_params=pltpu.CompilerParams(
            dimension_semantics=("arbitrary",), vmem_limit_bytes=VMEM_LIMIT),
        name="post_mlp",
    )(*ins)


LOG2E = 1.4426950408889634
AUG_BASE = (D_HB, 0)


def _split3(x):
    h1 = x.astype(BF16).astype(F32)
    r1 = x - h1
    h2 = r1.astype(BF16).astype(F32)
    h3 = (r1 - h2).astype(BF16).astype(F32)
    return h1, h2, h3


def _head_pitch(tm):
    return tm + 8 if (tm // 8) % 2 == 0 else tm


def _store_head_major(val, scr, out_ref):
    tm = val.shape[0]
    pitch = _head_pitch(tm)
    for h in range(H_B):
        pair = val[:, (h // 2) * LANES:(h // 2 + 1) * LANES]
        scr[h * pitch:h * pitch + tm, :] = pair if h % 2 == 0 else pltpu.roll(pair, D_HB, axis=1)

    for t in range(tm):
        for grp in range(H_B // 8):
            rows = scr[pl.ds(grp * 8 * pitch + t, 8, stride=pitch), :]
            out_ref[t, grp * 8:(grp + 1) * 8, :] = rows[:, :D_HB]


def _proj_kernel(*refs, aug):
    if aug:
        (x_ref, nkv_ref, nb_ref, wk_ref, wv_ref, wf_ref, bf_ref, wq_ref, wg_ref,
         k_ref, v_ref, lf_ref, g_ref, qa_ref, ka_ref, va_ref, carry) = refs
    else:
        (x_ref, nkv_ref, nb_ref, wk_ref, wv_ref, wf_ref, bf_ref, wq_ref, wg_ref,
         k_ref, v_ref, lf_ref, g_ref, q_ref, kf_ref, vf_ref, k_scr, v_scr) = refs
    tm = x_ref.shape[0]
    x = x_ref[...]
    xs = x * lax.rsqrt(jnp.mean(x * x, axis=-1, keepdims=True) + EPS)
    xk = (xs * nkv_ref[...]).astype(BF16)
    xq = (xs * nb_ref[...]).astype(BF16)
    z = _dot(xk, wf_ref[...]) + bf_ref[...]
    lf = jnp.minimum(z, 0.0) - jnp.log(1.0 + jnp.exp(-jnp.abs(z)))
    if not aug:
        k = _dot(xk, wk_ref[...])
        v = _dot(xk, wv_ref[...])
        g_ref[...] = _dot(xq, wg_ref[...])
        _store_head_major(k, k_scr, k_ref)
        _store_head_major(v, v_scr, v_ref)
        lf_ref[...] = lf[:, :H_B]
        q_ref[...] = _dot(xq, wq_ref[...])
        kf_ref[...] = k
        vf_ref[...] = v
        return

    lf_ref[0] = lf.T[:H_B, :]

    @pl.when(pl.program_id(1) == 0)
    def _():
        carry[...] = jnp.zeros_like(carry)

    row = lax.broadcasted_iota(jnp.int32, (tm, LANES), 0)
    lane = lax.broadcasted_iota(jnp.int32, (tm, LANES), 1)
    f = lf
    s = 1
    while s < tm:
        f = f + jnp.where(row >= s, pltpu.roll(f, s, axis=0), 0.0)
        s *= 2
    f = f + carry[...]
    carry[...] = f[tm - 1:tm, :]
    h1, h2, h3 = _split3(f * LOG2E)
    pieces = jnp.where(lane < H_B, h1,
                       jnp.where(lane < 2 * H_B, pltpu.roll(h2, H_B, axis=1),
                                 jnp.where(lane < 3 * H_B, pltpu.roll(h3, 2 * H_B, axis=1), 0.0)))
    both = pieces + pltpu.roll(pieces, D_HB, axis=1)
    shifted = pltpu.roll(both, 1, axis=1)
    neg_both = -both
    low = lane < D_HB

    def tails(h):
        base = AUG_BASE[h % 2]
        at = (lane == base + h) | (lane == base + H_B + h) | (lane == base + 2 * H_B + h)
        up = (lane == base + h + 1) | (lane == base + H_B + h + 1) | (lane == base + 2 * H_B + h + 1)
        q_tail = jnp.where(up, shifted, jnp.where(at, 1.0, 0.0))
        k_tail = jnp.where(at, neg_both, jnp.where(up, 1.0, 0.0))
        return q_tail, k_tail

    def put(ref, h, data, tail):
        ref[h] = (jnp.where(low, data, tail) if h % 2 == 0 else jnp.where(low, tail, data)).astype(BF16)

    k = _dot(xk, wk_ref[...])
    k_ref[0] = k.T.reshape(H_B, D_HB, tm)
    for h in range(H_B):
        put(ka_ref, h, k[:, (h // 2) * LANES:(h // 2 + 1) * LANES], tails(h)[1])

    v_t = _dot(xk, wv_ref[...]).T
    v_ref[0] = v_t.reshape(H_B, D_HB, tm)
    row_t = lax.broadcasted_iota(jnp.int32, (LANES, tm), 0)
    for j in range(H_B // 2):
        v_pair = v_t[j * LANES:(j + 1) * LANES, :]
        va_ref[2 * j] = jnp.where(row_t < D_HB, v_pair,
                                  jnp.where(row_t == AUG_BASE[0], 1.0, 0.0)).astype(BF16)
        va_ref[2 * j + 1] = jnp.where(row_t < D_HB, jnp.where(row_t == AUG_BASE[1], 1.0, 0.0),
                                      v_pair).astype(BF16)

    q = _dot(xq, wq_ref[...]) * ((D_HB ** -0.5) * LOG2E)
    for h in range(H_B):
        put(qa_ref, h, q[:, (h // 2) * LANES:(h // 2 + 1) * LANES], tails(h)[0])

    g_ref[...] = _dot(xq, wg_ref[...])


def _proj(x2d, batch, seq, aug, norm_kv, norm_b, w_k, w_v, w_f, b_f, w_qq, w_qg):
    ntok, d = x2d.shape
    hd = w_k.shape[1]
    ins = [x2d, norm_kv, norm_b, w_k, w_v, w_f, b_f, w_qq, w_qg]
    if aug:
        tm = min(seq, 512)
        n_t = seq // tm
        grid = (batch, n_t)
        tok = lambda b, t: (b * n_t + t, 0)
        head_major = lambda b, t: (0, b * n_t + t, 0)
        time_minor = pl.BlockSpec((1, H_B, D_HB, tm), lambda b, t: (b, 0, 0, t))
        out_specs = [time_minor, time_minor, pl.BlockSpec((1, H_B, tm), lambda b, t: (b, 0, t)),
                     pl.BlockSpec((tm, hd), tok),
                     pl.BlockSpec((H_B, tm, LANES), head_major),
                     pl.BlockSpec((H_B, tm, LANES), head_major),
                     pl.BlockSpec((H_B, LANES, tm), lambda b, t: (0, 0, b * n_t + t))]
        out_shape = ([jax.ShapeDtypeStruct((batch, H_B, D_HB, seq), F32)] * 2
                     + [jax.ShapeDtypeStruct((batch, H_B, seq), F32),
                        jax.ShapeDtypeStruct((ntok, hd), F32),
                        jax.ShapeDtypeStruct((H_B, ntok, LANES), BF16),
                        jax.ShapeDtypeStruct((H_B, ntok, LANES), BF16),
                        jax.ShapeDtypeStruct((H_B, LANES, ntok), BF16)])
        scratch = [pltpu.VMEM((1, LANES), F32)]
    else:
        tm = min(ntok, 512)
        n_t = ntok // tm
        grid = (1, n_t)
        tok = lambda b, t: (t, 0)
        tok4 = lambda b, t: (t, 0, 0)
        out_specs = ([pl.BlockSpec((tm, H_B, D_HB), tok4)] * 2 + [pl.BlockSpec((tm, H_B), tok)]
                     + [pl.BlockSpec((tm, hd), tok)] * 4)
        out_shape = ([jax.ShapeDtypeStruct((ntok, H_B, D_HB), F32)] * 2
                     + [jax.ShapeDtypeStruct((ntok, H_B), F32)]
                     + [jax.ShapeDtypeStruct((ntok, hd), F32)] * 4)
        scratch = [pltpu.VMEM((H_B * _head_pitch(tm), LANES), F32)] * 2
    return pl.pallas_call(
        functools.partial(_proj_kernel, aug=aug),
        grid=grid,
        in_specs=[pl.BlockSpec((tm, d), tok)] + [_const_spec(a.shape) for a in ins[1:]],
        out_specs=out_specs,
        out_shape=out_shape,
        scratch_shapes=scratch,
        compiler_params=pltpu.CompilerParams(
            dimension_semantics=("arbitrary", "arbitrary"), vmem_limit_bytes=VMEM_LIMIT),
        name="kvq_proj",
    )(*ins)


def _cumsum_kernel(lf_ref, f_ref):
    n_blk = lf_ref.shape[2] // LANES
    upper = (lax.broadcasted_iota(jnp.int32, (LANES, LANES), 0)
             <= lax.broadcasted_iota(jnp.int32, (LANES, LANES), 1)).astype(BF16)
    run = jnp.zeros((lf_ref.shape[1], 1), F32)
    for c in range(n_blk):
        x = lf_ref[0, :, c * LANES:(c + 1) * LANES]
        h1 = x.astype(BF16)
        r1 = x - h1.astype(F32)
        h2 = r1.astype(BF16)
        h3 = (r1 - h2.astype(F32)).astype(BF16)
        cs = (_dot(h1, upper) + _dot(h2, upper)) + _dot(h3, upper) + run
        f_ref[0, :, c * LANES:(c + 1) * LANES] = cs
        run = cs[:, LANES - 1:LANES]


def _cumsum_time(lf_t):
    b, h, l = lf_t.shape
    return pl.pallas_call(
        _cumsum_kernel,
        grid=(b,),
        in_specs=[pl.BlockSpec((1, h, l), lambda i: (i, 0, 0))],
        out_specs=pl.BlockSpec((1, h, l), lambda i: (i, 0, 0)),
        out_shape=jax.ShapeDtypeStruct((b, h, l), F32),
        compiler_params=pltpu.CompilerParams(dimension_semantics=("arbitrary",)),
        name="logf_cumsum",
    )(lf_t)


def _head_column(f_tile, head):
    lane = lax.broadcasted_iota(jnp.int32, f_tile.shape, 1)
    return jnp.sum(jnp.where(lane == head, f_tile, 0.0), axis=-1, keepdims=True)


def _attn_prompt_kernel(qa_ref, ka_ref, vat_ref, g_ref, o_ref, m_scr, acc_scr, *, tq):
    seq = qa_ref.shape[1]
    n_q = seq // tq
    half = tq // 2
    row_o = lax.broadcasted_iota(jnp.int32, (LANES, tq), 0)

    def visible(keys, queries):
        return (lax.broadcasted_iota(jnp.int32, (keys, queries), 0)
                <= lax.broadcasted_iota(jnp.int32, (keys, queries), 1))

    items = []
    for pair in range(qa_ref.shape[0] // 2):
        for qt in range(n_q):
            tile = (pair, qt)
            for kt in range(qt):
                items.append((tile, 0, tq, kt * tq, tq, None))
            items.append((tile, 0, tq, qt * tq, half, visible(half, tq)))
            items.append((tile, half, half, qt * tq + half, half, visible(half, half)))

    def scores(item):
        (pair, qt), q_lo, q_len, c0, c_len, _ = item
        return [_dot_nt(ka_ref[2 * pair + hf, c0:c0 + c_len, :],
                        qa_ref[2 * pair + hf, qt * tq + q_lo:qt * tq + q_lo + q_len, :])
                for hf in range(2)]

    def update(item, s_pair):
        (pair, _), q_lo, q_len, c0, c_len, mask = item
        cols = slice(q_lo, q_lo + q_len)
        for hf in range(2):
            st = s_pair[hf]
            if mask is not None:
                st = jnp.where(mask, st, -jnp.inf)
            m_old = m_scr[hf, :, cols]
            m_new = jnp.maximum(m_old, jnp.max(st, axis=0, keepdims=True))
            pt = jnp.exp2(st - m_new).astype(BF16)
            acc_scr[hf, :, cols] = (jnp.exp2(m_old - m_new) * acc_scr[hf, :, cols]
                                    + _dot(vat_ref[2 * pair + hf, :, c0:c0 + c_len], pt))
            m_scr[hf, :, cols] = m_new

    def finish(tile):
        pair, qt = tile
        rows, cols = slice(qt * tq, (qt + 1) * tq), slice(pair * LANES, (pair + 1) * LANES)
        outs = []
        for hf in range(2):
            acc = acc_scr[hf]
            outs.append(acc / acc[AUG_BASE[hf]:AUG_BASE[hf] + 1, :])
        o = jnp.where(row_o < D_HB, outs[0], outs[1]).T
        o = o * _sigmoid(g_ref[0, rows, cols])
        o_ref[0, rows, cols] = o.astype(BF16)

    s_next = scores(items[0])
    for i, item in enumerate(items):
        s_cur = s_next
        if i + 1 < len(items):
            s_next = scores(items[i + 1])
        if i == 0 or items[i - 1][0] != item[0]:
            m_scr[...] = jnp.full(m_scr.shape, -jnp.inf, F32)
            acc_scr[...] = jnp.zeros(acc_scr.shape, F32)
        update(item, s_cur)
        if i + 1 == len(items) or items[i + 1][0] != item[0]:
            finish(item[0])


def _attn_prompt(qa, ka, vat, g):
    b, t, hd = g.shape
    tq = min(t, 512)
    pairs = 2
    n_blk = hd // (pairs * LANES)
    heads = pl.BlockSpec((2 * pairs, t, LANES), lambda i, j: (j, i, 0))
    heads_t = pl.BlockSpec((2 * pairs, LANES, t), lambda i, j: (j, 0, i))
    col = pl.BlockSpec((1, t, pairs * LANES), lambda i, j: (i, 0, j))
    kern = functools.partial(_attn_prompt_kernel, tq=tq)
    return pl.pallas_call(
        kern,
        grid=(b, n_blk),
        in_specs=[heads, heads, heads_t, col],
        out_specs=col,
        out_shape=jax.ShapeDtypeStruct((b, t, hd), BF16),
        scratch_shapes=[pltpu.VMEM((2, 1, tq), F32), pltpu.VMEM((2, LANES, tq), F32)],
        compiler_params=pltpu.CompilerParams(
            dimension_semantics=("arbitrary", "arbitrary"), vmem_limit_bytes=VMEM_LIMIT),
        name="fox_attn_prompt",
    )(qa, ka, vat, g)


def _pair_block(qs, k2, v2, fq, fk, mask, m_ref, l_ref, acc_ref, keys_on_lanes=False):
    tq = qs.shape[0] // 2
    tk = k2.shape[1] if keys_on_lanes else k2.shape[0]
    width = min(tk, LANES)
    s = _dot(qs, k2) if keys_on_lanes else _dot_nt(qs, k2)
    alphas, probs = [], []
    for hf in range(2):
        sh = s[hf * tq:(hf + 1) * tq] + (fq[hf] - fk[hf])
        if mask is not None:
            sh = jnp.where(mask, sh, -jnp.inf)
        cols = [sh[:, c * width:(c + 1) * width] for c in range(tk // width)]
        m_old = m_ref[hf]
        m_new = jnp.maximum(
            m_old, jnp.max(functools.reduce(jnp.maximum, cols), axis=-1, keepdims=True))
        p_cols = [jnp.exp(c - m_new[:, :width]) for c in cols]
        alpha = jnp.exp(m_old - m_new)
        l_ref[hf] = alpha * l_ref[hf] + jnp.sum(
            functools.reduce(lambda a, b: a + b, p_cols), axis=-1, keepdims=True)
        m_ref[hf] = m_new
        probs.append(jnp.concatenate([p.astype(BF16) for p in p_cols], axis=1))
        alphas.append(alpha)
    p_all = jnp.concatenate(probs, axis=0)
    pv = _dot_nt(p_all, v2) if keys_on_lanes else _dot(p_all, v2)
    acc_ref[...] = jnp.concatenate(alphas, axis=0) * acc_ref[...] + pv


def _split_heads(q2, lane_a):
    zero = jnp.zeros_like(q2)
    return jnp.concatenate([jnp.where(lane_a, q2, zero), jnp.where(lane_a, zero, q2)], axis=0)


def _attn_sample_kernel(q_ref, g_ref, kc_ref, vc_ref, kn_ref, vn_ref, fq_ref, fkc_ref, fkn_ref,
                        o_ref, m_scr, l_scr, acc_scr, *, n_pairs):
    kt = pl.program_id(1)
    n_kt = pl.num_programs(1)
    tq = q_ref.shape[1]
    scale = D_HB ** -0.5
    lane_a = lax.broadcasted_iota(jnp.int32, (tq, LANES), 1) < D_HB
    causal = (lax.broadcasted_iota(jnp.int32, (tq, tq), 1)
              <= lax.broadcasted_iota(jnp.int32, (tq, tq), 0))
    f_tile = fq_ref[0]

    @pl.when(kt == 0)
    def _():
        m_scr[...] = jnp.full(m_scr.shape, -jnp.inf, F32)
        l_scr[...] = jnp.zeros(l_scr.shape, F32)
        acc_scr[...] = jnp.zeros(acc_scr.shape, F32)

    def cache_pair(ref, hp):
        return ref[0, 2 * hp:2 * hp + 2].reshape(2 * D_HB, ref.shape[3]).astype(BF16)

    def pair_inputs(hp):
        cols = slice(hp * LANES, (hp + 1) * LANES)
        qs = _split_heads(q_ref[0, :, cols] * scale, lane_a).astype(BF16)
        fq = [_head_column(f_tile, 2 * hp + hf) for hf in range(2)]
        return cols, qs, fq

    for hp in range(n_pairs):
        cols, qs, fq = pair_inputs(hp)
        fk = [fkc_ref[0, 2 * hp + hf:2 * hp + hf + 1, :] for hf in range(2)]
        _pair_block(qs, cache_pair(kc_ref, hp), cache_pair(vc_ref, hp),
                    fq, fk, None, m_scr.at[hp], l_scr.at[hp], acc_scr.at[hp], keys_on_lanes=True)

    @pl.when(kt == n_kt - 1)
    def _():
        for hp in range(n_pairs):
            cols, qs, fq = pair_inputs(hp)
            fk = [fkn_ref[0, 2 * hp + hf:2 * hp + hf + 1, 0:tq] for hf in range(2)]
            _pair_block(qs, kn_ref[0, :, cols].astype(BF16), vn_ref[0, :, cols].astype(BF16),
                        fq, fk, causal, m_scr.at[hp], l_scr.at[hp], acc_scr.at[hp])
            acc = acc_scr[hp]
            o = jnp.where(lane_a, acc[:tq] / l_scr[hp, 0], acc[tq:] / l_scr[hp, 1])
            o = o * _sigmoid(g_ref[0, :, cols])
            o_ref[0, :, cols] = o.astype(BF16)


def _attn_sample(q, g, k_cache, v_cache, k_new, v_new, f, f_t, past):
    b, t, hd = q.shape
    n_pairs = hd // LANES
    tk = min(past, 1024)
    n_kt = past // tk
    row = pl.BlockSpec((1, t, hd), lambda i, j: (i, 0, 0))
    cache = pl.BlockSpec((1, H_B, D_HB, tk), lambda i, j: (i, 0, 0, j))
    kern = functools.partial(_attn_sample_kernel, n_pairs=n_pairs)
    return pl.pallas_call(
        kern,
        grid=(b, n_kt),
        in_specs=[row, row, cache, cache, row, row,
                  pl.BlockSpec((1, t, H_B), lambda i, j: (i, past // t, 0)),
                  pl.BlockSpec((1, H_B, tk), lambda i, j: (i, 0, j)),
                  pl.BlockSpec((1, H_B, LANES), lambda i, j: (i, 0, past // LANES))],
        out_specs=row,
        out_shape=jax.ShapeDtypeStruct((b, t, hd), BF16),
        scratch_shapes=[pltpu.VMEM((n_pairs, 2, t, LANES), F32), pltpu.VMEM((n_pairs, 2, t, LANES), F32),
                        pltpu.VMEM((n_pairs, 2 * t, LANES), F32)],
        compiler_params=pltpu.CompilerParams(
            dimension_semantics=("arbitrary", "arbitrary"), vmem_limit_bytes=VMEM_LIMIT),
        name="fox_attn_sample",
    )(q, g, k_cache, v_cache, k_new, v_new, f, f_t, f_t)


def _trunk(x, state0, k_past, v_past, logf_past, w):
    batch, seq, d = x.shape
    ntok = batch * seq
    hd = H_B * D_HB
    x2d = x.reshape(ntok, d)
    s0 = None if state0 is None else state0[:, 0]
    o_a, state = _hgrn(x2d, batch, seq, w["norm_a"], w["w_in"], w["lb_logits_h"], w["g_norm_a"], s0)
    x2d = _post(x2d, o_a, w["w_o_a"], w["norm_mlp0"], w["w_up0"], w["w_down0"])
    proj_w = (w["norm_kv"], w["norm_b"], w["w_k"], w["w_v"], w["w_f"], w["b_f"], w["w_qq"], w["w_qg"])
    if k_past is None:
        k_t, v_t, logf_t, g, qa, ka, va = _proj(x2d, batch, seq, True, *proj_w)
        k, v = jnp.transpose(k_t, (0, 3, 1, 2)), jnp.transpose(v_t, (0, 3, 1, 2))
        logf = jnp.swapaxes(logf_t, 1, 2)
        o_b = _attn_prompt(qa, ka, va, g.reshape(batch, seq, hd))
    else:
        k, v, logf, g, q, k_flat, v_flat = _proj(x2d, batch, seq, False, *proj_w)
        past = k_past.shape[1]
        total = past + seq
        padded = -(-total // LANES) * LANES
        lf_all = jnp.concatenate(
            [jnp.swapaxes(logf_past.astype(F32), 1, 2),
             jnp.swapaxes(logf.reshape(batch, seq, H_B), 1, 2),
             jnp.zeros((batch, H_B, padded - total), F32)], axis=2)
        f_t = _cumsum_time(lf_all.reshape(1, batch * H_B, padded)).reshape(batch, H_B, padded)
        o_b = _attn_sample(q.reshape(batch, seq, hd), g.reshape(batch, seq, hd),
                           jnp.transpose(k_past, (0, 2, 3, 1)).astype(F32),
                           jnp.transpose(v_past, (0, 2, 3, 1)).astype(F32),
                           k_flat.reshape(batch, seq, hd), v_flat.reshape(batch, seq, hd),
                           jnp.swapaxes(f_t, 1, 2), f_t, past)
    y = _post(x2d, o_b.reshape(ntok, hd), w["w_o_b"], w["norm_mlp1"], w["w_up1"], w["w_down1"],
              w["norm_f"])
    return (y.reshape(batch, seq, d), state[:, None],
            k.reshape(batch, seq, H_B, D_HB), v.reshape(batch, seq, H_B, D_HB),
            logf.reshape(batch, seq, H_B))


def kernel(x_prompt, x_sample, state_hgrn, cache_k, cache_v, cache_logf, norm_a, w_in_a, lb_logits, g_norm_a, w_o_a, norm_kv, w_kv, b_f, norm_b, w_q_b, w_o_b, norm_mlp, w_up, w_down, norm_f):
    d = x_prompt.shape[-1]
    assert w_in_a.shape[0] == 1 and w_q_b.shape[0] == 1, "one HGRN2 layer and one FoX layer"
    n_heads = w_in_a.shape[2] // (4 * K_A)
    hd = H_B * D_HB
    row = lambda a: a.reshape(1, -1).astype(F32)
    w_f = jnp.zeros((d, LANES), F32).at[:, :H_B].set(w_kv[:, 2 * hd:])
    w = {
        "norm_a": row(norm_a[0]),
        "w_in": w_in_a[0].astype(BF16),
        "lb_logits_h": lb_logits.astype(F32).reshape(lb_logits.shape[0], n_heads, 1, K_A),
        "g_norm_a": row(g_norm_a[0]),
        "w_o_a": w_o_a[0].astype(BF16),
        "norm_mlp0": row(norm_mlp[0]), "w_up0": w_up[0].astype(BF16), "w_down0": w_down[0].astype(BF16),
        "norm_mlp1": row(norm_mlp[1]), "w_up1": w_up[1].astype(BF16), "w_down1": w_down[1].astype(BF16),
        "norm_kv": row(norm_kv), "norm_b": row(norm_b[0]),
        "w_k": w_kv[:, :hd].astype(BF16), "w_v": w_kv[:, hd:2 * hd].astype(BF16),
        "w_f": w_f.astype(BF16), "b_f": jnp.zeros((1, LANES), F32).at[0, :H_B].set(b_f.astype(F32)),
        "w_qq": w_q_b[0][:, :hd].astype(BF16), "w_qg": w_q_b[0][:, hd:].astype(BF16),
        "w_o_b": w_o_b[0].astype(BF16),
        "norm_f": row(norm_f),
    }
    y_p, st_p, k_p, v_p, lf_p = _trunk(x_prompt, None, None, None, None, w)
    y_s, st_s, k_s, v_s, lf_s = _trunk(x_sample, state_hgrn, cache_k, cache_v, cache_logf, w)
    return (y_p, y_s, st_p, k_p, v_p, lf_p, st_s, k_s, v_s, lf_s)
```

```python
import functools

import jax
import jax.numpy as jnp
from jax import lax
from jax.experimental import pallas as pl
from jax.experimental.pallas import tpu as pltpu

EPS = 1e-6
CHUNK = 64
K_A = 128
V_A = 128
H_B = 16
D_HB = 64
LANES = 128
VMEM_LIMIT = 56 * 1024 * 1024

F32 = jnp.float32
BF16 = jnp.bfloat16


def _dot(a, b):
    return jnp.dot(a, b, preferred_element_type=F32)


def _dot_nt(a, b):
    return lax.dot_general(a, b, (((1,), (1,)), ((), ())), preferred_element_type=F32)


def _dot_tn(a, b):
    return lax.dot_general(a, b, (((0,), (0,)), ((), ())), preferred_element_type=F32)


def _sigmoid(x):
    return 0.5 * jnp.tanh(0.5 * x) + 0.5


def _const_spec(shape):
    nd = len(shape)
    return pl.BlockSpec(shape, lambda *_: (0,) * nd, pipeline_mode=pl.Buffered(1))


def _hgrn_kernel(*refs, n_heads, n_chunks, carry):
    if carry:
        (x_ref, nrm_ref, win_ref, lbl_ref, gn_ref, o_ref, sout_ref, xn_scr, st_scr) = refs
        s0_ref = None
    else:
        (x_ref, nrm_ref, win_ref, lbl_ref, gn_ref, s0_ref, o_ref, sout_ref, xn_scr) = refs
        st_scr = None
    tm = n_chunks * CHUNK
    n_sub = x_ref.shape[0] // tm

    def normalise(sub):
        x = x_ref[sub * tm:(sub + 1) * tm, :]
        inv = lax.rsqrt(jnp.mean(x * x, axis=-1, keepdims=True) + EPS)
        xn_scr[sub] = (x * inv * nrm_ref[...]).astype(BF16)

    if carry:
        @pl.when(pl.program_id(1) == 0)
        def _():
            st_scr[...] = jnp.zeros_like(st_scr)

    row_in_chunk = lax.broadcasted_iota(jnp.int32, (tm, K_A), 0) % CHUNK
    tri = (lax.broadcasted_iota(jnp.int32, (CHUNK, CHUNK), 1)
           <= lax.broadcasted_iota(jnp.int32, (CHUNK, CHUNK), 0))

    d_a = n_heads * K_A

    def project(sub, pair):
        xn = xn_scr[sub]
        return [_dot(xn, win_ref[:, part * d_a + pair * 2 * K_A:part * d_a + (pair + 1) * 2 * K_A])
                for part in range(4)]

    def gates(h, p4):
        half = slice((h % 2) * K_A, (h % 2 + 1) * K_A)
        pq, pz, pi, pg = (p[:, half] for p in p4)
        n_rows = lbl_ref.shape[0]
        lrows = [lbl_ref[r, h] for r in range(n_rows)]
        lmax = functools.reduce(jnp.maximum, lrows)
        lexp = [jnp.exp(l - lmax) for l in lrows]
        lb = lexp[0] / functools.reduce(lambda a, b: a + b, lexp)

        q = pq * _sigmoid(pq)
        f = lb + (1.0 - lb) * _sigmoid(pz)
        logf = jnp.log(f)
        k = 1.0 - f
        bc = logf
        s = 1
        while s < CHUNK:
            bc = bc + jnp.where(row_in_chunk >= s, pltpu.roll(bc, s, axis=0), 0.0)
            s *= 2
        chunked = lambda a: a.reshape(n_chunks, CHUNK, K_A)
        bc3 = chunked(bc)
        btot = bc3[:, CHUNK - 1:CHUNK, :]
        q_dec = chunked(q * jnp.exp(bc)).astype(BF16)
        k_inv = chunked(k * jnp.exp(-bc)).astype(BF16)
        k_end = (chunked(k) * jnp.exp(btot - bc3)).astype(BF16)
        decay = jnp.exp(btot)
        v = chunked(pi).astype(BF16)
        return q_dec, k_inv, k_end, decay, v, _sigmoid(pg)

    def recur(sub, heads, staged_heads):
        rows = slice(sub * tm, (sub + 1) * tm)
        cat = lambda i: jnp.concatenate([s[i] for s in staged_heads], axis=0)
        q_dec, k_inv, k_end, v = cat(0), cat(1), cat(2), cat(4)
        sc = jnp.einsum('cqk,csk->cqs', q_dec, k_inv, preferred_element_type=F32)
        sc = jnp.where(tri, sc, 0.0).astype(BF16)
        o_intra = jnp.einsum('cqs,csv->cqv', sc, v, preferred_element_type=F32)
        upd = jnp.einsum('csv,csk->cvk', v, k_end, preferred_element_type=F32)
        states = []
        for i, h in enumerate(heads):
            decay = staged_heads[i][3]
            if carry:
                st = st_scr[h]
                for c in range(n_chunks):
                    states.append(st)
                    st = st * decay[c] + upd[i * n_chunks + c]
                st_scr[h] = st
                sout_ref[0, h] = st.T
            else:
                for c in range(n_chunks):
                    st = s0_ref[c, h].T
                    states.append(st)
                    sout_ref[c, h] = (st * decay[c] + upd[i * n_chunks + c]).T
        s_in = jnp.stack(states, axis=0).astype(BF16)
        o_inter = jnp.einsum('cqk,cvk->cqv', q_dec, s_in, preferred_element_type=F32)
        o_all = o_intra + o_inter
        for i, h in enumerate(heads):
            o = o_all[i * n_chunks:(i + 1) * n_chunks].reshape(tm, V_A)
            o = o * lax.rsqrt(jnp.mean(o * o, axis=-1, keepdims=True) + EPS) * gn_ref[...]
            o_ref[rows, h * V_A:(h + 1) * V_A] = (o * staged_heads[i][5]).astype(BF16)

    items = [(sub, pair) for sub in range(n_sub) for pair in range(n_heads // 2)]
    normalise(0)
    p_next = project(*items[0])
    staged = None
    for i, (sub, pair) in enumerate(items):
        p_cur = p_next
        if i + 1 < len(items):
            if items[i + 1][0] != sub:
                normalise(items[i + 1][0])
            p_next = project(*items[i + 1])
        if staged is not None:
            recur(*staged)
        staged = (sub, (2 * pair, 2 * pair + 1), [gates(2 * pair, p_cur), gates(2 * pair + 1, p_cur)])
    recur(*staged)


def _hgrn(x2d, batch, seq, norm_w, w_in, lb_logits_h, g_norm, s0):
    ntok, d = x2d.shape
    n_heads = w_in.shape[1] // (4 * K_A)
    carry = s0 is None
    if carry:
        sub = min(seq, 512)
        tm = min(seq, 2 * sub)
        n_t = seq // tm
        grid = (batch, n_t)
        tok_map = lambda b, t: (b * n_t + t, 0)
        st_spec = pl.BlockSpec((1, n_heads, K_A, V_A), lambda b, t: (b, 0, 0, 0))
        scratch = [pltpu.VMEM((tm // sub, sub, d), BF16), pltpu.VMEM((n_heads, V_A, K_A), F32)]
        extra_in, extra_specs = [], []
    else:
        assert seq == CHUNK
        bt = min(batch, 8)
        sub = tm = bt * CHUNK
        grid = (batch // bt, 1)
        tok_map = lambda b, t: (b, 0)
        st_spec = pl.BlockSpec((bt, n_heads, K_A, V_A), lambda b, t: (b, 0, 0, 0))
        scratch = [pltpu.VMEM((1, tm, d), BF16)]
        extra_in, extra_specs = [s0], [st_spec]
    n_chunks = sub // CHUNK
    kern = functools.partial(_hgrn_kernel, n_heads=n_heads, n_chunks=n_chunks, carry=carry)
    return pl.pallas_call(
        kern,
        grid=grid,
        in_specs=[pl.BlockSpec((tm, d), tok_map),
                  _const_spec(norm_w.shape), _const_spec(w_in.shape),
                  _const_spec(lb_logits_h.shape), _const_spec(g_norm.shape)] + extra_specs,
        out_specs=[pl.BlockSpec((tm, n_heads * V_A), tok_map), st_spec],
        out_shape=[jax.ShapeDtypeStruct((ntok, n_heads * V_A), BF16),
                   jax.ShapeDtypeStruct((batch, n_heads, K_A, V_A), F32)],
        scratch_shapes=scratch,
        compiler_params=pltpu.CompilerParams(
            dimension_semantics=("arbitrary", "arbitrary"), vmem_limit_bytes=VMEM_LIMIT),
        name="hgrn_mixer",
    )(x2d, norm_w, w_in, lb_logits_h, g_norm, *extra_in)


def _post_kernel(*refs, ff_block, final_norm):
    if final_norm:
        x_ref, o_ref, wo_ref, nm_ref, wup_ref, wdn_ref, nf_ref, y_ref = refs
    else:
        x_ref, o_ref, wo_ref, nm_ref, wup_ref, wdn_ref, y_ref = refs
    x1 = x_ref[...] + _dot(o_ref[...], wo_ref[...])
    inv = lax.rsqrt(jnp.mean(x1 * x1, axis=-1, keepdims=True) + EPS)
    xn = (x1 * inv * nm_ref[...]).astype(BF16)
    acc = x1
    d_ff = wup_ref.shape[1]
    for j in range(d_ff // ff_block):
        hcol = jnp.maximum(_dot(xn, wup_ref[:, j * ff_block:(j + 1) * ff_block]), 0.0)
        acc = acc + _dot((hcol * hcol).astype(BF16), wdn_ref[j * ff_block:(j + 1) * ff_block, :])
    if final_norm:
        inv = lax.rsqrt(jnp.mean(acc * acc, axis=-1, keepdims=True) + EPS)
        acc = acc * inv * nf_ref[...]
    y_ref[...] = acc


def _layer_spec(stacked, layer):
    return pl.BlockSpec((None,) + stacked.shape[1:], lambda *_: (layer, 0, 0),
                        pipeline_mode=pl.Buffered(1))


def _post(x2d, o2d, w_o, norm_mlp, w_up, w_down, layer, norm_f=None):
    ntok, d = x2d.shape
    tm = min(ntok, 512)
    final_norm = norm_f is not None
    tok = lambda i: (i, 0)
    ins = [x2d, o2d, w_o, norm_mlp, w_up, w_down] + ([norm_f] if final_norm else [])
    specs = [pl.BlockSpec((tm, d), tok), pl.BlockSpec((tm, o2d.shape[1]), tok),
             _const_spec(w_o.shape), _const_spec(norm_mlp.shape),
             _layer_spec(w_up, layer), _layer_spec(w_down, layer)]
    specs += [_const_spec(norm_f.shape)] if final_norm else []
    kern = functools.partial(_post_kernel, ff_block=1024, final_norm=final_norm)
    return pl.pallas_call(
        kern,
        grid=(ntok // tm,),
        in_specs=specs,
        out_specs=pl.BlockSpec((tm, d), tok),
        out_shape=jax.ShapeDtypeStruct((ntok, d), F32),
        compiler_params=pltpu.CompilerParams(
            dimension_semantics=("arbitrary",), vmem_limit_bytes=VMEM_LIMIT),
        name="post_mlp",
    )(*ins)


LOG2E = 1.4426950408889634
AUG_BASE = (D_HB, 0)


def _split3(x):
    h1 = x.astype(BF16).astype(F32)
    r1 = x - h1
    h2 = r1.astype(BF16).astype(F32)
    h3 = (r1 - h2).astype(BF16).astype(F32)
    return h1, h2, h3


def _head_pitch(tm):
    return tm + 8 if (tm // 8) % 2 == 0 else tm


def _store_head_major(val, scr, out_ref):
    tm = val.shape[0]
    pitch = _head_pitch(tm)
    for h in range(H_B):
        pair = val[:, (h // 2) * LANES:(h // 2 + 1) * LANES]
        scr[h * pitch:h * pitch + tm, :] = pair if h % 2 == 0 else pltpu.roll(pair, D_HB, axis=1)

    for t in range(tm):
        for grp in range(H_B // 8):
            rows = scr[pl.ds(grp * 8 * pitch + t, 8, stride=pitch), :]
            out_ref[t, grp * 8:(grp + 1) * 8, :] = rows[:, :D_HB]


def _proj_kernel(*refs, aug, sub_tiles):
    if aug:
        (x_ref, nkv_ref, nb_ref, wk_ref, wv_ref, wf_ref, bf_ref, wq_ref, wg_ref,
         k_ref, v_ref, lf_ref, g_ref, qa_ref, ka_ref, va_ref, carry) = refs
    else:
        (x_ref, nkv_ref, nb_ref, wk_ref, wv_ref, wf_ref, bf_ref, wq_ref, wg_ref,
         k_ref, v_ref, lf_ref, g_ref, q_ref, kf_ref, vf_ref, k_scr, v_scr) = refs
    tm = x_ref.shape[0]
    if not aug:
        x = x_ref[...]
        xs = x * lax.rsqrt(jnp.mean(x * x, axis=-1, keepdims=True) + EPS)
        xk = (xs * nkv_ref[...]).astype(BF16)
        xq = (xs * nb_ref[...]).astype(BF16)
        z = _dot(xk, wf_ref[...]) + bf_ref[...]
        lf = jnp.minimum(z, 0.0) - jnp.log(1.0 + jnp.exp(-jnp.abs(z)))
        k = _dot(xk, wk_ref[...])
        v = _dot(xk, wv_ref[...])
        g_ref[...] = _dot(xq, wg_ref[...])
        _store_head_major(k, k_scr, k_ref)
        _store_head_major(v, v_scr, v_ref)
        lf_ref[...] = lf[:, :H_B]
        q_ref[...] = _dot(xq, wq_ref[...])
        kf_ref[...] = k
        vf_ref[...] = v
        return

    @pl.when(pl.program_id(1) == 0)
    def _():
        carry[...] = jnp.zeros_like(carry)

    ts = tm // sub_tiles
    row = lax.broadcasted_iota(jnp.int32, (ts, LANES), 0)
    lane = lax.broadcasted_iota(jnp.int32, (ts, LANES), 1)
    row_t = lax.broadcasted_iota(jnp.int32, (LANES, ts), 0)
    low = lane < D_HB

    def sub_tile(r0, f_in):
        rows = slice(r0, r0 + ts)
        x = x_ref[rows, :]
        xs = x * lax.rsqrt(jnp.mean(x * x, axis=-1, keepdims=True) + EPS)
        xk = (xs * nkv_ref[...]).astype(BF16)
        xq = (xs * nb_ref[...]).astype(BF16)
        z = _dot(xk, wf_ref[...]) + bf_ref[...]
        lf = jnp.minimum(z, 0.0) - jnp.log(1.0 + jnp.exp(-jnp.abs(z)))
        lf_ref[0, :, rows] = lf.T[:H_B, :]
        f = lf
        s = 1
        while s < ts:
            f = f + jnp.where(row >= s, pltpu.roll(f, s, axis=0), 0.0)
            s *= 2
        f = f + f_in
        h1, h2, h3 = _split3(f * LOG2E)
        pieces = jnp.where(lane < H_B, h1,
                           jnp.where(lane < 2 * H_B, pltpu.roll(h2, H_B, axis=1),
                                     jnp.where(lane < 3 * H_B, pltpu.roll(h3, 2 * H_B, axis=1), 0.0)))
        both = pieces + pltpu.roll(pieces, D_HB, axis=1)
        shifted = pltpu.roll(both, 1, axis=1)
        neg_both = -both

        def tails(h):
            base = AUG_BASE[h % 2]
            at = (lane == base + h) | (lane == base + H_B + h) | (lane == base + 2 * H_B + h)
            up = ((lane == base + h + 1) | (lane == base + H_B + h + 1)
                  | (lane == base + 2 * H_B + h + 1))
            q_tail = jnp.where(up, shifted, jnp.where(at, 1.0, 0.0))
            k_tail = jnp.where(at, neg_both, jnp.where(up, 1.0, 0.0))
            return q_tail, k_tail

        def put(ref, h, data, tail):
            ref[h, rows, :] = (jnp.where(low, data, tail) if h % 2 == 0
                               else jnp.where(low, tail, data)).astype(BF16)

        k = _dot(xk, wk_ref[...])
        k_ref[0, :, :, rows] = k.T.reshape(H_B, D_HB, ts)
        for h in range(H_B):
            put(ka_ref, h, k[:, (h // 2) * LANES:(h // 2 + 1) * LANES], tails(h)[1])

        v_t = _dot(xk, wv_ref[...]).T
        v_ref[0, :, :, rows] = v_t.reshape(H_B, D_HB, ts)
        for j in range(H_B // 2):
            v_pair = v_t[j * LANES:(j + 1) * LANES, :]
            va_ref[2 * j, :, rows] = jnp.where(row_t < D_HB, v_pair,
                                               jnp.where(row_t == AUG_BASE[0], 1.0, 0.0)).astype(BF16)
            va_ref[2 * j + 1, :, rows] = jnp.where(row_t < D_HB,
                                                   jnp.where(row_t == AUG_BASE[1], 1.0, 0.0),
                                                   v_pair).astype(BF16)

        q = _dot(xq, wq_ref[...]) * ((D_HB ** -0.5) * LOG2E)
        for h in range(H_B):
            put(qa_ref, h, q[:, (h // 2) * LANES:(h // 2 + 1) * LANES], tails(h)[0])

        g_ref[rows, :] = _dot(xq, wg_ref[...])
        return f[ts - 1:ts, :]

    f_run = carry[...]
    for i in range(sub_tiles):
        f_run = sub_tile(i * ts, f_run)
    carry[...] = f_run


def _proj(x2d, batch, seq, aug, norm_kv, norm_b, w_k, w_v, w_f, b_f, w_qq, w_qg):
    ntok, d = x2d.shape
    hd = w_k.shape[1]
    ins = [x2d, norm_kv, norm_b, w_k, w_v, w_f, b_f, w_qq, w_qg]
    if aug:
        tm = min(seq, 512)
        n_t = seq // tm
        grid = (batch, n_t)
        tok = lambda b, t: (b * n_t + t, 0)
        head_major = lambda b, t: (0, b * n_t + t, 0)
        time_minor = pl.BlockSpec((1, H_B, D_HB, tm), lambda b, t: (b, 0, 0, t))
        out_specs = [time_minor, time_minor, pl.BlockSpec((1, H_B, tm), lambda b, t: (b, 0, t)),
                     pl.BlockSpec((tm, hd), tok),
                     pl.BlockSpec((H_B, tm, LANES), head_major),
                     pl.BlockSpec((H_B, tm, LANES), head_major),
                     pl.BlockSpec((H_B, LANES, tm), lambda b, t: (0, 0, b * n_t + t))]
        out_shape = ([jax.ShapeDtypeStruct((batch, H_B, D_HB, seq), F32)] * 2
                     + [jax.ShapeDtypeStruct((batch, H_B, seq), F32),
                        jax.ShapeDtypeStruct((ntok, hd), F32),
                        jax.ShapeDtypeStruct((H_B, ntok, LANES), BF16),
                        jax.ShapeDtypeStruct((H_B, ntok, LANES), BF16),
                        jax.ShapeDtypeStruct((H_B, LANES, ntok), BF16)])
        scratch = [pltpu.VMEM((1, LANES), F32)]
    else:
        tm = min(ntok, 512)
        n_t = ntok // tm
        grid = (1, n_t)
        tok = lambda b, t: (t, 0)
        tok4 = lambda b, t: (t, 0, 0)
        out_specs = ([pl.BlockSpec((tm, H_B, D_HB), tok4)] * 2 + [pl.BlockSpec((tm, H_B), tok)]
                     + [pl.BlockSpec((tm, hd), tok)] * 4)
        out_shape = ([jax.ShapeDtypeStruct((ntok, H_B, D_HB), F32)] * 2
                     + [jax.ShapeDtypeStruct((ntok, H_B), F32)]
                     + [jax.ShapeDtypeStruct((ntok, hd), F32)] * 4)
        scratch = [pltpu.VMEM((H_B * _head_pitch(tm), LANES), F32)] * 2
    return pl.pallas_call(
        functools.partial(_proj_kernel, aug=aug, sub_tiles=2 if tm % 512 == 0 else 1),
        grid=grid,
        in_specs=[pl.BlockSpec((tm, d), tok)] + [_const_spec(a.shape) for a in ins[1:]],
        out_specs=out_specs,
        out_shape=out_shape,
        scratch_shapes=scratch,
        compiler_params=pltpu.CompilerParams(
            dimension_semantics=("arbitrary", "arbitrary"), vmem_limit_bytes=VMEM_LIMIT),
        name="kvq_proj",
    )(*ins)


def _cumsum_kernel(lf_ref, f_ref):
    n_blk = lf_ref.shape[2] // LANES
    upper = (lax.broadcasted_iota(jnp.int32, (LANES, LANES), 0)
             <= lax.broadcasted_iota(jnp.int32, (LANES, LANES), 1)).astype(BF16)
    run = jnp.zeros((lf_ref.shape[1], 1), F32)
    for c in range(n_blk):
        x = lf_ref[0, :, c * LANES:(c + 1) * LANES]
        h1 = x.astype(BF16)
        r1 = x - h1.astype(F32)
        h2 = r1.astype(BF16)
        h3 = (r1 - h2.astype(F32)).astype(BF16)
        cs = (_dot(h1, upper) + _dot(h2, upper)) + _dot(h3, upper) + run
        f_ref[0, :, c * LANES:(c + 1) * LANES] = cs
        run = cs[:, LANES - 1:LANES]


def _cumsum_time(lf_t):
    b, h, l = lf_t.shape
    return pl.pallas_call(
        _cumsum_kernel,
        grid=(b,),
        in_specs=[pl.BlockSpec((1, h, l), lambda i: (i, 0, 0))],
        out_specs=pl.BlockSpec((1, h, l), lambda i: (i, 0, 0)),
        out_shape=jax.ShapeDtypeStruct((b, h, l), F32),
        compiler_params=pltpu.CompilerParams(dimension_semantics=("arbitrary",)),
        name="logf_cumsum",
    )(lf_t)


def _head_column(f_tile, head):
    lane = lax.broadcasted_iota(jnp.int32, f_tile.shape, 1)
    return jnp.sum(jnp.where(lane == head, f_tile, 0.0), axis=-1, keepdims=True)


def _attn_prompt_kernel(qa_ref, ka_ref, vat_ref, g_ref, o_ref, m_scr, acc_scr, *, tq):
    seq = qa_ref.shape[1]
    n_q = seq // tq
    half = tq // 2
    row_o = lax.broadcasted_iota(jnp.int32, (LANES, tq), 0)

    def visible(keys, queries):
        return (lax.broadcasted_iota(jnp.int32, (keys, queries), 0)
                <= lax.broadcasted_iota(jnp.int32, (keys, queries), 1))

    items = []
    for pair in range(qa_ref.shape[0] // 2):
        for qt in range(n_q):
            tile = (pair, qt)
            for kt in range(qt):
                items.append((tile, 0, tq, kt * tq, tq, None))
            items.append((tile, 0, tq, qt * tq, half, visible(half, tq)))
            items.append((tile, half, half, qt * tq + half, half, visible(half, half)))

    def scores(item):
        (pair, qt), q_lo, q_len, c0, c_len, _ = item
        return [_dot_nt(ka_ref[2 * pair + hf, c0:c0 + c_len, :],
                        qa_ref[2 * pair + hf, qt * tq + q_lo:qt * tq + q_lo + q_len, :])
                for hf in range(2)]

    def update(item, s_pair):
        (pair, _), q_lo, q_len, c0, c_len, mask = item
        cols = slice(q_lo, q_lo + q_len)
        for hf in range(2):
            st = s_pair[hf]
            if mask is not None:
                st = jnp.where(mask, st, -jnp.inf)
            m_old = m_scr[hf, :, cols]
            m_new = jnp.maximum(m_old, jnp.max(st, axis=0, keepdims=True))
            pt = jnp.exp2(st - m_new).astype(BF16)
            acc_scr[hf, :, cols] = (jnp.exp2(m_old - m_new) * acc_scr[hf, :, cols]
                                    + _dot(vat_ref[2 * pair + hf, :, c0:c0 + c_len], pt))
            m_scr[hf, :, cols] = m_new

    def finish(tile):
        pair, qt = tile
        rows, cols = slice(qt * tq, (qt + 1) * tq), slice(pair * LANES, (pair + 1) * LANES)
        outs = []
        for hf in range(2):
            acc = acc_scr[hf]
            outs.append(acc / acc[AUG_BASE[hf]:AUG_BASE[hf] + 1, :])
        o = jnp.where(row_o < D_HB, outs[0], outs[1]).T
        o = o * _sigmoid(g_ref[0, rows, cols])
        o_ref[0, rows, cols] = o.astype(BF16)

    s_next = scores(items[0])
    for i, item in enumerate(items):
        s_cur = s_next
        if i + 1 < len(items):
            s_next = scores(items[i + 1])
        if i == 0 or items[i - 1][0] != item[0]:
            m_scr[...] = jnp.full(m_scr.shape, -jnp.inf, F32)
            acc_scr[...] = jnp.zeros(acc_scr.shape, F32)
        update(item, s_cur)
        if i + 1 == len(items) or items[i + 1][0] != item[0]:
            finish(item[0])


def _attn_prompt(qa, ka, vat, g):
    b, t, hd = g.shape
    tq = min(t, 512)
    pairs = 2
    n_blk = hd // (pairs * LANES)
    heads = pl.BlockSpec((2 * pairs, t, LANES), lambda i, j: (j, i, 0))
    heads_t = pl.BlockSpec((2 * pairs, LANES, t), lambda i, j: (j, 0, i))
    col = pl.BlockSpec((1, t, pairs * LANES), lambda i, j: (i, 0, j))
    kern = functools.partial(_attn_prompt_kernel, tq=tq)
    return pl.pallas_call(
        kern,
        grid=(b, n_blk),
        in_specs=[heads, heads, heads_t, col],
        out_specs=col,
        out_shape=jax.ShapeDtypeStruct((b, t, hd), BF16),
        scratch_shapes=[pltpu.VMEM((2, 1, tq), F32), pltpu.VMEM((2, LANES, tq), F32)],
        compiler_params=pltpu.CompilerParams(
            dimension_semantics=("arbitrary", "arbitrary"), vmem_limit_bytes=VMEM_LIMIT),
        name="fox_attn_prompt",
    )(qa, ka, vat, g)


def _pair_block(qs, k2, v2, fq, fk, mask, m_ref, l_ref, acc_ref, keys_on_lanes=False):
    tq = qs.shape[0] // 2
    tk = k2.shape[1] if keys_on_lanes else k2.shape[0]
    width = min(tk, LANES)
    s = _dot(qs, k2) if keys_on_lanes else _dot_nt(qs, k2)
    alphas, probs = [], []
    for hf in range(2):
        sh = s[hf * tq:(hf + 1) * tq] + (fq[hf] - fk[hf])
        if mask is not None:
            sh = jnp.where(mask, sh, -jnp.inf)
        cols = [sh[:, c * width:(c + 1) * width] for c in range(tk // width)]
        m_old = m_ref[hf]
        m_new = jnp.maximum(
            m_old, jnp.max(functools.reduce(jnp.maximum, cols), axis=-1, keepdims=True))
        p_cols = [jnp.exp(c - m_new[:, :width]) for c in cols]
        alpha = jnp.exp(m_old - m_new)
        l_ref[hf] = alpha * l_ref[hf] + jnp.sum(
            functools.reduce(lambda a, b: a + b, p_cols), axis=-1, keepdims=True)
        m_ref[hf] = m_new
        probs.append(jnp.concatenate([p.astype(BF16) for p in p_cols], axis=1))
        alphas.append(alpha)
    p_all = jnp.concatenate(probs, axis=0)
    pv = _dot_nt(p_all, v2) if keys_on_lanes else _dot(p_all, v2)
    acc_ref[...] = jnp.concatenate(alphas, axis=0) * acc_ref[...] + pv


def _split_heads(q2, lane_a):
    zero = jnp.zeros_like(q2)
    return jnp.concatenate([jnp.where(lane_a, q2, zero), jnp.where(lane_a, zero, q2)], axis=0)


def _attn_sample_kernel(q_ref, g_ref, kc_ref, vc_ref, kn_ref, vn_ref, fq_ref, fkc_ref, fkn_ref,
                        o_ref, m_scr, l_scr, acc_scr, *, n_pairs):
    kt = pl.program_id(1)
    n_kt = pl.num_programs(1)
    tq = q_ref.shape[1]
    scale = D_HB ** -0.5
    lane_a = lax.broadcasted_iota(jnp.int32, (tq, LANES), 1) < D_HB
    causal = (lax.broadcasted_iota(jnp.int32, (tq, tq), 1)
              <= lax.broadcasted_iota(jnp.int32, (tq, tq), 0))
    f_tile = fq_ref[0]

    @pl.when(kt == 0)
    def _():
        m_scr[...] = jnp.full(m_scr.shape, -jnp.inf, F32)
        l_scr[...] = jnp.zeros(l_scr.shape, F32)
        acc_scr[...] = jnp.zeros(acc_scr.shape, F32)

    def cache_pair(ref, hp):
        return ref[0, 2 * hp:2 * hp + 2].reshape(2 * D_HB, ref.shape[3]).astype(BF16)

    def pair_inputs(hp):
        cols = slice(hp * LANES, (hp + 1) * LANES)
        qs = _split_heads(q_ref[0, :, cols] * scale, lane_a).astype(BF16)
        fq = [_head_column(f_tile, 2 * hp + hf) for hf in range(2)]
        return cols, qs, fq

    for hp in range(n_pairs):
        cols, qs, fq = pair_inputs(hp)
        fk = [fkc_ref[0, 2 * hp + hf:2 * hp + hf + 1, :] for hf in range(2)]
        _pair_block(qs, cache_pair(kc_ref, hp), cache_pair(vc_ref, hp),
                    fq, fk, None, m_scr.at[hp], l_scr.at[hp], acc_scr.at[hp], keys_on_lanes=True)

    @pl.when(kt == n_kt - 1)
    def _():
        for hp in range(n_pairs):
            cols, qs, fq = pair_inputs(hp)
            fk = [fkn_ref[0, 2 * hp + hf:2 * hp + hf + 1, 0:tq] for hf in range(2)]
            _pair_block(qs, kn_ref[0, :, cols].astype(BF16), vn_ref[0, :, cols].astype(BF16),
                        fq, fk, causal, m_scr.at[hp], l_scr.at[hp], acc_scr.at[hp])
            acc = acc_scr[hp]
            o = jnp.where(lane_a, acc[:tq] / l_scr[hp, 0], acc[tq:] / l_scr[hp, 1])
            o = o * _sigmoid(g_ref[0, :, cols])
            o_ref[0, :, cols] = o.astype(BF16)


def _attn_sample(q, g, k_cache, v_cache, k_new, v_new, f_new, f_t, past):
    b, t, hd = q.shape
    n_pairs = hd // LANES
    tk = min(past, 1024)
    n_kt = past // tk
    row = pl.BlockSpec((1, t, hd), lambda i, j: (i, 0, 0))
    cache = pl.BlockSpec((1, H_B, D_HB, tk), lambda i, j: (i, 0, 0, j))
    kern = functools.partial(_attn_sample_kernel, n_pairs=n_pairs)
    return pl.pallas_call(
        kern,
        grid=(b, n_kt),
        in_specs=[row, row, cache, cache, row, row,
                  pl.BlockSpec((1, t, H_B), lambda i, j: (i, 0, 0)),
                  pl.BlockSpec((1, H_B, tk), lambda i, j: (i, 0, j)),
                  pl.BlockSpec((1, H_B, LANES), lambda i, j: (i, 0, past // LANES))],
        out_specs=row,
        out_shape=jax.ShapeDtypeStruct((b, t, hd), BF16),
        scratch_shapes=[pltpu.VMEM((n_pairs, 2, t, LANES), F32), pltpu.VMEM((n_pairs, 2, t, LANES), F32),
                        pltpu.VMEM((n_pairs, 2 * t, LANES), F32)],
        compiler_params=pltpu.CompilerParams(
            dimension_semantics=("arbitrary", "arbitrary"), vmem_limit_bytes=VMEM_LIMIT),
        name="fox_attn_sample",
    )(q, g, k_cache, v_cache, k_new, v_new, f_new, f_t, f_t)


def _trunk(x, state0, k_past, v_past, logf_past, w):
    batch, seq, d = x.shape
    ntok = batch * seq
    hd = H_B * D_HB
    x2d = x.reshape(ntok, d)
    s0 = None if state0 is None else state0[:, 0]
    o_a, state = _hgrn(x2d, batch, seq, w["norm_a"], w["w_in"], w["lb_logits_h"], w["g_norm_a"], s0)
    x2d = _post(x2d, o_a, w["w_o_a"], w["norm_mlp0"], w["w_up"], w["w_down"], 0)
    proj_w = (w["norm_kv"], w["norm_b"], w["w_k"], w["w_v"], w["w_f"], w["b_f"], w["w_qq"], w["w_qg"])
    if k_past is None:
        k_t, v_t, logf_t, g, qa, ka, va = _proj(x2d, batch, seq, True, *proj_w)
        k, v = jnp.transpose(k_t, (0, 3, 1, 2)), jnp.transpose(v_t, (0, 3, 1, 2))
        logf = jnp.swapaxes(logf_t, 1, 2)
        o_b = _attn_prompt(qa, ka, va, g.reshape(batch, seq, hd))
    else:
        k, v, logf, g, q, k_flat, v_flat = _proj(x2d, batch, seq, False, *proj_w)
        past = k_past.shape[1]
        total = past + seq
        padded = -(-total // LANES) * LANES
        lf_all = jnp.concatenate(
            [jnp.swapaxes(logf_past.astype(F32), 1, 2),
             jnp.swapaxes(logf.reshape(batch, seq, H_B), 1, 2),
             jnp.zeros((batch, H_B, padded - total), F32)], axis=2)
        f_t = _cumsum_time(lf_all.reshape(1, batch * H_B, padded)).reshape(batch, H_B, padded)
        o_b = _attn_sample(q.reshape(batch, seq, hd), g.reshape(batch, seq, hd),
                           jnp.transpose(k_past, (0, 2, 3, 1)).astype(F32),
                           jnp.transpose(v_past, (0, 2, 3, 1)).astype(F32),
                           k_flat.reshape(batch, seq, hd), v_flat.reshape(batch, seq, hd),
                           jnp.swapaxes(f_t[:, :, past:total], 1, 2), f_t, past)
    y = _post(x2d, o_b.reshape(ntok, hd), w["w_o_b"], w["norm_mlp1"], w["w_up"], w["w_down"], 1,
              w["norm_f"])
    return (y.reshape(batch, seq, d), state[:, None],
            k.reshape(batch, seq, H_B, D_HB), v.reshape(batch, seq, H_B, D_HB),
            logf.reshape(batch, seq, H_B))


def kernel(x_prompt, x_sample, state_hgrn, cache_k, cache_v, cache_logf, norm_a, w_in_a, lb_logits, g_norm_a, w_o_a, norm_kv, w_kv, b_f, norm_b, w_q_b, w_o_b, norm_mlp, w_up, w_down, norm_f):
    d = x_prompt.shape[-1]
    assert w_in_a.shape[0] == 1 and w_q_b.shape[0] == 1, "one HGRN2 layer and one FoX layer"
    n_heads = w_in_a.shape[2] // (4 * K_A)
    hd = H_B * D_HB
    row = lambda a: a.reshape(1, -1).astype(F32)
    w_f = jnp.zeros((d, LANES), F32).at[:, :H_B].set(w_kv[:, 2 * hd:])
    w = {
        "norm_a": row(norm_a[0]),
        "w_in": w_in_a[0].astype(BF16),
        "lb_logits_h": lb_logits.astype(F32).reshape(lb_logits.shape[0], n_heads, 1, K_A),
        "g_norm_a": row(g_norm_a[0]),
        "w_o_a": w_o_a[0].astype(BF16),
        "norm_mlp0": row(norm_mlp[0]), "norm_mlp1": row(norm_mlp[1]),
        "w_up": w_up.astype(BF16), "w_down": w_down.astype(BF16),
        "norm_kv": row(norm_kv), "norm_b": row(norm_b[0]),
        "w_k": w_kv[:, :hd].astype(BF16), "w_v": w_kv[:, hd:2 * hd].astype(BF16),
        "w_f": w_f.astype(BF16), "b_f": jnp.zeros((1, LANES), F32).at[0, :H_B].set(b_f.astype(F32)),
        "w_qq": w_q_b[0][:, :hd].astype(BF16), "w_qg": w_q_b[0][:, hd:].astype(BF16),
        "w_o_b": w_o_b[0].astype(BF16),
        "norm_f": row(norm_f),
    }
    y_p, st_p, k_p, v_p, lf_p = _trunk(x_prompt, None, None, None, None, w)
    y_s, st_s, k_s, v_s, lf_s = _trunk(x_sample, state_hgrn, cache_k, cache_v, cache_logf, w)
    return (y_p, y_s, st_p, k_p, v_p, lf_p, st_s, k_s, v_s, lf_s)
```

```python
import functools

import jax
import jax.numpy as jnp
from jax import lax
from jax.experimental import pallas as pl
from jax.experimental.pallas import tpu as pltpu

EPS = 1e-6
CHUNK = 64
K_A = 128
V_A = 128
H_B = 16
D_HB = 64
LANES = 128
VMEM_LIMIT = 56 * 1024 * 1024

F32 = jnp.float32
BF16 = jnp.bfloat16


def _dot(a, b):
    return jnp.dot(a, b, preferred_element_type=F32)


def _dot_nt(a, b):
    return lax.dot_general(a, b, (((1,), (1,)), ((), ())), preferred_element_type=F32)


def _dot_tn(a, b):
    return lax.dot_general(a, b, (((0,), (0,)), ((), ())), preferred_element_type=F32)


def _sigmoid(x):
    return 0.5 * jnp.tanh(0.5 * x) + 0.5


def _const_spec(shape):
    nd = len(shape)
    return pl.BlockSpec(shape, lambda *_: (0,) * nd, pipeline_mode=pl.Buffered(1))


def _hgrn_kernel(*refs, n_heads, n_chunks, carry):
    if carry:
        (x_ref, nrm_ref, win_ref, lbl_ref, gn_ref, o_ref, sout_ref, xn_scr, st_scr) = refs
        s0_ref = None
    else:
        (x_ref, nrm_ref, win_ref, lbl_ref, gn_ref, s0_ref, o_ref, sout_ref, xn_scr) = refs
        st_scr = None
    tm = n_chunks * CHUNK
    n_sub = x_ref.shape[0] // tm

    def normalise(sub):
        x = x_ref[sub * tm:(sub + 1) * tm, :]
        inv = lax.rsqrt(jnp.mean(x * x, axis=-1, keepdims=True) + EPS)
        xn_scr[sub] = (x * inv * nrm_ref[...]).astype(BF16)

    if carry:
        @pl.when(pl.program_id(1) == 0)
        def _():
            st_scr[...] = jnp.zeros_like(st_scr)

    row_in_chunk = lax.broadcasted_iota(jnp.int32, (tm, K_A), 0) % CHUNK
    tri = (lax.broadcasted_iota(jnp.int32, (CHUNK, CHUNK), 1)
           <= lax.broadcasted_iota(jnp.int32, (CHUNK, CHUNK), 0))

    d_a = n_heads * K_A

    def project(sub, pair):
        xn = xn_scr[sub]
        return [_dot(xn, win_ref[:, part * d_a + pair * 2 * K_A:part * d_a + (pair + 1) * 2 * K_A])
                for part in range(4)]

    def gates(h, p4):
        half = slice((h % 2) * K_A, (h % 2 + 1) * K_A)
        pq, pz, pi, pg = (p[:, half] for p in p4)
        n_rows = lbl_ref.shape[0]
        lrows = [lbl_ref[r, h] for r in range(n_rows)]
        lmax = functools.reduce(jnp.maximum, lrows)
        lexp = [jnp.exp(l - lmax) for l in lrows]
        lb = lexp[0] / functools.reduce(lambda a, b: a + b, lexp)

        q = pq * _sigmoid(pq)
        f = lb + (1.0 - lb) * _sigmoid(pz)
        logf = jnp.log(f)
        k = 1.0 - f
        bc = logf
        s = 1
        while s < CHUNK:
            bc = bc + jnp.where(row_in_chunk >= s, pltpu.roll(bc, s, axis=0), 0.0)
            s *= 2
        chunked = lambda a: a.reshape(n_chunks, CHUNK, K_A)
        bc3 = chunked(bc)
        btot = bc3[:, CHUNK - 1:CHUNK, :]
        q_dec = chunked(q * jnp.exp(bc)).astype(BF16)
        k_inv = chunked(k * jnp.exp(-bc)).astype(BF16)
        k_end = (chunked(k) * jnp.exp(btot - bc3)).astype(BF16)
        decay = jnp.exp(btot)
        v = chunked(pi).astype(BF16)
        return q_dec, k_inv, k_end, decay, v, _sigmoid(pg)

    def recur(sub, heads, staged_heads):
        rows = slice(sub * tm, (sub + 1) * tm)
        cat = lambda i: jnp.concatenate([s[i] for s in staged_heads], axis=0)
        q_dec, k_inv, k_end, v = cat(0), cat(1), cat(2), cat(4)
        sc = jnp.einsum('cqk,csk->cqs', q_dec, k_inv, preferred_element_type=F32)
        sc = jnp.where(tri, sc, 0.0).astype(BF16)
        o_intra = jnp.einsum('cqs,csv->cqv', sc, v, preferred_element_type=F32)
        upd = jnp.einsum('csv,csk->cvk', v, k_end, preferred_element_type=F32)
        states = []
        for i, h in enumerate(heads):
            decay = staged_heads[i][3]
            if carry:
                st = st_scr[h]
                for c in range(n_chunks):
                    states.append(st)
                    st = st * decay[c] + upd[i * n_chunks + c]
                st_scr[h] = st
                sout_ref[0, h] = st.T
            else:
                for c in range(n_chunks):
                    st = s0_ref[c, h].T
                    states.append(st)
                    sout_ref[c, h] = (st * decay[c] + upd[i * n_chunks + c]).T
        s_in = jnp.stack(states, axis=0).astype(BF16)
        o_inter = jnp.einsum('cqk,cvk->cqv', q_dec, s_in, preferred_element_type=F32)
        o_all = o_intra + o_inter
        for i, h in enumerate(heads):
            o = o_all[i * n_chunks:(i + 1) * n_chunks].reshape(tm, V_A)
            o = o * lax.rsqrt(jnp.mean(o * o, axis=-1, keepdims=True) + EPS) * gn_ref[...]
            o_ref[rows, h * V_A:(h + 1) * V_A] = (o * staged_heads[i][5]).astype(BF16)

    items = [(sub, pair) for sub in range(n_sub) for pair in range(n_heads // 2)]
    normalise(0)
    p_next = project(*items[0])
    staged = None
    for i, (sub, pair) in enumerate(items):
        p_cur = p_next
        if i + 1 < len(items):
            if items[i + 1][0] != sub:
                normalise(items[i + 1][0])
            p_next = project(*items[i + 1])
        if staged is not None:
            recur(*staged)
        staged = (sub, (2 * pair, 2 * pair + 1), [gates(2 * pair, p_cur), gates(2 * pair + 1, p_cur)])
    recur(*staged)


def _hgrn(x2d, batch, seq, norm_w, w_in, lb_logits_h, g_norm, s0):
    ntok, d = x2d.shape
    n_heads = w_in.shape[1] // (4 * K_A)
    carry = s0 is None
    if carry:
        sub = min(seq, 512)
        tm = min(seq, 2 * sub)
        n_t = seq // tm
        grid = (batch, n_t)
        tok_map = lambda b, t: (b * n_t + t, 0)
        st_spec = pl.BlockSpec((1, n_heads, K_A, V_A), lambda b, t: (b, 0, 0, 0))
        scratch = [pltpu.VMEM((tm // sub, sub, d), BF16), pltpu.VMEM((n_heads, V_A, K_A), F32)]
        extra_in, extra_specs = [], []
    else:
        assert seq == CHUNK
        bt = min(batch, 8)
        sub = tm = bt * CHUNK
        grid = (batch // bt, 1)
        tok_map = lambda b, t: (b, 0)
        st_spec = pl.BlockSpec((bt, n_heads, K_A, V_A), lambda b, t: (b, 0, 0, 0))
        scratch = [pltpu.VMEM((1, tm, d), BF16)]
        extra_in, extra_specs = [s0], [st_spec]
    n_chunks = sub // CHUNK
    kern = functools.partial(_hgrn_kernel, n_heads=n_heads, n_chunks=n_chunks, carry=carry)
    return pl.pallas_call(
        kern,
        grid=grid,
        in_specs=[pl.BlockSpec((tm, d), tok_map),
                  _const_spec(norm_w.shape), _const_spec(w_in.shape),
                  _const_spec(lb_logits_h.shape), _const_spec(g_norm.shape)] + extra_specs,
        out_specs=[pl.BlockSpec((tm, n_heads * V_A), tok_map), st_spec],
        out_shape=[jax.ShapeDtypeStruct((ntok, n_heads * V_A), BF16),
                   jax.ShapeDtypeStruct((batch, n_heads, K_A, V_A), F32)],
        scratch_shapes=scratch,
        compiler_params=pltpu.CompilerParams(
            dimension_semantics=("arbitrary", "arbitrary"), vmem_limit_bytes=VMEM_LIMIT),
        name="hgrn_mixer",
    )(x2d, norm_w, w_in, lb_logits_h, g_norm, *extra_in)


def _post_kernel(*refs, ff_block, final_norm):
    if final_norm:
        x_ref, o_ref, wo_ref, nm_ref, wup_ref, wdn_ref, nf_ref, y_ref = refs
    else:
        x_ref, o_ref, wo_ref, nm_ref, wup_ref, wdn_ref, y_ref = refs
    x1 = x_ref[...] + _dot(o_ref[...], wo_ref[...])
    inv = lax.rsqrt(jnp.mean(x1 * x1, axis=-1, keepdims=True) + EPS)
    xn = (x1 * inv * nm_ref[...]).astype(BF16)
    acc = x1
    d_ff = wup_ref.shape[1]
    for j in range(d_ff // ff_block):
        hcol = jnp.maximum(_dot(xn, wup_ref[:, j * ff_block:(j + 1) * ff_block]), 0.0)
        acc = acc + _dot((hcol * hcol).astype(BF16), wdn_ref[j * ff_block:(j + 1) * ff_block, :])
    if final_norm:
        inv = lax.rsqrt(jnp.mean(acc * acc, axis=-1, keepdims=True) + EPS)
        acc = acc * inv * nf_ref[...]
    y_ref[...] = acc


def _layer_spec(stacked, layer):
    return pl.BlockSpec((None,) + stacked.shape[1:], lambda *_: (layer, 0, 0),
                        pipeline_mode=pl.Buffered(1))


def _post(x2d, o2d, w_o, norm_mlp, w_up, w_down, layer, norm_f=None):
    ntok, d = x2d.shape
    tm = min(ntok, 512)
    final_norm = norm_f is not None
    tok = lambda i: (i, 0)
    ins = [x2d, o2d, w_o, norm_mlp, w_up, w_down] + ([norm_f] if final_norm else [])
    specs = [pl.BlockSpec((tm, d), tok), pl.BlockSpec((tm, o2d.shape[1]), tok),
             _const_spec(w_o.shape), _const_spec(norm_mlp.shape),
             _layer_spec(w_up, layer), _layer_spec(w_down, layer)]
    specs += [_const_spec(norm_f.shape)] if final_norm else []
    kern = functools.partial(_post_kernel, ff_block=1024, final_norm=final_norm)
    return pl.pallas_call(
        kern,
        grid=(ntok // tm,),
        in_specs=specs,
        out_specs=pl.BlockSpec((tm, d), tok),
        out_shape=jax.ShapeDtypeStruct((ntok, d), F32),
        compiler_params=pltpu.CompilerParams(
            dimension_semantics=("arbitrary",), vmem_limit_bytes=VMEM_LIMIT),
        name="post_mlp",
    )(*ins)


LOG2E = 1.4426950408889634
AUG_BASE = (D_HB, 0)


def _split3(x):
    h1 = x.astype(BF16).astype(F32)
    r1 = x - h1
    h2 = r1.astype(BF16).astype(F32)
    h3 = (r1 - h2).astype(BF16).astype(F32)
    return h1, h2, h3


def _head_pitch(tm):
    return tm + 8 if (tm // 8) % 2 == 0 else tm


def _store_head_major(val, scr, out_ref):
    tm = val.shape[0]
    pitch = _head_pitch(tm)
    for h in range(H_B):
        pair = val[:, (h // 2) * LANES:(h // 2 + 1) * LANES]
        scr[h * pitch:h * pitch + tm, :] = pair if h % 2 == 0 else pltpu.roll(pair, D_HB, axis=1)

    for t in range(tm):
        for grp in range(H_B // 8):
            rows = scr[pl.ds(grp * 8 * pitch + t, 8, stride=pitch), :]
            out_ref[t, grp * 8:(grp + 1) * 8, :] = rows[:, :D_HB]


def _proj_kernel(*refs, aug, sub_tiles):
    if aug:
        (x_ref, nkv_ref, nb_ref, wk_ref, wv_ref, wf_ref, bf_ref, wq_ref, wg_ref,
         k_ref, v_ref, lf_ref, g_ref, qa_ref, ka_ref, va_ref, carry) = refs
    else:
        (x_ref, nkv_ref, nb_ref, wk_ref, wv_ref, wf_ref, bf_ref, wq_ref, wg_ref,
         k_ref, v_ref, lf_ref, g_ref, q_ref, kf_ref, vf_ref, k_scr, v_scr) = refs
    tm = x_ref.shape[0]
    if not aug:
        x = x_ref[...]
        xs = x * lax.rsqrt(jnp.mean(x * x, axis=-1, keepdims=True) + EPS)
        xk = (xs * nkv_ref[...]).astype(BF16)
        xq = (xs * nb_ref[...]).astype(BF16)
        z = _dot(xk, wf_ref[...]) + bf_ref[...]
        lf = jnp.minimum(z, 0.0) - jnp.log(1.0 + jnp.exp(-jnp.abs(z)))
        k = _dot(xk, wk_ref[...])
        v = _dot(xk, wv_ref[...])
        g_ref[...] = _dot(xq, wg_ref[...])
        _store_head_major(k, k_scr, k_ref)
        _store_head_major(v, v_scr, v_ref)
        lf_ref[...] = lf[:, :H_B]
        q_ref[...] = _dot(xq, wq_ref[...])
        kf_ref[...] = k
        vf_ref[...] = v
        return

    @pl.when(pl.program_id(1) == 0)
    def _():
        carry[...] = jnp.zeros_like(carry)

    ts = tm // sub_tiles
    row = lax.broadcasted_iota(jnp.int32, (ts, LANES), 0)
    lane = lax.broadcasted_iota(jnp.int32, (ts, LANES), 1)
    row_t = lax.broadcasted_iota(jnp.int32, (LANES, ts), 0)
    low = lane < D_HB

    def sub_tile(r0, f_in):
        rows = slice(r0, r0 + ts)
        x = x_ref[rows, :]
        xs = x * lax.rsqrt(jnp.mean(x * x, axis=-1, keepdims=True) + EPS)
        xk = (xs * nkv_ref[...]).astype(BF16)
        xq = (xs * nb_ref[...]).astype(BF16)
        z = _dot(xk, wf_ref[...]) + bf_ref[...]
        lf = jnp.minimum(z, 0.0) - jnp.log(1.0 + jnp.exp(-jnp.abs(z)))
        lf_ref[0, :, rows] = lf.T[:H_B, :]
        f = lf
        s = 1
        while s < ts:
            f = f + jnp.where(row >= s, pltpu.roll(f, s, axis=0), 0.0)
            s *= 2
        f = f + f_in
        h1, h2, h3 = _split3(f * LOG2E)
        pieces = jnp.where(lane < H_B, h1,
                           jnp.where(lane < 2 * H_B, pltpu.roll(h2, H_B, axis=1),
                                     jnp.where(lane < 3 * H_B, pltpu.roll(h3, 2 * H_B, axis=1), 0.0)))
        both = pieces + pltpu.roll(pieces, D_HB, axis=1)
        shifted = pltpu.roll(both, 1, axis=1)
        neg_both = -both

        def tails(h):
            base = AUG_BASE[h % 2]
            at = (lane == base + h) | (lane == base + H_B + h) | (lane == base + 2 * H_B + h)
            up = ((lane == base + h + 1) | (lane == base + H_B + h + 1)
                  | (lane == base + 2 * H_B + h + 1))
            q_tail = jnp.where(up, shifted, jnp.where(at, 1.0, 0.0))
            k_tail = jnp.where(at, neg_both, jnp.where(up, 1.0, 0.0))
            return q_tail, k_tail

        def put(ref, h, data, tail):
            ref[h, rows, :] = (jnp.where(low, data, tail) if h % 2 == 0
                               else jnp.where(low, tail, data)).astype(BF16)

        k = _dot(xk, wk_ref[...])
        k_ref[0, :, :, rows] = k.T.reshape(H_B, D_HB, ts)
        for h in range(H_B):
            put(ka_ref, h, k[:, (h // 2) * LANES:(h // 2 + 1) * LANES], tails(h)[1])

        v_t = _dot(xk, wv_ref[...]).T
        v_ref[0, :, :, rows] = v_t.reshape(H_B, D_HB, ts)
        for j in range(H_B // 2):
            v_pair = v_t[j * LANES:(j + 1) * LANES, :]
            va_ref[2 * j, :, rows] = jnp.where(row_t < D_HB, v_pair,
                                               jnp.where(row_t == AUG_BASE[0], 1.0, 0.0)).astype(BF16)
            va_ref[2 * j + 1, :, rows] = jnp.where(row_t < D_HB,
                                                   jnp.where(row_t == AUG_BASE[1], 1.0, 0.0),
                                                   v_pair).astype(BF16)

        q = _dot(xq, wq_ref[...]) * ((D_HB ** -0.5) * LOG2E)
        for h in range(H_B):
            put(qa_ref, h, q[:, (h // 2) * LANES:(h // 2 + 1) * LANES], tails(h)[0])

        g_ref[rows, :] = _dot(xq, wg_ref[...])
        return f[ts - 1:ts, :]

    f_run = carry[...]
    for i in range(sub_tiles):
        f_run = sub_tile(i * ts, f_run)
    carry[...] = f_run


def _proj(x2d, batch, seq, aug, norm_kv, norm_b, w_k, w_v, w_f, b_f, w_qq, w_qg):
    ntok, d = x2d.shape
    hd = w_k.shape[1]
    ins = [x2d, norm_kv, norm_b, w_k, w_v, w_f, b_f, w_qq, w_qg]
    if aug:
        tm = min(seq, 512)
        n_t = seq // tm
        grid = (batch, n_t)
        tok = lambda b, t: (b * n_t + t, 0)
        head_major = lambda b, t: (0, b * n_t + t, 0)
        time_minor = pl.BlockSpec((1, H_B, D_HB, tm), lambda b, t: (b, 0, 0, t))
        out_specs = [time_minor, time_minor, pl.BlockSpec((1, H_B, tm), lambda b, t: (b, 0, t)),
                     pl.BlockSpec((tm, hd), tok),
                     pl.BlockSpec((H_B, tm, LANES), head_major),
                     pl.BlockSpec((H_B, tm, LANES), head_major),
                     pl.BlockSpec((H_B, LANES, tm), lambda b, t: (0, 0, b * n_t + t))]
        out_shape = ([jax.ShapeDtypeStruct((batch, H_B, D_HB, seq), F32)] * 2
                     + [jax.ShapeDtypeStruct((batch, H_B, seq), F32),
                        jax.ShapeDtypeStruct((ntok, hd), F32),
                        jax.ShapeDtypeStruct((H_B, ntok, LANES), BF16),
                        jax.ShapeDtypeStruct((H_B, ntok, LANES), BF16),
                        jax.ShapeDtypeStruct((H_B, LANES, ntok), BF16)])
        scratch = [pltpu.VMEM((1, LANES), F32)]
    else:
        tm = min(ntok, 512)
        n_t = ntok // tm
        grid = (1, n_t)
        tok = lambda b, t: (t, 0)
        tok4 = lambda b, t: (t, 0, 0)
        out_specs = ([pl.BlockSpec((tm, H_B, D_HB), tok4)] * 2 + [pl.BlockSpec((tm, H_B), tok)]
                     + [pl.BlockSpec((tm, hd), tok)] * 4)
        out_shape = ([jax.ShapeDtypeStruct((ntok, H_B, D_HB), F32)] * 2
                     + [jax.ShapeDtypeStruct((ntok, H_B), F32)]
                     + [jax.ShapeDtypeStruct((ntok, hd), F32)] * 4)
        scratch = [pltpu.VMEM((H_B * _head_pitch(tm), LANES), F32)] * 2
    return pl.pallas_call(
        functools.partial(_proj_kernel, aug=aug, sub_tiles=2 if tm % 512 == 0 else 1),
        grid=grid,
        in_specs=[pl.BlockSpec((tm, d), tok)] + [_const_spec(a.shape) for a in ins[1:]],
        out_specs=out_specs,
        out_shape=out_shape,
        scratch_shapes=scratch,
        compiler_params=pltpu.CompilerParams(
            dimension_semantics=("arbitrary", "arbitrary"), vmem_limit_bytes=VMEM_LIMIT),
        name="kvq_proj",
    )(*ins)


def _cumsum_kernel(lf_ref, f_ref):
    n_blk = lf_ref.shape[2] // LANES
    upper = (lax.broadcasted_iota(jnp.int32, (LANES, LANES), 0)
             <= lax.broadcasted_iota(jnp.int32, (LANES, LANES), 1)).astype(BF16)
    run = jnp.zeros((lf_ref.shape[1], 1), F32)
    for c in range(n_blk):
        x = lf_ref[0, :, c * LANES:(c + 1) * LANES]
        h1 = x.astype(BF16)
        r1 = x - h1.astype(F32)
        h2 = r1.astype(BF16)
        h3 = (r1 - h2.astype(F32)).astype(BF16)
        cs = (_dot(h1, upper) + _dot(h2, upper)) + _dot(h3, upper) + run
        f_ref[0, :, c * LANES:(c + 1) * LANES] = cs
        run = cs[:, LANES - 1:LANES]


def _cumsum_time(lf_t):
    b, h, l = lf_t.shape
    return pl.pallas_call(
        _cumsum_kernel,
        grid=(b,),
        in_specs=[pl.BlockSpec((1, h, l), lambda i: (i, 0, 0))],
        out_specs=pl.BlockSpec((1, h, l), lambda i: (i, 0, 0)),
        out_shape=jax.ShapeDtypeStruct((b, h, l), F32),
        compiler_params=pltpu.CompilerParams(dimension_semantics=("arbitrary",)),
        name="logf_cumsum",
    )(lf_t)


def _head_column(f_tile, head):
    lane = lax.broadcasted_iota(jnp.int32, f_tile.shape, 1)
    return jnp.sum(jnp.where(lane == head, f_tile, 0.0), axis=-1, keepdims=True)


def _attn_prompt_kernel(qa_ref, ka_ref, vat_ref, g_ref, o_ref, m_scr, acc_scr, *, tq):
    seq = qa_ref.shape[1]
    n_q = seq // tq
    half = tq // 2
    row_o = lax.broadcasted_iota(jnp.int32, (LANES, tq), 0)

    def visible(keys, queries):
        return (lax.broadcasted_iota(jnp.int32, (keys, queries), 0)
                <= lax.broadcasted_iota(jnp.int32, (keys, queries), 1))

    items = []
    for pair in range(qa_ref.shape[0] // 2):
        for qt in range(n_q):
            tile = (pair, qt)
            for kt in range(qt):
                items.append((tile, 0, tq, kt * tq, tq, None))
            items.append((tile, 0, tq, qt * tq, half, visible(half, tq)))
            items.append((tile, half, half, qt * tq + half, half, visible(half, half)))

    def scores(item):
        (pair, qt), q_lo, q_len, c0, c_len, _ = item
        return [_dot_nt(ka_ref[2 * pair + hf, c0:c0 + c_len, :],
                        qa_ref[2 * pair + hf, qt * tq + q_lo:qt * tq + q_lo + q_len, :])
                for hf in range(2)]

    def update(item, s_pair):
        (pair, _), q_lo, q_len, c0, c_len, mask = item
        cols = slice(q_lo, q_lo + q_len)
        for hf in range(2):
            st = s_pair[hf]
            if mask is not None:
                st = jnp.where(mask, st, -jnp.inf)
            m_old = m_scr[hf, :, cols]
            m_new = jnp.maximum(m_old, jnp.max(st, axis=0, keepdims=True))
            pt = jnp.exp2(st - m_new).astype(BF16)
            acc_scr[hf, :, cols] = (jnp.exp2(m_old - m_new) * acc_scr[hf, :, cols]
                                    + _dot(vat_ref[2 * pair + hf, :, c0:c0 + c_len], pt))
            m_scr[hf, :, cols] = m_new

    def finish(tile):
        pair, qt = tile
        rows, cols = slice(qt * tq, (qt + 1) * tq), slice(pair * LANES, (pair + 1) * LANES)
        outs = []
        for hf in range(2):
            acc = acc_scr[hf]
            outs.append(acc / acc[AUG_BASE[hf]:AUG_BASE[hf] + 1, :])
        o = jnp.where(row_o < D_HB, outs[0], outs[1]).T
        o = o * _sigmoid(g_ref[0, rows, cols])
        o_ref[0, rows, cols] = o.astype(BF16)

    s_next = scores(items[0])
    for i, item in enumerate(items):
        s_cur = s_next
        if i + 1 < len(items):
            s_next = scores(items[i + 1])
        if i == 0 or items[i - 1][0] != item[0]:
            m_scr[...] = jnp.full(m_scr.shape, -jnp.inf, F32)
            acc_scr[...] = jnp.zeros(acc_scr.shape, F32)
        update(item, s_cur)
        if i + 1 == len(items) or items[i + 1][0] != item[0]:
            finish(item[0])


def _attn_prompt(qa, ka, vat, g):
    b, t, hd = g.shape
    tq = min(t, 512)
    pairs = 4
    n_blk = hd // (pairs * LANES)
    heads = pl.BlockSpec((2 * pairs, t, LANES), lambda i, j: (j, i, 0))
    heads_t = pl.BlockSpec((2 * pairs, LANES, t), lambda i, j: (j, 0, i))
    col = pl.BlockSpec((1, t, pairs * LANES), lambda i, j: (i, 0, j))
    kern = functools.partial(_attn_prompt_kernel, tq=tq)
    return pl.pallas_call(
        kern,
        grid=(b, n_blk),
        in_specs=[heads, heads, heads_t, col],
        out_specs=col,
        out_shape=jax.ShapeDtypeStruct((b, t, hd), BF16),
        scratch_shapes=[pltpu.VMEM((2, 1, tq), F32), pltpu.VMEM((2, LANES, tq), F32)],
        compiler_params=pltpu.CompilerParams(
            dimension_semantics=("arbitrary", "arbitrary"), vmem_limit_bytes=VMEM_LIMIT),
        name="fox_attn_prompt",
    )(qa, ka, vat, g)


def _pair_block(qs, k2, v2, fq, fk, mask, m_ref, l_ref, acc_ref, keys_on_lanes=False):
    tq = qs.shape[0] // 2
    tk = k2.shape[1] if keys_on_lanes else k2.shape[0]
    width = min(tk, LANES)
    s = _dot(qs, k2) if keys_on_lanes else _dot_nt(qs, k2)
    alphas, probs = [], []
    for hf in range(2):
        sh = s[hf * tq:(hf + 1) * tq] + (fq[hf] - fk[hf])
        if mask is not None:
            sh = jnp.where(mask, sh, -jnp.inf)
        cols = [sh[:, c * width:(c + 1) * width] for c in range(tk // width)]
        m_old = m_ref[hf]
        m_new = jnp.maximum(
            m_old, jnp.max(functools.reduce(jnp.maximum, cols), axis=-1, keepdims=True))
        p_cols = [jnp.exp(c - m_new[:, :width]) for c in cols]
        alpha = jnp.exp(m_old - m_new)
        l_ref[hf] = alpha * l_ref[hf] + jnp.sum(
            functools.reduce(lambda a, b: a + b, p_cols), axis=-1, keepdims=True)
        m_ref[hf] = m_new
        probs.append(jnp.concatenate([p.astype(BF16) for p in p_cols], axis=1))
        alphas.append(alpha)
    p_all = jnp.concatenate(probs, axis=0)
    pv = _dot_nt(p_all, v2) if keys_on_lanes else _dot(p_all, v2)
    acc_ref[...] = jnp.concatenate(alphas, axis=0) * acc_ref[...] + pv


def _split_heads(q2, lane_a):
    zero = jnp.zeros_like(q2)
    return jnp.concatenate([jnp.where(lane_a, q2, zero), jnp.where(lane_a, zero, q2)], axis=0)


def _attn_sample_kernel(q_ref, g_ref, kc_ref, vc_ref, kn_ref, vn_ref, fq_ref, fkc_ref, fkn_ref,
                        o_ref, m_scr, l_scr, acc_scr, *, n_pairs):
    kt = pl.program_id(1)
    n_kt = pl.num_programs(1)
    tq = q_ref.shape[1]
    scale = D_HB ** -0.5
    lane_a = lax.broadcasted_iota(jnp.int32, (tq, LANES), 1) < D_HB
    causal = (lax.broadcasted_iota(jnp.int32, (tq, tq), 1)
              <= lax.broadcasted_iota(jnp.int32, (tq, tq), 0))
    f_tile = fq_ref[0]

    @pl.when(kt == 0)
    def _():
        m_scr[...] = jnp.full(m_scr.shape, -jnp.inf, F32)
        l_scr[...] = jnp.zeros(l_scr.shape, F32)
        acc_scr[...] = jnp.zeros(acc_scr.shape, F32)

    def cache_pair(ref, hp):
        return ref[0, 2 * hp:2 * hp + 2].reshape(2 * D_HB, ref.shape[3]).astype(BF16)

    def pair_inputs(hp):
        cols = slice(hp * LANES, (hp + 1) * LANES)
        qs = _split_heads(q_ref[0, :, cols] * scale, lane_a).astype(BF16)
        fq = [_head_column(f_tile, 2 * hp + hf) for hf in range(2)]
        return cols, qs, fq

    for hp in range(n_pairs):
        cols, qs, fq = pair_inputs(hp)
        fk = [fkc_ref[0, 2 * hp + hf:2 * hp + hf + 1, :] for hf in range(2)]
        _pair_block(qs, cache_pair(kc_ref, hp), cache_pair(vc_ref, hp),
                    fq, fk, None, m_scr.at[hp], l_scr.at[hp], acc_scr.at[hp], keys_on_lanes=True)

    @pl.when(kt == n_kt - 1)
    def _():
        for hp in range(n_pairs):
            cols, qs, fq = pair_inputs(hp)
            fk = [fkn_ref[0, 2 * hp + hf:2 * hp + hf + 1, 0:tq] for hf in range(2)]
            _pair_block(qs, kn_ref[0, :, cols].astype(BF16), vn_ref[0, :, cols].astype(BF16),
                        fq, fk, causal, m_scr.at[hp], l_scr.at[hp], acc_scr.at[hp])
            acc = acc_scr[hp]
            o = jnp.where(lane_a, acc[:tq] / l_scr[hp, 0], acc[tq:] / l_scr[hp, 1])
            o = o * _sigmoid(g_ref[0, :, cols])
            o_ref[0, :, cols] = o.astype(BF16)


def _attn_sample(q, g, k_cache, v_cache, k_new, v_new, f_new, f_t, past):
    b, t, hd = q.shape
    n_pairs = hd // LANES
    tk = min(past, 1024)
    n_kt = past // tk
    row = pl.BlockSpec((1, t, hd), lambda i, j: (i, 0, 0))
    cache = pl.BlockSpec((1, H_B, D_HB, tk), lambda i, j: (i, 0, 0, j))
    kern = functools.partial(_attn_sample_kernel, n_pairs=n_pairs)
    return pl.pallas_call(
        kern,
        grid=(b, n_kt),
        in_specs=[row, row, cache, cache, row, row,
                  pl.BlockSpec((1, t, H_B), lambda i, j: (i, 0, 0)),
                  pl.BlockSpec((1, H_B, tk), lambda i, j: (i, 0, j)),
                  pl.BlockSpec((1, H_B, LANES), lambda i, j: (i, 0, past // LANES))],
        out_specs=row,
        out_shape=jax.ShapeDtypeStruct((b, t, hd), BF16),
        scratch_shapes=[pltpu.VMEM((n_pairs, 2, t, LANES), F32), pltpu.VMEM((n_pairs, 2, t, LANES), F32),
                        pltpu.VMEM((n_pairs, 2 * t, LANES), F32)],
        compiler_params=pltpu.CompilerParams(
            dimension_semantics=("arbitrary", "arbitrary"), vmem_limit_bytes=VMEM_LIMIT),
        name="fox_attn_sample",
    )(q, g, k_cache, v_cache, k_new, v_new, f_new, f_t, f_t)


def _trunk(x, state0, k_past, v_past, logf_past, w):
    batch, seq, d = x.shape
    ntok = batch * seq
    hd = H_B * D_HB
    x2d = x.reshape(ntok, d)
    s0 = None if state0 is None else state0[:, 0]
    o_a, state = _hgrn(x2d, batch, seq, w["norm_a"], w["w_in"], w["lb_logits_h"], w["g_norm_a"], s0)
    x2d = _post(x2d, o_a, w["w_o_a"], w["norm_mlp0"], w["w_up"], w["w_down"], 0)
    proj_w = (w["norm_kv"], w["norm_b"], w["w_k"], w["w_v"], w["w_f"], w["b_f"], w["w_qq"], w["w_qg"])
    if k_past is None:
        k_t, v_t, logf_t, g, qa, ka, va = _proj(x2d, batch, seq, True, *proj_w)
        k, v = jnp.transpose(k_t, (0, 3, 1, 2)), jnp.transpose(v_t, (0, 3, 1, 2))
        logf = jnp.swapaxes(logf_t, 1, 2)
        o_b = _attn_prompt(qa, ka, va, g.reshape(batch, seq, hd))
    else:
        k, v, logf, g, q, k_flat, v_flat = _proj(x2d, batch, seq, False, *proj_w)
        past = k_past.shape[1]
        total = past + seq
        padded = -(-total // LANES) * LANES
        lf_all = jnp.concatenate(
            [jnp.swapaxes(logf_past.astype(F32), 1, 2),
             jnp.swapaxes(logf.reshape(batch, seq, H_B), 1, 2),
             jnp.zeros((batch, H_B, padded - total), F32)], axis=2)
        f_t = _cumsum_time(lf_all.reshape(1, batch * H_B, padded)).reshape(batch, H_B, padded)
        o_b = _attn_sample(q.reshape(batch, seq, hd), g.reshape(batch, seq, hd),
                           jnp.transpose(k_past, (0, 2, 3, 1)).astype(F32),
                           jnp.transpose(v_past, (0, 2, 3, 1)).astype(F32),
                           k_flat.reshape(batch, seq, hd), v_flat.reshape(batch, seq, hd),
                           jnp.swapaxes(f_t[:, :, past:total], 1, 2), f_t, past)
    y = _post(x2d, o_b.reshape(ntok, hd), w["w_o_b"], w["norm_mlp1"], w["w_up"], w["w_down"], 1,
              w["norm_f"])
    return (y.reshape(batch, seq, d), state[:, None],
            k.reshape(batch, seq, H_B, D_HB), v.reshape(batch, seq, H_B, D_HB),
            logf.reshape(batch, seq, H_B))


def kernel(x_prompt, x_sample, state_hgrn, cache_k, cache_v, cache_logf, norm_a, w_in_a, lb_logits, g_norm_a, w_o_a, norm_kv, w_kv, b_f, norm_b, w_q_b, w_o_b, norm_mlp, w_up, w_down, norm_f):
    d = x_prompt.shape[-1]
    assert w_in_a.shape[0] == 1 and w_q_b.shape[0] == 1, "one HGRN2 layer and one FoX layer"
    n_heads = w_in_a.shape[2] // (4 * K_A)
    hd = H_B * D_HB
    row = lambda a: a.reshape(1, -1).astype(F32)
    w_f = jnp.zeros((d, LANES), F32).at[:, :H_B].set(w_kv[:, 2 * hd:])
    w = {
        "norm_a": row(norm_a[0]),
        "w_in": w_in_a[0].astype(BF16),
        "lb_logits_h": lb_logits.astype(F32).reshape(lb_logits.shape[0], n_heads, 1, K_A),
        "g_norm_a": row(g_norm_a[0]),
        "w_o_a": w_o_a[0].astype(BF16),
        "norm_mlp0": row(norm_mlp[0]), "norm_mlp1": row(norm_mlp[1]),
        "w_up": w_up.astype(BF16), "w_down": w_down.astype(BF16),
        "norm_kv": row(norm_kv), "norm_b": row(norm_b[0]),
        "w_k": w_kv[:, :hd].astype(BF16), "w_v": w_kv[:, hd:2 * hd].astype(BF16),
        "w_f": w_f.astype(BF16), "b_f": jnp.zeros((1, LANES), F32).at[0, :H_B].set(b_f.astype(F32)),
        "w_qq": w_q_b[0][:, :hd].astype(BF16), "w_qg": w_q_b[0][:, hd:].astype(BF16),
        "w_o_b": w_o_b[0].astype(BF16),
        "norm_f": row(norm_f),
    }
    y_p, st_p, k_p, v_p, lf_p = _trunk(x_prompt, None, None, None, None, w)
    y_s, st_s, k_s, v_s, lf_s = _trunk(x_sample, state_hgrn, cache_k, cache_v, cache_logf, w)
    return (y_p, y_s, st_p, k_p, v_p, lf_p, st_s, k_s, v_s, lf_s)
```

```python
import functools

import jax
import jax.numpy as jnp
from jax import lax
from jax.experimental import pallas as pl
from jax.experimental.pallas import tpu as pltpu

EPS = 1e-6
CHUNK = 64
K_A = 128
V_A = 128
H_B = 16
D_HB = 64
LANES = 128
VMEM_LIMIT = 56 * 1024 * 1024

F32 = jnp.float32
BF16 = jnp.bfloat16


def _dot(a, b):
    return jnp.dot(a, b, preferred_element_type=F32)


def _dot_nt(a, b):
    return lax.dot_general(a, b, (((1,), (1,)), ((), ())), preferred_element_type=F32)


def _dot_tn(a, b):
    return lax.dot_general(a, b, (((0,), (0,)), ((), ())), preferred_element_type=F32)


def _sigmoid(x):
    return 0.5 * jnp.tanh(0.5 * x) + 0.5


def _const_spec(shape):
    nd = len(shape)
    return pl.BlockSpec(shape, lambda *_: (0,) * nd, pipeline_mode=pl.Buffered(1))


def _hgrn_kernel(*refs, n_heads, n_chunks, carry):
    if carry:
        (x_ref, nrm_ref, win_ref, lbl_ref, gn_ref, o_ref, sout_ref, xn_scr, st_scr) = refs
        s0_ref = None
    else:
        (x_ref, nrm_ref, win_ref, lbl_ref, gn_ref, s0_ref, o_ref, sout_ref, xn_scr) = refs
        st_scr = None
    tm = n_chunks * CHUNK
    n_sub = x_ref.shape[0] // tm

    def normalise(sub):
        x = x_ref[sub * tm:(sub + 1) * tm, :]
        inv = lax.rsqrt(jnp.mean(x * x, axis=-1, keepdims=True) + EPS)
        xn_scr[sub] = (x * inv * nrm_ref[...]).astype(BF16)

    if carry:
        @pl.when(pl.program_id(1) == 0)
        def _():
            st_scr[...] = jnp.zeros_like(st_scr)

    row_in_chunk = lax.broadcasted_iota(jnp.int32, (tm, K_A), 0) % CHUNK
    tri = (lax.broadcasted_iota(jnp.int32, (CHUNK, CHUNK), 1)
           <= lax.broadcasted_iota(jnp.int32, (CHUNK, CHUNK), 0))

    d_a = n_heads * K_A

    def project(sub, pair):
        xn = xn_scr[sub]
        return [_dot(xn, win_ref[:, part * d_a + pair * 2 * K_A:part * d_a + (pair + 1) * 2 * K_A])
                for part in range(4)]

    def gates(h, p4):
        half = slice((h % 2) * K_A, (h % 2 + 1) * K_A)
        pq, pz, pi, pg = (p[:, half] for p in p4)
        n_rows = lbl_ref.shape[0]
        lrows = [lbl_ref[r, h] for r in range(n_rows)]
        lmax = functools.reduce(jnp.maximum, lrows)
        lexp = [jnp.exp(l - lmax) for l in lrows]
        lb = lexp[0] / functools.reduce(lambda a, b: a + b, lexp)

        q = pq * _sigmoid(pq)
        f = lb + (1.0 - lb) * _sigmoid(pz)
        logf = jnp.log(f)
        k = 1.0 - f
        bc = logf
        s = 1
        while s < CHUNK:
            bc = bc + jnp.where(row_in_chunk >= s, pltpu.roll(bc, s, axis=0), 0.0)
            s *= 2
        chunked = lambda a: a.reshape(n_chunks, CHUNK, K_A)
        bc3 = chunked(bc)
        btot = bc3[:, CHUNK - 1:CHUNK, :]
        q_dec = chunked(q * jnp.exp(bc)).astype(BF16)
        k_inv = chunked(k * jnp.exp(-bc)).astype(BF16)
        k_end = (chunked(k) * jnp.exp(btot - bc3)).astype(BF16)
        decay = jnp.exp(btot)
        v = chunked(pi).astype(BF16)
        return q_dec, k_inv, k_end, decay, v, _sigmoid(pg)

    def recur(sub, heads, staged_heads):
        rows = slice(sub * tm, (sub + 1) * tm)
        cat = lambda i: jnp.concatenate([s[i] for s in staged_heads], axis=0)
        q_dec, k_inv, k_end, v = cat(0), cat(1), cat(2), cat(4)
        sc = jnp.einsum('cqk,csk->cqs', q_dec, k_inv, preferred_element_type=F32)
        sc = jnp.where(tri, sc, 0.0).astype(BF16)
        o_intra = jnp.einsum('cqs,csv->cqv', sc, v, preferred_element_type=F32)
        upd = jnp.einsum('csv,csk->cvk', v, k_end, preferred_element_type=F32)
        states = []
        for i, h in enumerate(heads):
            decay = staged_heads[i][3]
            if carry:
                st = st_scr[h]
                for c in range(n_chunks):
                    states.append(st)
                    st = st * decay[c] + upd[i * n_chunks + c]
                st_scr[h] = st
                sout_ref[0, h] = st.T
            else:
                for c in range(n_chunks):
                    st = s0_ref[c, h].T
                    states.append(st)
                    sout_ref[c, h] = (st * decay[c] + upd[i * n_chunks + c]).T
        s_in = jnp.stack(states, axis=0).astype(BF16)
        o_inter = jnp.einsum('cqk,cvk->cqv', q_dec, s_in, preferred_element_type=F32)
        o_all = o_intra + o_inter
        for i, h in enumerate(heads):
            o = o_all[i * n_chunks:(i + 1) * n_chunks].reshape(tm, V_A)
            o = o * lax.rsqrt(jnp.mean(o * o, axis=-1, keepdims=True) + EPS) * gn_ref[...]
            o_ref[rows, h * V_A:(h + 1) * V_A] = (o * staged_heads[i][5]).astype(BF16)

    items = [(sub, pair) for sub in range(n_sub) for pair in range(n_heads // 2)]
    normalise(0)
    p_next = project(*items[0])
    staged = None
    for i, (sub, pair) in enumerate(items):
        p_cur = p_next
        if i + 1 < len(items):
            if items[i + 1][0] != sub:
                normalise(items[i + 1][0])
            p_next = project(*items[i + 1])
        if staged is not None:
            recur(*staged)
        staged = (sub, (2 * pair, 2 * pair + 1), [gates(2 * pair, p_cur), gates(2 * pair + 1, p_cur)])
    recur(*staged)


def _hgrn(x2d, batch, seq, norm_w, w_in, lb_logits_h, g_norm, s0):
    ntok, d = x2d.shape
    n_heads = w_in.shape[1] // (4 * K_A)
    carry = s0 is None
    if carry:
        sub = min(seq, 512)
        tm = min(seq, 2 * sub)
        n_t = seq // tm
        grid = (batch, n_t)
        tok_map = lambda b, t: (b * n_t + t, 0)
        st_spec = pl.BlockSpec((1, n_heads, K_A, V_A), lambda b, t: (b, 0, 0, 0))
        scratch = [pltpu.VMEM((tm // sub, sub, d), BF16), pltpu.VMEM((n_heads, V_A, K_A), F32)]
        extra_in, extra_specs = [], []
    else:
        assert seq == CHUNK
        bt = min(batch, 8)
        sub = tm = bt * CHUNK
        grid = (batch // bt, 1)
        tok_map = lambda b, t: (b, 0)
        st_spec = pl.BlockSpec((bt, n_heads, K_A, V_A), lambda b, t: (b, 0, 0, 0))
        scratch = [pltpu.VMEM((1, tm, d), BF16)]
        extra_in, extra_specs = [s0], [st_spec]
    n_chunks = sub // CHUNK
    kern = functools.partial(_hgrn_kernel, n_heads=n_heads, n_chunks=n_chunks, carry=carry)
    return pl.pallas_call(
        kern,
        grid=grid,
        in_specs=[pl.BlockSpec((tm, d), tok_map),
                  _const_spec(norm_w.shape), _const_spec(w_in.shape),
                  _const_spec(lb_logits_h.shape), _const_spec(g_norm.shape)] + extra_specs,
        out_specs=[pl.BlockSpec((tm, n_heads * V_A), tok_map), st_spec],
        out_shape=[jax.ShapeDtypeStruct((ntok, n_heads * V_A), BF16),
                   jax.ShapeDtypeStruct((batch, n_heads, K_A, V_A), F32)],
        scratch_shapes=scratch,
        compiler_params=pltpu.CompilerParams(
            dimension_semantics=("arbitrary", "arbitrary"), vmem_limit_bytes=VMEM_LIMIT),
        name="hgrn_mixer",
    )(x2d, norm_w, w_in, lb_logits_h, g_norm, *extra_in)


def _post_kernel(*refs, ff_block, final_norm):
    if final_norm:
        x_ref, o_ref, wo_ref, nm_ref, wup_ref, wdn_ref, nf_ref, y_ref = refs
    else:
        x_ref, o_ref, wo_ref, nm_ref, wup_ref, wdn_ref, y_ref = refs
    x1 = x_ref[...] + _dot(o_ref[...], wo_ref[...])
    inv = lax.rsqrt(jnp.mean(x1 * x1, axis=-1, keepdims=True) + EPS)
    xn = (x1 * inv * nm_ref[...]).astype(BF16)
    acc = x1
    d_ff = wup_ref.shape[1]
    for j in range(d_ff // ff_block):
        hcol = jnp.maximum(_dot(xn, wup_ref[:, j * ff_block:(j + 1) * ff_block]), 0.0)
        acc = acc + _dot((hcol * hcol).astype(BF16), wdn_ref[j * ff_block:(j + 1) * ff_block, :])
    if final_norm:
        inv = lax.rsqrt(jnp.mean(acc * acc, axis=-1, keepdims=True) + EPS)
        acc = acc * inv * nf_ref[...]
    y_ref[...] = acc


def _layer_spec(stacked, layer):
    return pl.BlockSpec((None,) + stacked.shape[1:], lambda *_: (layer, 0, 0),
                        pipeline_mode=pl.Buffered(1))


def _post(x2d, o2d, w_o, norm_mlp, w_up, w_down, layer, norm_f=None):
    ntok, d = x2d.shape
    tm = min(ntok, 1024)
    final_norm = norm_f is not None
    tok = lambda i: (i, 0)
    ins = [x2d, o2d, w_o, norm_mlp, w_up, w_down] + ([norm_f] if final_norm else [])
    specs = [pl.BlockSpec((tm, d), tok), pl.BlockSpec((tm, o2d.shape[1]), tok),
             _const_spec(w_o.shape), _const_spec(norm_mlp.shape),
             _layer_spec(w_up, layer), _layer_spec(w_down, layer)]
    specs += [_const_spec(norm_f.shape)] if final_norm else []
    kern = functools.partial(_post_kernel, ff_block=1024, final_norm=final_norm)
    return pl.pallas_call(
        kern,
        grid=(ntok // tm,),
        in_specs=specs,
        out_specs=pl.BlockSpec((tm, d), tok),
        out_shape=jax.ShapeDtypeStruct((ntok, d), F32),
        compiler_params=pltpu.CompilerParams(
            dimension_semantics=("arbitrary",), vmem_limit_bytes=VMEM_LIMIT),
        name="post_mlp",
    )(*ins)


LOG2E = 1.4426950408889634
AUG_BASE = (D_HB, 0)


def _split3(x):
    h1 = x.astype(BF16).astype(F32)
    r1 = x - h1
    h2 = r1.astype(BF16).astype(F32)
    h3 = (r1 - h2).astype(BF16).astype(F32)
    return h1, h2, h3


def _head_pitch(tm):
    return tm + 8 if (tm // 8) % 2 == 0 else tm


def _store_head_major(val, scr, out_ref):
    tm = val.shape[0]
    pitch = _head_pitch(tm)
    for h in range(H_B):
        pair = val[:, (h // 2) * LANES:(h // 2 + 1) * LANES]
        scr[h * pitch:h * pitch + tm, :] = pair if h % 2 == 0 else pltpu.roll(pair, D_HB, axis=1)

    for t in range(tm):
        for grp in range(H_B // 8):
            rows = scr[pl.ds(grp * 8 * pitch + t, 8, stride=pitch), :]
            out_ref[t, grp * 8:(grp + 1) * 8, :] = rows[:, :D_HB]


def _proj_kernel(*refs, aug, sub_tiles):
    if aug:
        (x_ref, nkv_ref, nb_ref, wk_ref, wv_ref, wf_ref, bf_ref, wq_ref, wg_ref,
         k_ref, v_ref, lf_ref, g_ref, qa_ref, ka_ref, va_ref, carry) = refs
    else:
        (x_ref, nkv_ref, nb_ref, wk_ref, wv_ref, wf_ref, bf_ref, wq_ref, wg_ref,
         k_ref, v_ref, lf_ref, g_ref, q_ref, kf_ref, vf_ref, k_scr, v_scr) = refs
    tm = x_ref.shape[0]
    if not aug:
        x = x_ref[...]
        xs = x * lax.rsqrt(jnp.mean(x * x, axis=-1, keepdims=True) + EPS)
        xk = (xs * nkv_ref[...]).astype(BF16)
        xq = (xs * nb_ref[...]).astype(BF16)
        z = _dot(xk, wf_ref[...]) + bf_ref[...]
        lf = jnp.minimum(z, 0.0) - jnp.log(1.0 + jnp.exp(-jnp.abs(z)))
        k = _dot(xk, wk_ref[...])
        v = _dot(xk, wv_ref[...])
        g_ref[...] = _dot(xq, wg_ref[...])
        _store_head_major(k, k_scr, k_ref)
        _store_head_major(v, v_scr, v_ref)
        lf_ref[...] = lf[:, :H_B]
        q_ref[...] = _dot(xq, wq_ref[...])
        kf_ref[...] = k
        vf_ref[...] = v
        return

    @pl.when(pl.program_id(1) == 0)
    def _():
        carry[...] = jnp.zeros_like(carry)

    ts = tm // sub_tiles
    row = lax.broadcasted_iota(jnp.int32, (ts, LANES), 0)
    lane = lax.broadcasted_iota(jnp.int32, (ts, LANES), 1)
    row_t = lax.broadcasted_iota(jnp.int32, (LANES, ts), 0)
    low = lane < D_HB

    def sub_tile(r0, f_in):
        rows = slice(r0, r0 + ts)
        x = x_ref[rows, :]
        xs = x * lax.rsqrt(jnp.mean(x * x, axis=-1, keepdims=True) + EPS)
        xk = (xs * nkv_ref[...]).astype(BF16)
        xq = (xs * nb_ref[...]).astype(BF16)
        z = _dot(xk, wf_ref[...]) + bf_ref[...]
        lf = jnp.minimum(z, 0.0) - jnp.log(1.0 + jnp.exp(-jnp.abs(z)))
        lf_ref[0, :, rows] = lf.T[:H_B, :]
        f = lf
        s = 1
        while s < ts:
            f = f + jnp.where(row >= s, pltpu.roll(f, s, axis=0), 0.0)
            s *= 2
        f = f + f_in
        h1, h2, h3 = _split3(f * LOG2E)
        pieces = jnp.where(lane < H_B, h1,
                           jnp.where(lane < 2 * H_B, pltpu.roll(h2, H_B, axis=1),
                                     jnp.where(lane < 3 * H_B, pltpu.roll(h3, 2 * H_B, axis=1), 0.0)))
        both = pieces + pltpu.roll(pieces, D_HB, axis=1)
        shifted = pltpu.roll(both, 1, axis=1)
        neg_both = -both

        def tails(h):
            base = AUG_BASE[h % 2]
            at = (lane == base + h) | (lane == base + H_B + h) | (lane == base + 2 * H_B + h)
            up = ((lane == base + h + 1) | (lane == base + H_B + h + 1)
                  | (lane == base + 2 * H_B + h + 1))
            q_tail = jnp.where(up, shifted, jnp.where(at, 1.0, 0.0))
            k_tail = jnp.where(at, neg_both, jnp.where(up, 1.0, 0.0))
            return q_tail, k_tail

        def put(ref, h, data, tail):
            ref[h, rows, :] = (jnp.where(low, data, tail) if h % 2 == 0
                               else jnp.where(low, tail, data)).astype(BF16)

        k = _dot(xk, wk_ref[...])
        k_ref[0, :, :, rows] = k.T.reshape(H_B, D_HB, ts)
        for h in range(H_B):
            put(ka_ref, h, k[:, (h // 2) * LANES:(h // 2 + 1) * LANES], tails(h)[1])

        v_t = _dot(xk, wv_ref[...]).T
        v_ref[0, :, :, rows] = v_t.reshape(H_B, D_HB, ts)
        for j in range(H_B // 2):
            v_pair = v_t[j * LANES:(j + 1) * LANES, :]
            va_ref[2 * j, :, rows] = jnp.where(row_t < D_HB, v_pair,
                                               jnp.where(row_t == AUG_BASE[0], 1.0, 0.0)).astype(BF16)
            va_ref[2 * j + 1, :, rows] = jnp.where(row_t < D_HB,
                                                   jnp.where(row_t == AUG_BASE[1], 1.0, 0.0),
                                                   v_pair).astype(BF16)

        q = _dot(xq, wq_ref[...]) * ((D_HB ** -0.5) * LOG2E)
        for h in range(H_B):
            put(qa_ref, h, q[:, (h // 2) * LANES:(h // 2 + 1) * LANES], tails(h)[0])

        g_ref[rows, :] = _dot(xq, wg_ref[...])
        return f[ts - 1:ts, :]

    f_run = carry[...]
    for i in range(sub_tiles):
        f_run = sub_tile(i * ts, f_run)
    carry[...] = f_run


def _proj(x2d, batch, seq, aug, norm_kv, norm_b, w_k, w_v, w_f, b_f, w_qq, w_qg):
    ntok, d = x2d.shape
    hd = w_k.shape[1]
    ins = [x2d, norm_kv, norm_b, w_k, w_v, w_f, b_f, w_qq, w_qg]
    if aug:
        tm = min(seq, 512)
        n_t = seq // tm
        grid = (batch, n_t)
        tok = lambda b, t: (b * n_t + t, 0)
        head_major = lambda b, t: (0, b * n_t + t, 0)
        time_minor = pl.BlockSpec((1, H_B, D_HB, tm), lambda b, t: (b, 0, 0, t))
        out_specs = [time_minor, time_minor, pl.BlockSpec((1, H_B, tm), lambda b, t: (b, 0, t)),
                     pl.BlockSpec((tm, hd), tok),
                     pl.BlockSpec((H_B, tm, LANES), head_major),
                     pl.BlockSpec((H_B, tm, LANES), head_major),
                     pl.BlockSpec((H_B, LANES, tm), lambda b, t: (0, 0, b * n_t + t))]
        out_shape = ([jax.ShapeDtypeStruct((batch, H_B, D_HB, seq), F32)] * 2
                     + [jax.ShapeDtypeStruct((batch, H_B, seq), F32),
                        jax.ShapeDtypeStruct((ntok, hd), F32),
                        jax.ShapeDtypeStruct((H_B, ntok, LANES), BF16),
                        jax.ShapeDtypeStruct((H_B, ntok, LANES), BF16),
                        jax.ShapeDtypeStruct((H_B, LANES, ntok), BF16)])
        scratch = [pltpu.VMEM((1, LANES), F32)]
    else:
        tm = min(ntok, 512)
        n_t = ntok // tm
        grid = (1, n_t)
        tok = lambda b, t: (t, 0)
        tok4 = lambda b, t: (t, 0, 0)
        out_specs = ([pl.BlockSpec((tm, H_B, D_HB), tok4)] * 2 + [pl.BlockSpec((tm, H_B), tok)]
                     + [pl.BlockSpec((tm, hd), tok)] * 4)
        out_shape = ([jax.ShapeDtypeStruct((ntok, H_B, D_HB), F32)] * 2
                     + [jax.ShapeDtypeStruct((ntok, H_B), F32)]
                     + [jax.ShapeDtypeStruct((ntok, hd), F32)] * 4)
        scratch = [pltpu.VMEM((H_B * _head_pitch(tm), LANES), F32)] * 2
    return pl.pallas_call(
        functools.partial(_proj_kernel, aug=aug, sub_tiles=2 if tm % 512 == 0 else 1),
        grid=grid,
        in_specs=[pl.BlockSpec((tm, d), tok)] + [_const_spec(a.shape) for a in ins[1:]],
        out_specs=out_specs,
        out_shape=out_shape,
        scratch_shapes=scratch,
        compiler_params=pltpu.CompilerParams(
            dimension_semantics=("arbitrary", "arbitrary"), vmem_limit_bytes=VMEM_LIMIT),
        name="kvq_proj",
    )(*ins)


def _cumsum_kernel(lf_ref, f_ref):
    n_blk = lf_ref.shape[2] // LANES
    upper = (lax.broadcasted_iota(jnp.int32, (LANES, LANES), 0)
             <= lax.broadcasted_iota(jnp.int32, (LANES, LANES), 1)).astype(BF16)
    run = jnp.zeros((lf_ref.shape[1], 1), F32)
    for c in range(n_blk):
        x = lf_ref[0, :, c * LANES:(c + 1) * LANES]
        h1 = x.astype(BF16)
        r1 = x - h1.astype(F32)
        h2 = r1.astype(BF16)
        h3 = (r1 - h2.astype(F32)).astype(BF16)
        cs = (_dot(h1, upper) + _dot(h2, upper)) + _dot(h3, upper) + run
        f_ref[0, :, c * LANES:(c + 1) * LANES] = cs
        run = cs[:, LANES - 1:LANES]


def _cumsum_time(lf_t):
    b, h, l = lf_t.shape
    return pl.pallas_call(
        _cumsum_kernel,
        grid=(b,),
        in_specs=[pl.BlockSpec((1, h, l), lambda i: (i, 0, 0))],
        out_specs=pl.BlockSpec((1, h, l), lambda i: (i, 0, 0)),
        out_shape=jax.ShapeDtypeStruct((b, h, l), F32),
        compiler_params=pltpu.CompilerParams(dimension_semantics=("arbitrary",)),
        name="logf_cumsum",
    )(lf_t)


def _head_column(f_tile, head):
    lane = lax.broadcasted_iota(jnp.int32, f_tile.shape, 1)
    return jnp.sum(jnp.where(lane == head, f_tile, 0.0), axis=-1, keepdims=True)


def _attn_prompt_kernel(qa_ref, ka_ref, vat_ref, g_ref, o_ref, m_scr, acc_scr, *, tq):
    seq = qa_ref.shape[1]
    n_q = seq // tq
    half = tq // 2
    row_o = lax.broadcasted_iota(jnp.int32, (LANES, tq), 0)

    def visible(keys, queries):
        return (lax.broadcasted_iota(jnp.int32, (keys, queries), 0)
                <= lax.broadcasted_iota(jnp.int32, (keys, queries), 1))

    items = []
    for pair in range(qa_ref.shape[0] // 2):
        for qt in range(n_q):
            tile = (pair, qt)
            for kt in range(qt):
                items.append((tile, 0, tq, kt * tq, tq, None))
            items.append((tile, 0, tq, qt * tq, half, visible(half, tq)))
            items.append((tile, half, half, qt * tq + half, half, visible(half, half)))

    def scores(item):
        (pair, qt), q_lo, q_len, c0, c_len, _ = item
        return [_dot_nt(ka_ref[2 * pair + hf, c0:c0 + c_len, :],
                        qa_ref[2 * pair + hf, qt * tq + q_lo:qt * tq + q_lo + q_len, :])
                for hf in range(2)]

    def update(item, s_pair):
        (pair, _), q_lo, q_len, c0, c_len, mask = item
        cols = slice(q_lo, q_lo + q_len)
        for hf in range(2):
            st = s_pair[hf]
            if mask is not None:
                st = jnp.where(mask, st, -jnp.inf)
            m_old = m_scr[hf, :, cols]
            m_new = jnp.maximum(m_old, jnp.max(st, axis=0, keepdims=True))
            pt = jnp.exp2(st - m_new).astype(BF16)
            acc_scr[hf, :, cols] = (jnp.exp2(m_old - m_new) * acc_scr[hf, :, cols]
                                    + _dot(vat_ref[2 * pair + hf, :, c0:c0 + c_len], pt))
            m_scr[hf, :, cols] = m_new

    def finish(tile):
        pair, qt = tile
        rows, cols = slice(qt * tq, (qt + 1) * tq), slice(pair * LANES, (pair + 1) * LANES)
        outs = []
        for hf in range(2):
            acc = acc_scr[hf]
            outs.append(acc / acc[AUG_BASE[hf]:AUG_BASE[hf] + 1, :])
        o = jnp.where(row_o < D_HB, outs[0], outs[1]).T
        o = o * _sigmoid(g_ref[0, rows, cols])
        o_ref[0, rows, cols] = o.astype(BF16)

    s_next = scores(items[0])
    for i, item in enumerate(items):
        s_cur = s_next
        if i + 1 < len(items):
            s_next = scores(items[i + 1])
        if i == 0 or items[i - 1][0] != item[0]:
            m_scr[...] = jnp.full(m_scr.shape, -jnp.inf, F32)
            acc_scr[...] = jnp.zeros(acc_scr.shape, F32)
        update(item, s_cur)
        if i + 1 == len(items) or items[i + 1][0] != item[0]:
            finish(item[0])


def _attn_prompt(qa, ka, vat, g):
    b, t, hd = g.shape
    tq = min(t, 512)
    pairs = 4
    n_blk = hd // (pairs * LANES)
    heads = pl.BlockSpec((2 * pairs, t, LANES), lambda i, j: (j, i, 0))
    heads_t = pl.BlockSpec((2 * pairs, LANES, t), lambda i, j: (j, 0, i))
    col = pl.BlockSpec((1, t, pairs * LANES), lambda i, j: (i, 0, j))
    kern = functools.partial(_attn_prompt_kernel, tq=tq)
    return pl.pallas_call(
        kern,
        grid=(b, n_blk),
        in_specs=[heads, heads, heads_t, col],
        out_specs=col,
        out_shape=jax.ShapeDtypeStruct((b, t, hd), BF16),
        scratch_shapes=[pltpu.VMEM((2, 1, tq), F32), pltpu.VMEM((2, LANES, tq), F32)],
        compiler_params=pltpu.CompilerParams(
            dimension_semantics=("arbitrary", "arbitrary"), vmem_limit_bytes=VMEM_LIMIT),
        name="fox_attn_prompt",
    )(qa, ka, vat, g)


def _pair_block(qs, k2, v2, fq, fk, mask, m_ref, l_ref, acc_ref, keys_on_lanes=False):
    tq = qs.shape[0] // 2
    tk = k2.shape[1] if keys_on_lanes else k2.shape[0]
    width = min(tk, LANES)
    s = _dot(qs, k2) if keys_on_lanes else _dot_nt(qs, k2)
    alphas, probs = [], []
    for hf in range(2):
        sh = s[hf * tq:(hf + 1) * tq] + (fq[hf] - fk[hf])
        if mask is not None:
            sh = jnp.where(mask, sh, -jnp.inf)
        cols = [sh[:, c * width:(c + 1) * width] for c in range(tk // width)]
        m_old = m_ref[hf]
        m_new = jnp.maximum(
            m_old, jnp.max(functools.reduce(jnp.maximum, cols), axis=-1, keepdims=True))
        p_cols = [jnp.exp(c - m_new[:, :width]) for c in cols]
        alpha = jnp.exp(m_old - m_new)
        l_ref[hf] = alpha * l_ref[hf] + jnp.sum(
            functools.reduce(lambda a, b: a + b, p_cols), axis=-1, keepdims=True)
        m_ref[hf] = m_new
        probs.append(jnp.concatenate([p.astype(BF16) for p in p_cols], axis=1))
        alphas.append(alpha)
    p_all = jnp.concatenate(probs, axis=0)
    pv = _dot_nt(p_all, v2) if keys_on_lanes else _dot(p_all, v2)
    acc_ref[...] = jnp.concatenate(alphas, axis=0) * acc_ref[...] + pv


def _split_heads(q2, lane_a):
    zero = jnp.zeros_like(q2)
    return jnp.concatenate([jnp.where(lane_a, q2, zero), jnp.where(lane_a, zero, q2)], axis=0)


def _attn_sample_kernel(q_ref, g_ref, kc_ref, vc_ref, kn_ref, vn_ref, fq_ref, fkc_ref, fkn_ref,
                        o_ref, m_scr, l_scr, acc_scr, *, n_pairs):
    kt = pl.program_id(1)
    n_kt = pl.num_programs(1)
    tq = q_ref.shape[1]
    scale = D_HB ** -0.5
    lane_a = lax.broadcasted_iota(jnp.int32, (tq, LANES), 1) < D_HB
    causal = (lax.broadcasted_iota(jnp.int32, (tq, tq), 1)
              <= lax.broadcasted_iota(jnp.int32, (tq, tq), 0))
    f_tile = fq_ref[0]

    @pl.when(kt == 0)
    def _():
        m_scr[...] = jnp.full(m_scr.shape, -jnp.inf, F32)
        l_scr[...] = jnp.zeros(l_scr.shape, F32)
        acc_scr[...] = jnp.zeros(acc_scr.shape, F32)

    def cache_pair(ref, hp):
        return ref[0, 2 * hp:2 * hp + 2].reshape(2 * D_HB, ref.shape[3]).astype(BF16)

    def pair_inputs(hp):
        cols = slice(hp * LANES, (hp + 1) * LANES)
        qs = _split_heads(q_ref[0, :, cols] * scale, lane_a).astype(BF16)
        fq = [_head_column(f_tile, 2 * hp + hf) for hf in range(2)]
        return cols, qs, fq

    for hp in range(n_pairs):
        cols, qs, fq = pair_inputs(hp)
        fk = [fkc_ref[0, 2 * hp + hf:2 * hp + hf + 1, :] for hf in range(2)]
        _pair_block(qs, cache_pair(kc_ref, hp), cache_pair(vc_ref, hp),
                    fq, fk, None, m_scr.at[hp], l_scr.at[hp], acc_scr.at[hp], keys_on_lanes=True)

    @pl.when(kt == n_kt - 1)
    def _():
        for hp in range(n_pairs):
            cols, qs, fq = pair_inputs(hp)
            fk = [fkn_ref[0, 2 * hp + hf:2 * hp + hf + 1, 0:tq] for hf in range(2)]
            _pair_block(qs, kn_ref[0, :, cols].astype(BF16), vn_ref[0, :, cols].astype(BF16),
                        fq, fk, causal, m_scr.at[hp], l_scr.at[hp], acc_scr.at[hp])
            acc = acc_scr[hp]
            o = jnp.where(lane_a, acc[:tq] / l_scr[hp, 0], acc[tq:] / l_scr[hp, 1])
            o = o * _sigmoid(g_ref[0, :, cols])
            o_ref[0, :, cols] = o.astype(BF16)


def _attn_sample(q, g, k_cache, v_cache, k_new, v_new, f_new, f_t, past):
    b, t, hd = q.shape
    n_pairs = hd // LANES
    tk = min(past, 1024)
    n_kt = past // tk
    row = pl.BlockSpec((1, t, hd), lambda i, j: (i, 0, 0))
    cache = pl.BlockSpec((1, H_B, D_HB, tk), lambda i, j: (i, 0, 0, j))
    kern = functools.partial(_attn_sample_kernel, n_pairs=n_pairs)
    return pl.pallas_call(
        kern,
        grid=(b, n_kt),
        in_specs=[row, row, cache, cache, row, row,
                  pl.BlockSpec((1, t, H_B), lambda i, j: (i, 0, 0)),
                  pl.BlockSpec((1, H_B, tk), lambda i, j: (i, 0, j)),
                  pl.BlockSpec((1, H_B, LANES), lambda i, j: (i, 0, past // LANES))],
        out_specs=row,
        out_shape=jax.ShapeDtypeStruct((b, t, hd), BF16),
        scratch_shapes=[pltpu.VMEM((n_pairs, 2, t, LANES), F32), pltpu.VMEM((n_pairs, 2, t, LANES), F32),
                        pltpu.VMEM((n_pairs, 2 * t, LANES), F32)],
        compiler_params=pltpu.CompilerParams(
            dimension_semantics=("arbitrary", "arbitrary"), vmem_limit_bytes=VMEM_LIMIT),
        name="fox_attn_sample",
    )(q, g, k_cache, v_cache, k_new, v_new, f_new, f_t, f_t)


def _trunk(x, state0, k_past, v_past, logf_past, w):
    batch, seq, d = x.shape
    ntok = batch * seq
    hd = H_B * D_HB
    x2d = x.reshape(ntok, d)
    s0 = None if state0 is None else state0[:, 0]
    o_a, state = _hgrn(x2d, batch, seq, w["norm_a"], w["w_in"], w["lb_logits_h"], w["g_norm_a"], s0)
    x2d = _post(x2d, o_a, w["w_o_a"], w["norm_mlp0"], w["w_up"], w["w_down"], 0)
    proj_w = (w["norm_kv"], w["norm_b"], w["w_k"], w["w_v"], w["w_f"], w["b_f"], w["w_qq"], w["w_qg"])
    if k_past is None:
        k_t, v_t, logf_t, g, qa, ka, va = _proj(x2d, batch, seq, True, *proj_w)
        k, v = jnp.transpose(k_t, (0, 3, 1, 2)), jnp.transpose(v_t, (0, 3, 1, 2))
        logf = jnp.swapaxes(logf_t, 1, 2)
        o_b = _attn_prompt(qa, ka, va, g.reshape(batch, seq, hd))
    else:
        k, v, logf, g, q, k_flat, v_flat = _proj(x2d, batch, seq, False, *proj_w)
        past = k_past.shape[1]
        total = past + seq
        padded = -(-total // LANES) * LANES
        lf_all = jnp.concatenate(
            [jnp.swapaxes(logf_past.astype(F32), 1, 2),
             jnp.swapaxes(logf.reshape(batch, seq, H_B), 1, 2),
             jnp.zeros((batch, H_B, padded - total), F32)], axis=2)
        f_t = _cumsum_time(lf_all.reshape(1, batch * H_B, padded)).reshape(batch, H_B, padded)
        o_b = _attn_sample(q.reshape(batch, seq, hd), g.reshape(batch, seq, hd),
                           jnp.transpose(k_past, (0, 2, 3, 1)).astype(F32),
                           jnp.transpose(v_past, (0, 2, 3, 1)).astype(F32),
                           k_flat.reshape(batch, seq, hd), v_flat.reshape(batch, seq, hd),
                           jnp.swapaxes(f_t[:, :, past:total], 1, 2), f_t, past)
    y = _post(x2d, o_b.reshape(ntok, hd), w["w_o_b"], w["norm_mlp1"], w["w_up"], w["w_down"], 1,
              w["norm_f"])
    return (y.reshape(batch, seq, d), state[:, None],
            k.reshape(batch, seq, H_B, D_HB), v.reshape(batch, seq, H_B, D_HB),
            logf.reshape(batch, seq, H_B))


def kernel(x_prompt, x_sample, state_hgrn, cache_k, cache_v, cache_logf, norm_a, w_in_a, lb_logits, g_norm_a, w_o_a, norm_kv, w_kv, b_f, norm_b, w_q_b, w_o_b, norm_mlp, w_up, w_down, norm_f):
    d = x_prompt.shape[-1]
    assert w_in_a.shape[0] == 1 and w_q_b.shape[0] == 1, "one HGRN2 layer and one FoX layer"
    n_heads = w_in_a.shape[2] // (4 * K_A)
    hd = H_B * D_HB
    row = lambda a: a.reshape(1, -1).astype(F32)
    w_f = jnp.zeros((d, LANES), F32).at[:, :H_B].set(w_kv[:, 2 * hd:])
    w = {
        "norm_a": row(norm_a[0]),
        "w_in": w_in_a[0].astype(BF16),
        "lb_logits_h": lb_logits.astype(F32).reshape(lb_logits.shape[0], n_heads, 1, K_A),
        "g_norm_a": row(g_norm_a[0]),
        "w_o_a": w_o_a[0].astype(BF16),
        "norm_mlp0": row(norm_mlp[0]), "norm_mlp1": row(norm_mlp[1]),
        "w_up": w_up.astype(BF16), "w_down": w_down.astype(BF16),
        "norm_kv": row(norm_kv), "norm_b": row(norm_b[0]),
        "w_k": w_kv[:, :hd].astype(BF16), "w_v": w_kv[:, hd:2 * hd].astype(BF16),
        "w_f": w_f.astype(BF16), "b_f": jnp.zeros((1, LANES), F32).at[0, :H_B].set(b_f.astype(F32)),
        "w_qq": w_q_b[0][:, :hd].astype(BF16), "w_qg": w_q_b[0][:, hd:].astype(BF16),
        "w_o_b": w_o_b[0].astype(BF16),
        "norm_f": row(norm_f),
    }
    y_p, st_p, k_p, v_p, lf_p = _trunk(x_prompt, None, None, None, None, w)
    y_s, st_s, k_s, v_s, lf_s = _trunk(x_sample, state_hgrn, cache_k, cache_v, cache_logf, w)
    return (y_p, y_s, st_p, k_p, v_p, lf_p, st_s, k_s, v_s, lf_s)
```

```python
import functools

import jax
import jax.numpy as jnp
from jax import lax
from jax.experimental import pallas as pl
from jax.experimental.pallas import tpu as pltpu

EPS = 1e-6
CHUNK = 64
K_A = 128
V_A = 128
H_B = 16
D_HB = 64
LANES = 128
VMEM_LIMIT = 56 * 1024 * 1024

F32 = jnp.float32
BF16 = jnp.bfloat16


def _dot(a, b):
    return jnp.dot(a, b, preferred_element_type=F32)


def _dot_nt(a, b):
    return lax.dot_general(a, b, (((1,), (1,)), ((), ())), preferred_element_type=F32)


def _dot_tn(a, b):
    return lax.dot_general(a, b, (((0,), (0,)), ((), ())), preferred_element_type=F32)


def _sigmoid(x):
    return 0.5 * jnp.tanh(0.5 * x) + 0.5


def _const_spec(shape):
    nd = len(shape)
    return pl.BlockSpec(shape, lambda *_: (0,) * nd, pipeline_mode=pl.Buffered(1))


def _hgrn_kernel(*refs, n_heads, n_chunks, carry):
    if carry:
        (x_ref, nrm_ref, win_ref, lbl_ref, gn_ref, o_ref, sout_ref, xn_scr, st_scr) = refs
        s0_ref = None
    else:
        (x_ref, nrm_ref, win_ref, lbl_ref, gn_ref, s0_ref, o_ref, sout_ref, xn_scr) = refs
        st_scr = None
    tm = n_chunks * CHUNK
    n_sub = x_ref.shape[0] // tm

    def normalise(sub):
        x = x_ref[sub * tm:(sub + 1) * tm, :]
        inv = lax.rsqrt(jnp.mean(x * x, axis=-1, keepdims=True) + EPS)
        xn_scr[sub] = (x * inv * nrm_ref[...]).astype(BF16)

    if carry:
        @pl.when(pl.program_id(1) == 0)
        def _():
            st_scr[...] = jnp.zeros_like(st_scr)

    row_in_chunk = lax.broadcasted_iota(jnp.int32, (tm, K_A), 0) % CHUNK
    tri = (lax.broadcasted_iota(jnp.int32, (CHUNK, CHUNK), 1)
           <= lax.broadcasted_iota(jnp.int32, (CHUNK, CHUNK), 0))

    d_a = n_heads * K_A

    def project(sub, pair):
        xn = xn_scr[sub]
        return [_dot(xn, win_ref[:, part * d_a + pair * 2 * K_A:part * d_a + (pair + 1) * 2 * K_A])
                for part in range(4)]

    def gates(h, p4):
        half = slice((h % 2) * K_A, (h % 2 + 1) * K_A)
        pq, pz, pi, pg = (p[:, half] for p in p4)
        n_rows = lbl_ref.shape[0]
        lrows = [lbl_ref[r, h] for r in range(n_rows)]
        lmax = functools.reduce(jnp.maximum, lrows)
        lexp = [jnp.exp(l - lmax) for l in lrows]
        lb = lexp[0] / functools.reduce(lambda a, b: a + b, lexp)

        q = pq * _sigmoid(pq)
        f = lb + (1.0 - lb) * _sigmoid(pz)
        logf = jnp.log(f)
        k = 1.0 - f
        bc = logf
        s = 1
        while s < CHUNK:
            bc = bc + jnp.where(row_in_chunk >= s, pltpu.roll(bc, s, axis=0), 0.0)
            s *= 2
        chunked = lambda a: a.reshape(n_chunks, CHUNK, K_A)
        bc3 = chunked(bc)
        btot = bc3[:, CHUNK - 1:CHUNK, :]
        q_dec = chunked(q * jnp.exp(bc)).astype(BF16)
        k_inv = chunked(k * jnp.exp(-bc)).astype(BF16)
        k_end = (chunked(k) * jnp.exp(btot - bc3)).astype(BF16)
        decay = jnp.exp(btot)
        v = chunked(pi).astype(BF16)
        return q_dec, k_inv, k_end, decay, v, _sigmoid(pg)

    def recur(sub, heads, staged_heads):
        rows = slice(sub * tm, (sub + 1) * tm)
        cat = lambda i: jnp.concatenate([s[i] for s in staged_heads], axis=0)
        q_dec, k_inv, k_end, v = cat(0), cat(1), cat(2), cat(4)
        sc = jnp.einsum('cqk,csk->cqs', q_dec, k_inv, preferred_element_type=F32)
        sc = jnp.where(tri, sc, 0.0).astype(BF16)
        o_intra = jnp.einsum('cqs,csv->cqv', sc, v, preferred_element_type=F32)
        upd = jnp.einsum('csv,csk->cvk', v, k_end, preferred_element_type=F32)
        states = []
        for i, h in enumerate(heads):
            decay = staged_heads[i][3]
            if carry:
                st = st_scr[h]
                for c in range(n_chunks):
                    states.append(st)
                    st = st * decay[c] + upd[i * n_chunks + c]
                st_scr[h] = st
                sout_ref[0, h] = st.T
            else:
                for c in range(n_chunks):
                    st = s0_ref[c, h].T
                    states.append(st)
                    sout_ref[c, h] = (st * decay[c] + upd[i * n_chunks + c]).T
        s_in = jnp.stack(states, axis=0).astype(BF16)
        o_inter = jnp.einsum('cqk,cvk->cqv', q_dec, s_in, preferred_element_type=F32)
        o_all = o_intra + o_inter
        for i, h in enumerate(heads):
            o = o_all[i * n_chunks:(i + 1) * n_chunks].reshape(tm, V_A)
            o = o * lax.rsqrt(jnp.mean(o * o, axis=-1, keepdims=True) + EPS) * gn_ref[...]
            o_ref[rows, h * V_A:(h + 1) * V_A] = (o * staged_heads[i][5]).astype(BF16)

    items = [(sub, pair) for sub in range(n_sub) for pair in range(n_heads // 2)]
    normalise(0)
    p_next = project(*items[0])
    staged = None
    for i, (sub, pair) in enumerate(items):
        p_cur = p_next
        if i + 1 < len(items):
            if items[i + 1][0] != sub:
                normalise(items[i + 1][0])
            p_next = project(*items[i + 1])
        if staged is not None:
            recur(*staged)
        staged = (sub, (2 * pair, 2 * pair + 1), [gates(2 * pair, p_cur), gates(2 * pair + 1, p_cur)])
    recur(*staged)


def _hgrn(x2d, batch, seq, norm_w, w_in, lb_logits_h, g_norm, s0):
    ntok, d = x2d.shape
    n_heads = w_in.shape[1] // (4 * K_A)
    carry = s0 is None
    if carry:
        sub = min(seq, 512)
        tm = min(seq, 2 * sub)
        n_t = seq // tm
        grid = (batch, n_t)
        tok_map = lambda b, t: (b * n_t + t, 0)
        st_spec = pl.BlockSpec((1, n_heads, K_A, V_A), lambda b, t: (b, 0, 0, 0))
        scratch = [pltpu.VMEM((tm // sub, sub, d), BF16), pltpu.VMEM((n_heads, V_A, K_A), F32)]
        extra_in, extra_specs = [], []
    else:
        assert seq == CHUNK
        bt = min(batch, 8)
        sub = tm = bt * CHUNK
        grid = (batch // bt, 1)
        tok_map = lambda b, t: (b, 0)
        st_spec = pl.BlockSpec((bt, n_heads, K_A, V_A), lambda b, t: (b, 0, 0, 0))
        scratch = [pltpu.VMEM((1, tm, d), BF16)]
        extra_in, extra_specs = [s0], [st_spec]
    n_chunks = sub // CHUNK
    kern = functools.partial(_hgrn_kernel, n_heads=n_heads, n_chunks=n_chunks, carry=carry)
    return pl.pallas_call(
        kern,
        grid=grid,
        in_specs=[pl.BlockSpec((tm, d), tok_map),
                  _const_spec(norm_w.shape), _const_spec(w_in.shape),
                  _const_spec(lb_logits_h.shape), _const_spec(g_norm.shape)] + extra_specs,
        out_specs=[pl.BlockSpec((tm, n_heads * V_A), tok_map), st_spec],
        out_shape=[jax.ShapeDtypeStruct((ntok, n_heads * V_A), BF16),
                   jax.ShapeDtypeStruct((batch, n_heads, K_A, V_A), F32)],
        scratch_shapes=scratch,
        compiler_params=pltpu.CompilerParams(
            dimension_semantics=("arbitrary", "arbitrary"), vmem_limit_bytes=VMEM_LIMIT),
        name="hgrn_mixer",
    )(x2d, norm_w, w_in, lb_logits_h, g_norm, *extra_in)


def _post_kernel(*refs, ff_block, final_norm):
    if final_norm:
        x_ref, o_ref, wo_ref, nm_ref, wup_ref, wdn_ref, nf_ref, y_ref = refs
    else:
        x_ref, o_ref, wo_ref, nm_ref, wup_ref, wdn_ref, y_ref = refs
    x1 = x_ref[...] + _dot(o_ref[...], wo_ref[...])
    inv = lax.rsqrt(jnp.mean(x1 * x1, axis=-1, keepdims=True) + EPS)
    xn = (x1 * inv * nm_ref[...]).astype(BF16)
    acc = x1
    d_ff = wup_ref.shape[1]
    for j in range(d_ff // ff_block):
        hcol = jnp.maximum(_dot(xn, wup_ref[:, j * ff_block:(j + 1) * ff_block]), 0.0)
        acc = acc + _dot((hcol * hcol).astype(BF16), wdn_ref[j * ff_block:(j + 1) * ff_block, :])
    if final_norm:
        inv = lax.rsqrt(jnp.mean(acc * acc, axis=-1, keepdims=True) + EPS)
        acc = acc * inv * nf_ref[...]
    y_ref[...] = acc


def _layer_spec(stacked, layer):
    return pl.BlockSpec((None,) + stacked.shape[1:], lambda *_: (layer, 0, 0),
                        pipeline_mode=pl.Buffered(1))


def _post(x2d, o2d, w_o, norm_mlp, w_up, w_down, layer, norm_f=None):
    ntok, d = x2d.shape
    tm = min(ntok, 1024)
    final_norm = norm_f is not None
    tok = lambda i: (i, 0)
    ins = [x2d, o2d, w_o, norm_mlp, w_up, w_down] + ([norm_f] if final_norm else [])
    specs = [pl.BlockSpec((tm, d), tok), pl.BlockSpec((tm, o2d.shape[1]), tok),
             _const_spec(w_o.shape), _const_spec(norm_mlp.shape),
             _layer_spec(w_up, layer), _layer_spec(w_down, layer)]
    specs += [_const_spec(norm_f.shape)] if final_norm else []
    kern = functools.partial(_post_kernel, ff_block=1024, final_norm=final_norm)
    return pl.pallas_call(
        kern,
        grid=(ntok // tm,),
        in_specs=specs,
        out_specs=pl.BlockSpec((tm, d), tok),
        out_shape=jax.ShapeDtypeStruct((ntok, d), F32),
        compiler_params=pltpu.CompilerParams(
            dimension_semantics=("arbitrary",), vmem_limit_bytes=VMEM_LIMIT),
        name="post_mlp",
    )(*ins)


LOG2E = 1.4426950408889634
AUG_BASE = (D_HB, 0)


def _split3(x):
    h1 = x.astype(BF16).astype(F32)
    r1 = x - h1
    h2 = r1.astype(BF16).astype(F32)
    h3 = (r1 - h2).astype(BF16).astype(F32)
    return h1, h2, h3


def _head_pitch(tm):
    return tm + 8 if (tm // 8) % 2 == 0 else tm


def _store_head_major(val, scr, out_ref):
    tm = val.shape[0]
    pitch = _head_pitch(tm)
    for h in range(H_B):
        pair = val[:, (h // 2) * LANES:(h // 2 + 1) * LANES]
        scr[h * pitch:h * pitch + tm, :] = pair if h % 2 == 0 else pltpu.roll(pair, D_HB, axis=1)

    for t in range(tm):
        for grp in range(H_B // 8):
            rows = scr[pl.ds(grp * 8 * pitch + t, 8, stride=pitch), :]
            out_ref[t, grp * 8:(grp + 1) * 8, :] = rows[:, :D_HB]


def _proj_kernel(*refs, aug, sub_tiles):
    if aug:
        (x_ref, nkv_ref, nb_ref, wk_ref, wv_ref, wf_ref, bf_ref, wq_ref, wg_ref,
         k_ref, v_ref, lf_ref, g_ref, qa_ref, ka_ref, va_ref, carry) = refs
    else:
        (x_ref, nkv_ref, nb_ref, wk_ref, wv_ref, wf_ref, bf_ref, wq_ref, wg_ref,
         k_ref, v_ref, lf_ref, g_ref, q_ref, kf_ref, vf_ref, k_scr, v_scr) = refs
    tm = x_ref.shape[0]
    if not aug:
        x = x_ref[...]
        xs = x * lax.rsqrt(jnp.mean(x * x, axis=-1, keepdims=True) + EPS)
        xk = (xs * nkv_ref[...]).astype(BF16)
        xq = (xs * nb_ref[...]).astype(BF16)
        z = _dot(xk, wf_ref[...]) + bf_ref[...]
        lf = jnp.minimum(z, 0.0) - jnp.log(1.0 + jnp.exp(-jnp.abs(z)))
        k = _dot(xk, wk_ref[...])
        v = _dot(xk, wv_ref[...])
        g_ref[...] = _dot(xq, wg_ref[...])
        _store_head_major(k, k_scr, k_ref)
        _store_head_major(v, v_scr, v_ref)
        lf_ref[...] = lf[:, :H_B]
        q_ref[...] = _dot(xq, wq_ref[...])
        kf_ref[...] = k
        vf_ref[...] = v
        return

    @pl.when(pl.program_id(1) == 0)
    def _():
        carry[...] = jnp.zeros_like(carry)

    ts = tm // sub_tiles
    row = lax.broadcasted_iota(jnp.int32, (ts, LANES), 0)
    lane = lax.broadcasted_iota(jnp.int32, (ts, LANES), 1)
    row_t = lax.broadcasted_iota(jnp.int32, (LANES, ts), 0)
    low = lane < D_HB

    def sub_tile(r0, f_in):
        rows = slice(r0, r0 + ts)
        x = x_ref[rows, :]
        xs = x * lax.rsqrt(jnp.mean(x * x, axis=-1, keepdims=True) + EPS)
        xk = (xs * nkv_ref[...]).astype(BF16)
        xq = (xs * nb_ref[...]).astype(BF16)
        z = _dot(xk, wf_ref[...]) + bf_ref[...]
        lf = jnp.minimum(z, 0.0) - jnp.log(1.0 + jnp.exp(-jnp.abs(z)))
        lf_ref[0, :, rows] = lf.T[:H_B, :]
        f = lf
        s = 1
        while s < ts:
            f = f + jnp.where(row >= s, pltpu.roll(f, s, axis=0), 0.0)
            s *= 2
        f = f + f_in
        h1, h2, h3 = _split3(f * LOG2E)
        pieces = jnp.where(lane < H_B, h1,
                           jnp.where(lane < 2 * H_B, pltpu.roll(h2, H_B, axis=1),
                                     jnp.where(lane < 3 * H_B, pltpu.roll(h3, 2 * H_B, axis=1), 0.0)))
        both = pieces + pltpu.roll(pieces, D_HB, axis=1)
        shifted = pltpu.roll(both, 1, axis=1)
        neg_both = -both

        def tails(h):
            base = AUG_BASE[h % 2]
            at = (lane == base + h) | (lane == base + H_B + h) | (lane == base + 2 * H_B + h)
            up = ((lane == base + h + 1) | (lane == base + H_B + h + 1)
                  | (lane == base + 2 * H_B + h + 1))
            q_tail = jnp.where(up, shifted, jnp.where(at, 1.0, 0.0))
            k_tail = jnp.where(at, neg_both, jnp.where(up, 1.0, 0.0))
            return q_tail, k_tail

        def put(ref, h, data, tail):
            ref[h, rows, :] = (jnp.where(low, data, tail) if h % 2 == 0
                               else jnp.where(low, tail, data)).astype(BF16)

        k = _dot(xk, wk_ref[...])
        k_ref[0, :, :, rows] = k.T.reshape(H_B, D_HB, ts)
        for h in range(H_B):
            put(ka_ref, h, k[:, (h // 2) * LANES:(h // 2 + 1) * LANES], tails(h)[1])

        v_t = _dot(xk, wv_ref[...]).T
        v_ref[0, :, :, rows] = v_t.reshape(H_B, D_HB, ts)
        for j in range(H_B // 2):
            v_pair = v_t[j * LANES:(j + 1) * LANES, :]
            va_ref[2 * j, :, rows] = jnp.where(row_t < D_HB, v_pair,
                                               jnp.where(row_t == AUG_BASE[0], 1.0, 0.0)).astype(BF16)
            va_ref[2 * j + 1, :, rows] = jnp.where(row_t < D_HB,
                                                   jnp.where(row_t == AUG_BASE[1], 1.0, 0.0),
                                                   v_pair).astype(BF16)

        q = _dot(xq, wq_ref[...]) * ((D_HB ** -0.5) * LOG2E)
        for h in range(H_B):
            put(qa_ref, h, q[:, (h // 2) * LANES:(h // 2 + 1) * LANES], tails(h)[0])

        g_ref[rows, :] = _dot(xq, wg_ref[...])
        return f[ts - 1:ts, :]

    f_run = carry[...]
    for i in range(sub_tiles):
        f_run = sub_tile(i * ts, f_run)
    carry[...] = f_run


def _proj(x2d, batch, seq, aug, norm_kv, norm_b, w_k, w_v, w_f, b_f, w_qq, w_qg):
    ntok, d = x2d.shape
    hd = w_k.shape[1]
    ins = [x2d, norm_kv, norm_b, w_k, w_v, w_f, b_f, w_qq, w_qg]
    if aug:
        tm = min(seq, 512)
        n_t = seq // tm
        grid = (batch, n_t)
        tok = lambda b, t: (b * n_t + t, 0)
        head_major = lambda b, t: (0, b * n_t + t, 0)
        time_minor = pl.BlockSpec((1, H_B, D_HB, tm), lambda b, t: (b, 0, 0, t))
        out_specs = [time_minor, time_minor, pl.BlockSpec((1, H_B, tm), lambda b, t: (b, 0, t)),
                     pl.BlockSpec((tm, hd), tok),
                     pl.BlockSpec((H_B, tm, LANES), head_major),
                     pl.BlockSpec((H_B, tm, LANES), head_major),
                     pl.BlockSpec((H_B, LANES, tm), lambda b, t: (0, 0, b * n_t + t))]
        out_shape = ([jax.ShapeDtypeStruct((batch, H_B, D_HB, seq), F32)] * 2
                     + [jax.ShapeDtypeStruct((batch, H_B, seq), F32),
                        jax.ShapeDtypeStruct((ntok, hd), F32),
                        jax.ShapeDtypeStruct((H_B, ntok, LANES), BF16),
                        jax.ShapeDtypeStruct((H_B, ntok, LANES), BF16),
                        jax.ShapeDtypeStruct((H_B, LANES, ntok), BF16)])
        scratch = [pltpu.VMEM((1, LANES), F32)]
    else:
        tm = min(ntok, 512)
        n_t = ntok // tm
        grid = (1, n_t)
        tok = lambda b, t: (t, 0)
        tok4 = lambda b, t: (t, 0, 0)
        out_specs = ([pl.BlockSpec((tm, H_B, D_HB), tok4)] * 2 + [pl.BlockSpec((tm, H_B), tok)]
                     + [pl.BlockSpec((tm, hd), tok)] * 4)
        out_shape = ([jax.ShapeDtypeStruct((ntok, H_B, D_HB), F32)] * 2
                     + [jax.ShapeDtypeStruct((ntok, H_B), F32)]
                     + [jax.ShapeDtypeStruct((ntok, hd), F32)] * 4)
        scratch = [pltpu.VMEM((H_B * _head_pitch(tm), LANES), F32)] * 2
    return pl.pallas_call(
        functools.partial(_proj_kernel, aug=aug, sub_tiles=2 if tm % 512 == 0 else 1),
        grid=grid,
        in_specs=[pl.BlockSpec((tm, d), tok)] + [_const_spec(a.shape) for a in ins[1:]],
        out_specs=out_specs,
        out_shape=out_shape,
        scratch_shapes=scratch,
        compiler_params=pltpu.CompilerParams(
            dimension_semantics=("arbitrary", "arbitrary"), vmem_limit_bytes=VMEM_LIMIT),
        name="kvq_proj",
    )(*ins)


def _cumsum_kernel(lf_ref, f_ref):
    n_blk = lf_ref.shape[2] // LANES
    upper = (lax.broadcasted_iota(jnp.int32, (LANES, LANES), 0)
             <= lax.broadcasted_iota(jnp.int32, (LANES, LANES), 1)).astype(BF16)
    run = jnp.zeros((lf_ref.shape[1], 1), F32)
    for c in range(n_blk):
        x = lf_ref[0, :, c * LANES:(c + 1) * LANES]
        h1 = x.astype(BF16)
        r1 = x - h1.astype(F32)
        h2 = r1.astype(BF16)
        h3 = (r1 - h2.astype(F32)).astype(BF16)
        cs = (_dot(h1, upper) + _dot(h2, upper)) + _dot(h3, upper) + run
        f_ref[0, :, c * LANES:(c + 1) * LANES] = cs
        run = cs[:, LANES - 1:LANES]


def _cumsum_time(lf_t):
    b, h, l = lf_t.shape
    return pl.pallas_call(
        _cumsum_kernel,
        grid=(b,),
        in_specs=[pl.BlockSpec((1, h, l), lambda i: (i, 0, 0))],
        out_specs=pl.BlockSpec((1, h, l), lambda i: (i, 0, 0)),
        out_shape=jax.ShapeDtypeStruct((b, h, l), F32),
        compiler_params=pltpu.CompilerParams(dimension_semantics=("arbitrary",)),
        name="logf_cumsum",
    )(lf_t)


def _head_column(f_tile, head):
    lane = lax.broadcasted_iota(jnp.int32, f_tile.shape, 1)
    return jnp.sum(jnp.where(lane == head, f_tile, 0.0), axis=-1, keepdims=True)


def _attn_prompt_kernel(qa_ref, ka_ref, vat_ref, g_ref, o_ref, m_scr, acc_scr, *, tq):
    seq = qa_ref.shape[1]
    n_q = seq // tq
    half = tq // 2
    row_o = lax.broadcasted_iota(jnp.int32, (LANES, tq), 0)

    def visible(keys, queries):
        return (lax.broadcasted_iota(jnp.int32, (keys, queries), 0)
                <= lax.broadcasted_iota(jnp.int32, (keys, queries), 1))

    items = []
    for pair in range(qa_ref.shape[0] // 2):
        for qt in range(n_q):
            tile = (pair, qt)
            for kt in range(qt):
                items.append((tile, 0, tq, kt * tq, tq, None))
            items.append((tile, 0, tq, qt * tq, half, visible(half, tq)))
            items.append((tile, half, half, qt * tq + half, half, visible(half, half)))

    def scores(item):
        (pair, qt), q_lo, q_len, c0, c_len, _ = item
        return [_dot_nt(ka_ref[2 * pair + hf, c0:c0 + c_len, :],
                        qa_ref[2 * pair + hf, qt * tq + q_lo:qt * tq + q_lo + q_len, :])
                for hf in range(2)]

    def update(item, s_pair):
        (pair, _), q_lo, q_len, c0, c_len, mask = item
        cols = slice(q_lo, q_lo + q_len)
        for hf in range(2):
            st = s_pair[hf]
            if mask is not None:
                st = jnp.where(mask, st, -jnp.inf)
            m_old = m_scr[hf, :, cols]
            m_new = jnp.maximum(m_old, jnp.max(st, axis=0, keepdims=True))
            pt = jnp.exp2(st - m_new).astype(BF16)
            acc_scr[hf, :, cols] = (jnp.exp2(m_old - m_new) * acc_scr[hf, :, cols]
                                    + _dot(vat_ref[2 * pair + hf, :, c0:c0 + c_len], pt))
            m_scr[hf, :, cols] = m_new

    def finish(tile):
        pair, qt = tile
        rows, cols = slice(qt * tq, (qt + 1) * tq), slice(pair * LANES, (pair + 1) * LANES)
        outs = []
        for hf in range(2):
            acc = acc_scr[hf]
            outs.append(acc / acc[AUG_BASE[hf]:AUG_BASE[hf] + 1, :])
        o = jnp.where(row_o < D_HB, outs[0], outs[1]).T
        o = o * _sigmoid(g_ref[0, rows, cols])
        o_ref[0, rows, cols] = o.astype(BF16)

    s_next = scores(items[0])
    for i, item in enumerate(items):
        s_cur = s_next
        if i + 1 < len(items):
            s_next = scores(items[i + 1])
        if i == 0 or items[i - 1][0] != item[0]:
            m_scr[...] = jnp.full(m_scr.shape, -jnp.inf, F32)
            acc_scr[...] = jnp.zeros(acc_scr.shape, F32)
        update(item, s_cur)
        if i + 1 == len(items) or items[i + 1][0] != item[0]:
            finish(item[0])


def _attn_prompt(qa, ka, vat, g):
    b, t, hd = g.shape
    tq = min(t, 512)
    pairs = 4
    n_blk = hd // (pairs * LANES)
    heads = pl.BlockSpec((2 * pairs, t, LANES), lambda i, j: (j, i, 0))
    heads_t = pl.BlockSpec((2 * pairs, LANES, t), lambda i, j: (j, 0, i))
    col = pl.BlockSpec((1, t, pairs * LANES), lambda i, j: (i, 0, j))
    kern = functools.partial(_attn_prompt_kernel, tq=tq)
    return pl.pallas_call(
        kern,
        grid=(b, n_blk),
        in_specs=[heads, heads, heads_t, col],
        out_specs=col,
        out_shape=jax.ShapeDtypeStruct((b, t, hd), BF16),
        scratch_shapes=[pltpu.VMEM((2, 1, tq), F32), pltpu.VMEM((2, LANES, tq), F32)],
        compiler_params=pltpu.CompilerParams(
            dimension_semantics=("arbitrary", "arbitrary"), vmem_limit_bytes=VMEM_LIMIT),
        name="fox_attn_prompt",
    )(qa, ka, vat, g)


def _pair_block(qs, k2, v2, fq, fk, mask, m_ref, l_ref, acc_ref, keys_on_lanes=False):
    tq = qs.shape[0] // 2
    tk = k2.shape[1] if keys_on_lanes else k2.shape[0]
    width = min(tk, LANES)
    s = _dot(qs, k2) if keys_on_lanes else _dot_nt(qs, k2)
    alphas, probs = [], []
    for hf in range(2):
        sh = s[hf * tq:(hf + 1) * tq] + (fq[hf] - fk[hf])
        if mask is not None:
            sh = jnp.where(mask, sh, -jnp.inf)
        cols = [sh[:, c * width:(c + 1) * width] for c in range(tk // width)]
        m_old = m_ref[hf]
        m_new = jnp.maximum(
            m_old, jnp.max(functools.reduce(jnp.maximum, cols), axis=-1, keepdims=True))
        p_cols = [jnp.exp2(c - m_new[:, :width]) for c in cols]
        alpha = jnp.exp2(m_old - m_new)
        l_ref[hf] = alpha * l_ref[hf] + jnp.sum(
            functools.reduce(lambda a, b: a + b, p_cols), axis=-1, keepdims=True)
        m_ref[hf] = m_new
        probs.append(jnp.concatenate([p.astype(BF16) for p in p_cols], axis=1))
        alphas.append(alpha)
    p_all = jnp.concatenate(probs, axis=0)
    pv = _dot_nt(p_all, v2) if keys_on_lanes else _dot(p_all, v2)
    acc_ref[...] = jnp.concatenate(alphas, axis=0) * acc_ref[...] + pv


def _split_heads(q2, lane_a):
    zero = jnp.zeros_like(q2)
    return jnp.concatenate([jnp.where(lane_a, q2, zero), jnp.where(lane_a, zero, q2)], axis=0)


def _attn_sample_kernel(q_ref, g_ref, kc_ref, vc_ref, kn_ref, vn_ref, fq_ref, fkc_ref, fkn_ref,
                        o_ref, m_scr, l_scr, acc_scr, *, n_pairs):
    kt = pl.program_id(1)
    n_kt = pl.num_programs(1)
    tq = q_ref.shape[1]
    scale = (D_HB ** -0.5) * LOG2E
    lane_a = lax.broadcasted_iota(jnp.int32, (tq, LANES), 1) < D_HB
    causal = (lax.broadcasted_iota(jnp.int32, (tq, tq), 1)
              <= lax.broadcasted_iota(jnp.int32, (tq, tq), 0))
    f_tile = fq_ref[0] * LOG2E

    @pl.when(kt == 0)
    def _():
        m_scr[...] = jnp.full(m_scr.shape, -jnp.inf, F32)
        l_scr[...] = jnp.zeros(l_scr.shape, F32)
        acc_scr[...] = jnp.zeros(acc_scr.shape, F32)

    def cache_pair(ref, hp):
        return ref[0, 2 * hp:2 * hp + 2].reshape(2 * D_HB, ref.shape[3]).astype(BF16)

    def pair_inputs(hp):
        cols = slice(hp * LANES, (hp + 1) * LANES)
        qs = _split_heads(q_ref[0, :, cols] * scale, lane_a).astype(BF16)
        fq = [_head_column(f_tile, 2 * hp + hf) for hf in range(2)]
        return cols, qs, fq

    for hp in range(n_pairs):
        cols, qs, fq = pair_inputs(hp)
        fk = [fkc_ref[0, 2 * hp + hf:2 * hp + hf + 1, :] * LOG2E for hf in range(2)]
        _pair_block(qs, cache_pair(kc_ref, hp), cache_pair(vc_ref, hp),
                    fq, fk, None, m_scr.at[hp], l_scr.at[hp], acc_scr.at[hp], keys_on_lanes=True)

    @pl.when(kt == n_kt - 1)
    def _():
        for hp in range(n_pairs):
            cols, qs, fq = pair_inputs(hp)
            fk = [fkn_ref[0, 2 * hp + hf:2 * hp + hf + 1, 0:tq] * LOG2E for hf in range(2)]
            _pair_block(qs, kn_ref[0, :, cols].astype(BF16), vn_ref[0, :, cols].astype(BF16),
                        fq, fk, causal, m_scr.at[hp], l_scr.at[hp], acc_scr.at[hp])
            acc = acc_scr[hp]
            o = jnp.where(lane_a, acc[:tq] / l_scr[hp, 0], acc[tq:] / l_scr[hp, 1])
            o = o * _sigmoid(g_ref[0, :, cols])
            o_ref[0, :, cols] = o.astype(BF16)


def _attn_sample(q, g, k_cache, v_cache, k_new, v_new, f_new, f_t, past):
    b, t, hd = q.shape
    n_pairs = hd // LANES
    tk = min(past, 2048)
    n_kt = past // tk
    row = pl.BlockSpec((1, t, hd), lambda i, j: (i, 0, 0))
    cache = pl.BlockSpec((1, H_B, D_HB, tk), lambda i, j: (i, 0, 0, j))
    kern = functools.partial(_attn_sample_kernel, n_pairs=n_pairs)
    return pl.pallas_call(
        kern,
        grid=(b, n_kt),
        in_specs=[row, row, cache, cache, row, row,
                  pl.BlockSpec((1, t, H_B), lambda i, j: (i, 0, 0)),
                  pl.BlockSpec((1, H_B, tk), lambda i, j: (i, 0, j)),
                  pl.BlockSpec((1, H_B, LANES), lambda i, j: (i, 0, past // LANES))],
        out_specs=row,
        out_shape=jax.ShapeDtypeStruct((b, t, hd), BF16),
        scratch_shapes=[pltpu.VMEM((n_pairs, 2, t, LANES), F32), pltpu.VMEM((n_pairs, 2, t, LANES), F32),
                        pltpu.VMEM((n_pairs, 2 * t, LANES), F32)],
        compiler_params=pltpu.CompilerParams(
            dimension_semantics=("arbitrary", "arbitrary"), vmem_limit_bytes=VMEM_LIMIT),
        name="fox_attn_sample",
    )(q, g, k_cache, v_cache, k_new, v_new, f_new, f_t, f_t)


def _trunk(x, state0, k_past, v_past, logf_past, w):
    batch, seq, d = x.shape
    ntok = batch * seq
    hd = H_B * D_HB
    x2d = x.reshape(ntok, d)
    s0 = None if state0 is None else state0[:, 0]
    o_a, state = _hgrn(x2d, batch, seq, w["norm_a"], w["w_in"], w["lb_logits_h"], w["g_norm_a"], s0)
    x2d = _post(x2d, o_a, w["w_o_a"], w["norm_mlp0"], w["w_up"], w["w_down"], 0)
    proj_w = (w["norm_kv"], w["norm_b"], w["w_k"], w["w_v"], w["w_f"], w["b_f"], w["w_qq"], w["w_qg"])
    if k_past is None:
        k_t, v_t, logf_t, g, qa, ka, va = _proj(x2d, batch, seq, True, *proj_w)
        k, v = jnp.transpose(k_t, (0, 3, 1, 2)), jnp.transpose(v_t, (0, 3, 1, 2))
        logf = jnp.swapaxes(logf_t, 1, 2)
        o_b = _attn_prompt(qa, ka, va, g.reshape(batch, seq, hd))
    else:
        k, v, logf, g, q, k_flat, v_flat = _proj(x2d, batch, seq, False, *proj_w)
        past = k_past.shape[1]
        total = past + seq
        padded = -(-total // LANES) * LANES
        lf_all = jnp.concatenate(
            [jnp.swapaxes(logf_past.astype(F32), 1, 2),
             jnp.swapaxes(logf.reshape(batch, seq, H_B), 1, 2),
             jnp.zeros((batch, H_B, padded - total), F32)], axis=2)
        f_t = _cumsum_time(lf_all.reshape(1, batch * H_B, padded)).reshape(batch, H_B, padded)
        o_b = _attn_sample(q.reshape(batch, seq, hd), g.reshape(batch, seq, hd),
                           jnp.transpose(k_past, (0, 2, 3, 1)).astype(F32),
                           jnp.transpose(v_past, (0, 2, 3, 1)).astype(F32),
                           k_flat.reshape(batch, seq, hd), v_flat.reshape(batch, seq, hd),
                           jnp.swapaxes(f_t[:, :, past:total], 1, 2), f_t, past)
    y = _post(x2d, o_b.reshape(ntok, hd), w["w_o_b"], w["norm_mlp1"], w["w_up"], w["w_down"], 1,
              w["norm_f"])
    return (y.reshape(batch, seq, d), state[:, None],
            k.reshape(batch, seq, H_B, D_HB), v.reshape(batch, seq, H_B, D_HB),
            logf.reshape(batch, seq, H_B))


def kernel(x_prompt, x_sample, state_hgrn, cache_k, cache_v, cache_logf, norm_a, w_in_a, lb_logits, g_norm_a, w_o_a, norm_kv, w_kv, b_f, norm_b, w_q_b, w_o_b, norm_mlp, w_up, w_down, norm_f):
    d = x_prompt.shape[-1]
    assert w_in_a.shape[0] == 1 and w_q_b.shape[0] == 1, "one HGRN2 layer and one FoX layer"
    n_heads = w_in_a.shape[2] // (4 * K_A)
    hd = H_B * D_HB
    row = lambda a: a.reshape(1, -1).astype(F32)
    w_f = jnp.zeros((d, LANES), F32).at[:, :H_B].set(w_kv[:, 2 * hd:])
    w = {
        "norm_a": row(norm_a[0]),
        "w_in": w_in_a[0].astype(BF16),
        "lb_logits_h": lb_logits.astype(F32).reshape(lb_logits.shape[0], n_heads, 1, K_A),
        "g_norm_a": row(g_norm_a[0]),
        "w_o_a": w_o_a[0].astype(BF16),
        "norm_mlp0": row(norm_mlp[0]), "norm_mlp1": row(norm_mlp[1]),
        "w_up": w_up.astype(BF16), "w_down": w_down.astype(BF16),
        "norm_kv": row(norm_kv), "norm_b": row(norm_b[0]),
        "w_k": w_kv[:, :hd].astype(BF16), "w_v": w_kv[:, hd:2 * hd].astype(BF16),
        "w_f": w_f.astype(BF16), "b_f": jnp.zeros((1, LANES), F32).at[0, :H_B].set(b_f.astype(F32)),
        "w_qq": w_q_b[0][:, :hd].astype(BF16), "w_qg": w_q_b[0][:, hd:].astype(BF16),
        "w_o_b": w_o_b[0].astype(BF16),
        "norm_f": row(norm_f),
    }
    y_p, st_p, k_p, v_p, lf_p = _trunk(x_prompt, None, None, None, None, w)
    y_s, st_s, k_s, v_s, lf_s = _trunk(x_sample, state_hgrn, cache_k, cache_v, cache_logf, w)
    return (y_p, y_s, st_p, k_p, v_p, lf_p, st_s, k_s, v_s, lf_s)
```
